```python
import math
import jax
import jax.numpy as jnp
from jax import lax
import numpy as np

D_MODEL = 1024
BATCH = 4
SEQ = 8192
DEPTH = 2

NORM_EPS = 1e-6
NEG_INF = -1e30
TOP_BONUS = 1e9
NUM_BUCKETS = 32
T5_MAX_DISTANCE = 128
NSA_HEADS = 8
NSA_KV_GROUPS = 2
NSA_HEAD_DIM = 64
CMP_LEN = 32
CMP_STRIDE = 16
SEL_BLOCK = 64
SEL_TOPK = 16
WINDOW = 512
NSA_Q_BLOCK = 128
NSA_WIDTH = NSA_HEADS * NSA_HEAD_DIM
NSA_KV_WIDTH = NSA_KV_GROUPS * NSA_HEAD_DIM
N_NSA_BRANCH = 3
GLA_HEADS = 4
GLA_KEY_DIM = 64
GLA_VAL_DIM = 128
GLA_GATE_RANK = 16
GLA_GATE_TAU = 16.0
GLA_CHUNK = 64
MOBA_HEADS = 8
MOBA_HEAD_DIM = 64
MOBA_BLOCK = 256
MOBA_TOPK = 3
MOBA_Q_BLOCK = 32
MOBA_WIDTH = MOBA_HEADS * MOBA_HEAD_DIM
N_BRANCHES = 3
FFN_HIDDEN = ((8 * D_MODEL + 3 * 256 - 1) // (3 * 256)) * 256
SPLIT_SIZES = (
    NSA_WIDTH,
    NSA_KV_WIDTH, NSA_KV_WIDTH,
    NSA_KV_WIDTH, NSA_KV_WIDTH,
    NSA_KV_WIDTH, NSA_KV_WIDTH,
    NSA_HEADS * N_NSA_BRANCH,
    GLA_HEADS * GLA_KEY_DIM, GLA_HEADS * GLA_KEY_DIM, GLA_HEADS * GLA_VAL_DIM, GLA_HEADS * GLA_VAL_DIM, GLA_GATE_RANK,
    MOBA_WIDTH, MOBA_WIDTH, MOBA_WIDTH,
    N_BRANCHES * D_MODEL,
)
D_IN = sum(SPLIT_SIZES)

kernel_name = 'hybrid_nsa_gla_moba_block'


def rmsnorm(x, w):
    xf = x.astype(jnp.float32)
    y = xf * lax.rsqrt(jnp.mean(xf * xf, axis=-1, keepdims=True) + NORM_EPS)
    return (y * w.astype(jnp.float32)).astype(x.dtype)


def t5_bucket(dist):
    n = jnp.maximum(dist, 0)
    max_exact = NUM_BUCKETS // 2
    log_ratio = jnp.log(jnp.maximum(n, 1).astype(jnp.float32) / max_exact) / math.log(T5_MAX_DISTANCE / max_exact)
    large = max_exact + (log_ratio * (NUM_BUCKETS - max_exact)).astype(jnp.int32)
    return jnp.where(n < max_exact, n, jnp.minimum(large, NUM_BUCKETS - 1))


def masked_softmax(s, mask):
    s = jnp.where(mask, s.astype(jnp.float32), NEG_INF)
    m = jnp.max(s, axis=-1, keepdims=True)
    e = jnp.where(mask, jnp.exp(s - m), 0.0)
    return e / jnp.maximum(jnp.sum(e, axis=-1, keepdims=True), 1e-20)


def split_heads(t, n_heads):
    b, s, _ = t.shape
    return t.reshape(b, s, n_heads, -1).transpose(0, 2, 1, 3)


def compress(t, pe, w1, w2):
    b, g, s, d = t.shape
    ratio = CMP_LEN // CMP_STRIDE
    n_cmp = (s - CMP_LEN) // CMP_STRIDE + 1
    pieces = t.reshape(b, g, s // CMP_STRIDE, CMP_STRIDE, d)
    blocks = jnp.concatenate([pieces[:, :, m:m + n_cmp] for m in range(ratio)], axis=3)
    flat = (blocks + pe).reshape(b, g, n_cmp, CMP_LEN * d)
    return jax.nn.gelu(flat @ w1) @ w2


def nsa_mixer(q, kc, vc, ks, vs, kw, vw, gates, rel_a, pe_k, pe_v, ck_w1, ck_w2, cv_w1, cv_w2):
    b, s, _ = q.shape
    G, R, dk = NSA_KV_GROUPS, NSA_HEADS // NSA_KV_GROUPS, NSA_HEAD_DIM
    qh = q.reshape(b, s, G, R, dk).transpose(0, 2, 3, 1, 4) * dk ** -0.5
    gh = jax.nn.sigmoid(gates).reshape(b, s, G, R, N_NSA_BRANCH).transpose(0, 2, 3, 1, 4)
    kc, vc, ks, vs, kw, vw = [split_heads(t, G) for t in (kc, vc, ks, vs, kw, vw)]
    k_cmp = compress(kc, pe_k, ck_w1, ck_w2)
    v_cmp = compress(vc, pe_v, cv_w1, cv_w2)
    n_cmp = k_cmp.shape[2]
    cmp_end = jnp.arange(n_cmp) * CMP_STRIDE + CMP_LEN - 1
    cmp_start = cmp_end - (CMP_LEN - 1)
    n_sb = s // SEL_BLOCK
    n_sel = min(SEL_TOPK, n_sb)
    sb_start = jnp.arange(n_sb) * SEL_BLOCK
    overlap = ((cmp_start[:, None] < sb_start[None, :] + SEL_BLOCK) & (cmp_end[:, None] >= sb_start[None, :])).astype(jnp.float32)
    ks_blk = ks.reshape(b, G, n_sb, SEL_BLOCK, dk)
    vs_blk = vs.reshape(b, G, n_sb, SEL_BLOCK, dk)
    pad = ((0, 0), (0, 0), (WINDOW, 0), (0, 0))
    kw_pad = jnp.pad(kw, pad)
    vw_pad = jnp.pad(vw, pad)
    rel_g = rel_a.reshape(NUM_BUCKETS, G, R)
    bi = jnp.arange(b)[:, None, None, None]
    gi = jnp.arange(G)[None, :, None, None]
    Q = NSA_Q_BLOCK

    def chunk(c):
        t0 = c * Q
        tpos = t0 + jnp.arange(Q)
        qc = lax.dynamic_slice_in_dim(qh, t0, Q, axis=3)
        gc = lax.dynamic_slice_in_dim(gh, t0, Q, axis=3)
        d_cmp = tpos[:, None] - cmp_end[None, :]
        b_cmp = rel_a[t5_bucket(d_cmp)].reshape(Q, n_cmp, G, R).transpose(2, 3, 0, 1)
        s_cmp = jnp.einsum('bgrqd,bgnd->bgrqn', qc, k_cmp) + b_cmp
        p_cmp = masked_softmax(s_cmp, d_cmp >= 0)
        o_cmp = jnp.einsum('bgrqn,bgnd->bgrqd', p_cmp.astype(v_cmp.dtype), v_cmp)
        imp = jnp.einsum('bgrqn,ns->bgqs', p_cmp, overlap)
        cur = tpos // SEL_BLOCK
        sb = jnp.arange(n_sb)[None, :]
        forced = (sb == 0) | (sb == cur[:, None]) | (sb == cur[:, None] - 1)
        imp = jnp.where(forced, TOP_BONUS, jnp.where(sb <= cur[:, None], imp, NEG_INF))
        _, idx = lax.top_k(imp, n_sel)
        k_sel = ks_blk[bi, gi, idx].reshape(b, G, Q, n_sel * SEL_BLOCK, dk)
        v_sel = vs_blk[bi, gi, idx].reshape(b, G, Q, n_sel * SEL_BLOCK, dk)
        kpos = (idx[..., None] * SEL_BLOCK + jnp.arange(SEL_BLOCK)).reshape(b, G, Q, n_sel * SEL_BLOCK)
        d_sel = tpos[None, None, :, None] - kpos
        b_sel = jnp.moveaxis(rel_g[t5_bucket(d_sel), gi], -1, 2)
        s_sel = jnp.einsum('bgrqd,bgqkd->bgrqk', qc, k_sel) + b_sel
        p_sel = masked_softmax(s_sel, (d_sel >= 0)[:, :, None])
        o_sel = jnp.einsum('bgrqk,bgqkd->bgrqd', p_sel.astype(v_sel.dtype), v_sel)
        k_w = lax.dynamic_slice_in_dim(kw_pad, t0, Q + WINDOW, axis=2)
        v_w = lax.dynamic_slice_in_dim(vw_pad, t0, Q + WINDOW, axis=2)
        wpos = t0 - WINDOW + jnp.arange(Q + WINDOW)
        d_w = tpos[:, None] - wpos[None, :]
        m_w = (d_w >= 0) & (d_w < WINDOW) & (wpos[None, :] >= 0)
        b_w = rel_a[t5_bucket(d_w)].reshape(Q, Q + WINDOW, G, R).transpose(2, 3, 0, 1)
        s_w = jnp.einsum('bgrqd,bgkd->bgrqk', qc, k_w) + b_w
        p_w = masked_softmax(s_w, m_w)
        o_w = jnp.einsum('bgrqk,bgkd->bgrqd', p_w.astype(v_w.dtype), v_w)
        return gc[..., 0:1] * o_cmp + gc[..., 1:2] * o_sel + gc[..., 2:3] * o_w

    out = lax.map(chunk, jnp.arange(s // Q))
    return out.transpose(1, 0, 4, 2, 3, 5).reshape(b, s, NSA_WIDTH)


def gla_mixer(q, k, v, log_a, r, norm_w):
    b, s, _ = q.shape
    H, DK, DV, C = GLA_HEADS, GLA_KEY_DIM, GLA_VAL_DIM, GLA_CHUNK
    n_c = s // C

    def to_chunks(t, d):
        return t.reshape(b, n_c, C, H, d).transpose(1, 0, 3, 2, 4).astype(jnp.float32)

    qc = to_chunks(q * DK ** -0.5, DK)
    kc = to_chunks(k, DK)
    vc = to_chunks(v, DV)
    gc = to_chunks(log_a, DK)
    causal = jnp.tril(jnp.ones((C, C), dtype=bool))[:, :, None]

    def step(state, inp):
        qi, ki, vi, gi = inp
        bcum = jnp.cumsum(gi, axis=2)
        o_inter = jnp.einsum('bhcd,bhde->bhce', qi * jnp.exp(bcum), state)
        diff = bcum[:, :, :, None, :] - bcum[:, :, None, :, :]
        decay = jnp.exp(jnp.where(causal, diff, NEG_INF))
        attn = jnp.einsum('bhid,bhjd,bhijd->bhij', qi, ki, decay)
        o_intra = jnp.einsum('bhij,bhje->bhie', attn, vi)
        b_last = bcum[:, :, -1:, :]
        new_state = state * jnp.exp(b_last[:, :, 0, :])[..., None] + jnp.einsum('bhcd,bhce->bhde', ki * jnp.exp(b_last - bcum), vi)
        return new_state, o_inter + o_intra

    state0 = jnp.zeros((b, H, DK, DV), jnp.float32)
    _, o = lax.scan(step, state0, (qc, kc, vc, gc))
    o = o.transpose(1, 0, 3, 2, 4).reshape(b, s, H, DV)
    o = rmsnorm(o, norm_w) * jax.nn.silu(r.reshape(b, s, H, DV).astype(jnp.float32))
    return o.reshape(b, s, H * DV).astype(r.dtype)


def moba_mixer(q, k, v, rel_c):
    b, s, _ = q.shape
    H, dh, Q = MOBA_HEADS, MOBA_HEAD_DIM, MOBA_Q_BLOCK
    qh = split_heads(q, H) * dh ** -0.5
    kh = split_heads(k, H)
    vh = split_heads(v, H)
    n_blk = -(-s // MOBA_BLOCK)
    pad = ((0, 0), (0, 0), (0, n_blk * MOBA_BLOCK - s), (0, 0))
    kp = jnp.pad(kh, pad)
    vp = jnp.pad(vh, pad)
    k_blk = kp.reshape(b, H, n_blk, MOBA_BLOCK, dh)
    v_blk = vp.reshape(b, H, n_blk, MOBA_BLOCK, dh)
    k_mean = jnp.mean(k_blk.astype(jnp.float32), axis=3)
    n_top = min(MOBA_TOPK, n_blk)
    ks_len = n_top * MOBA_BLOCK
    bi = jnp.arange(b)[:, None, None, None]
    hi = jnp.arange(H)[None, :, None, None]

    def chunk(c):
        t0 = c * Q
        tpos = t0 + jnp.arange(Q)
        qb = lax.dynamic_slice_in_dim(qh, t0, Q, axis=2)
        cur = t0 // MOBA_BLOCK
        score = jnp.einsum('bhqd,bhnd->bhqn', qb.astype(jnp.float32), k_mean)
        score = jnp.where(jnp.arange(n_blk) < cur, score, NEG_INF)
        _, idx = lax.top_k(score, n_top)
        sel_valid = idx < cur
        k_sel = k_blk[bi, hi, idx].reshape(b, H, Q, ks_len, dh)
        v_sel = v_blk[bi, hi, idx].reshape(b, H, Q, ks_len, dh)
        kpos = (idx[..., None] * MOBA_BLOCK + jnp.arange(MOBA_BLOCK)).reshape(b, H, Q, ks_len)
        s_sel = jnp.einsum('bhqd,bhqkd->bhqk', qb, k_sel) + rel_c[t5_bucket(tpos[None, None, :, None] - kpos), hi]
        m_sel = jnp.repeat(sel_valid, MOBA_BLOCK, axis=-1)
        k_own = lax.dynamic_slice_in_dim(kp, cur * MOBA_BLOCK, MOBA_BLOCK, axis=2)
        v_own = lax.dynamic_slice_in_dim(vp, cur * MOBA_BLOCK, MOBA_BLOCK, axis=2)
        d_own = tpos[:, None] - (cur * MOBA_BLOCK + jnp.arange(MOBA_BLOCK))[None, :]
        s_own = jnp.einsum('bhqd,bhkd->bhqk', qb, k_own) + rel_c[t5_bucket(d_own)].transpose(2, 0, 1)
        m_own = jnp.broadcast_to(d_own >= 0, (b, H, Q, MOBA_BLOCK))
        p = masked_softmax(jnp.concatenate([s_sel, s_own], axis=-1), jnp.concatenate([m_sel, m_own], axis=-1)).astype(v.dtype)
        return jnp.einsum('bhqk,bhqkd->bhqd', p[..., :ks_len], v_sel) + jnp.einsum('bhqk,bhkd->bhqd', p[..., ks_len:], v_own)

    out = lax.map(chunk, jnp.arange(s // Q))
    return out.transpose(1, 0, 3, 2, 4).reshape(b, s, MOBA_WIDTH)


def setup_inputs(seed: int = 0) -> dict:
    key = jax.random.key(seed)
    ks = jax.random.split(key, 22)
    L, dk = DEPTH, NSA_HEAD_DIM

    def nrm(k, shape, scale):
        return jax.random.normal(k, shape, jnp.float32) * scale

    return {
        'x': nrm(ks[0], (BATCH, SEQ, D_MODEL), 1.0),
        'rel_bias': nrm(ks[1], (NUM_BUCKETS, NSA_HEADS + MOBA_HEADS), 0.2),
        'norm_mix_pre': 1.0 + nrm(ks[2], (L, D_MODEL), 0.02),
        'norm_mix_post': 1.0 + nrm(ks[3], (L, D_MODEL), 0.02),
        'norm_ffn_pre': 1.0 + nrm(ks[4], (L, D_MODEL), 0.02),
        'norm_ffn_post': 1.0 + nrm(ks[5], (L, D_MODEL), 0.02),
        'w_in': nrm(ks[6], (L, D_MODEL, D_IN), D_MODEL ** -0.5),
        'nsa_pe_k': nrm(ks[7], (L, CMP_LEN, dk), 0.1),
        'nsa_pe_v': nrm(ks[8], (L, CMP_LEN, dk), 0.1),
        'nsa_cmp_k_w1': nrm(ks[9], (L, CMP_LEN * dk, dk), (CMP_LEN * dk) ** -0.5),
        'nsa_cmp_k_w2': nrm(ks[10], (L, dk, dk), dk ** -0.5),
        'nsa_cmp_v_w1': nrm(ks[11], (L, CMP_LEN * dk, dk), (CMP_LEN * dk) ** -0.5),
        'nsa_cmp_v_w2': nrm(ks[12], (L, dk, dk), dk ** -0.5),
        'gla_gate_w2': nrm(ks[13], (L, GLA_GATE_RANK, GLA_HEADS * GLA_KEY_DIM), GLA_GATE_RANK ** -0.5),
        'gla_gate_b': nrm(ks[14], (L, GLA_HEADS * GLA_KEY_DIM), 0.1),
        'gla_norm': 1.0 + nrm(ks[15], (L, GLA_VAL_DIM), 0.02),
        'w_branch_a': nrm(ks[16], (L, NSA_WIDTH, D_MODEL), NSA_WIDTH ** -0.5),
        'w_branch_b': nrm(ks[17], (L, GLA_HEADS * GLA_VAL_DIM, D_MODEL), (GLA_HEADS * GLA_VAL_DIM) ** -0.5),
        'w_branch_c': nrm(ks[18], (L, MOBA_WIDTH, D_MODEL), MOBA_WIDTH ** -0.5),
        'w_out': nrm(ks[19], (L, D_MODEL, D_MODEL), D_MODEL ** -0.5),
        'w_ffn_in': nrm(ks[20], (L, D_MODEL, 2 * FFN_HIDDEN), D_MODEL ** -0.5),
        'w_ffn_out': nrm(ks[21], (L, FFN_HIDDEN, D_MODEL), FFN_HIDDEN ** -0.5),
    }


def reference(x, rel_bias, norm_mix_pre, norm_mix_post, norm_ffn_pre, norm_ffn_post, w_in,
              nsa_pe_k, nsa_pe_v, nsa_cmp_k_w1, nsa_cmp_k_w2, nsa_cmp_v_w1, nsa_cmp_v_w2,
              gla_gate_w2, gla_gate_b, gla_norm, w_branch_a, w_branch_b, w_branch_c, w_out,
              w_ffn_in, w_ffn_out):
    b, s, _ = x.shape
    split_points = np.cumsum(SPLIT_SIZES)[:-1].tolist()
    rel_a = rel_bias[:, :NSA_HEADS]
    rel_c = rel_bias[:, NSA_HEADS:]
    for layer in range(DEPTH):
        h = rmsnorm(x, norm_mix_pre[layer])
        (a_q, a_kc, a_vc, a_ks, a_vs, a_kw, a_vw, a_g,
         b_q, b_k, b_v, b_r, b_lr, c_q, c_k, c_v, mg) = jnp.split(h @ w_in[layer], split_points, axis=-1)
        o_a = nsa_mixer(a_q, a_kc, a_vc, a_ks, a_vs, a_kw, a_vw, a_g, rel_a,
                        nsa_pe_k[layer], nsa_pe_v[layer], nsa_cmp_k_w1[layer], nsa_cmp_k_w2[layer],
                        nsa_cmp_v_w1[layer], nsa_cmp_v_w2[layer])
        log_a = jax.nn.log_sigmoid((b_lr @ gla_gate_w2[layer] + gla_gate_b[layer]).astype(jnp.float32)) / GLA_GATE_TAU
        o_b = gla_mixer(b_q, b_k, b_v, log_a, b_r, gla_norm[layer])
        o_c = moba_mixer(c_q, c_k, c_v, rel_c)
        g = jax.nn.sigmoid(mg).reshape(b, s, N_BRANCHES, D_MODEL)
        merged = (g[:, :, 0] * (o_a @ w_branch_a[layer])
                  + g[:, :, 1] * (o_b @ w_branch_b[layer])
                  + g[:, :, 2] * (o_c @ w_branch_c[layer]))
        x = x + rmsnorm(merged @ w_out[layer], norm_mix_post[layer])
        h = rmsnorm(x, norm_ffn_pre[layer])
        gate, up = jnp.split(h @ w_ffn_in[layer], 2, axis=-1)
        x = x + rmsnorm((jax.nn.silu(gate) * up) @ w_ffn_out[layer], norm_ffn_post[layer])
    return x
```

```python
import functools
import math

import numpy as np
import jax
import jax.numpy as jnp
from jax import lax
from jax.experimental import pallas as pl
from jax.experimental.pallas import tpu as pltpu

F32 = jnp.float32
BF16 = jnp.bfloat16

NORM_EPS = 1e-6
NEG_INF = -1e30
TOP_BONUS = 1e9
NUM_BUCKETS = 32
T5_MAX_DISTANCE = 128
NSA_HEADS = 8
NSA_KV_GROUPS = 2
NSA_REP = NSA_HEADS // NSA_KV_GROUPS
NSA_HEAD_DIM = 64
CMP_LEN = 32
CMP_STRIDE = 16
SEL_BLOCK = 64
SEL_TOPK = 16
WINDOW = 512
N_NSA_BRANCH = 3
GLA_HEADS = 4
GLA_KEY_DIM = 64
GLA_VAL_DIM = 128
GLA_GATE_RANK = 16
GLA_GATE_TAU = 16.0
GLA_CHUNK = 64
MOBA_HEADS = 8
MOBA_HEAD_DIM = 64
MOBA_BLOCK = 256
MOBA_TOPK = 3
N_BRANCHES = 3

LANES = 128
ATT_TILE = 256
VMEM_LIMIT = 56 * 1024 * 1024

D_MODEL = 1024
SEC = {}
_off = 0
for _name, _w in (("mg", 3 * D_MODEL), ("a_q", 512), ("c_q", 512), ("c_k", 512), ("c_v", 512),
                  ("b_v", 512), ("b_r", 512), ("b_q", 256), ("b_k", 256),
                  ("a_kc", 128), ("a_vc", 128), ("a_ks", 128), ("a_vs", 128), ("a_kw", 128), ("a_vw", 128),
                  ("a_g", 128), ("b_lr", 128)):
    SEC[_name] = (_off, _w)
    _off += _w
NP_COLS = _off


def _src_columns():
    sizes = (512, 128, 128, 128, 128, 128, 128, 24, 256, 256, 512, 512, 16, 512, 512, 512, 3 * D_MODEL)
    names = ("a_q", "a_kc", "a_vc", "a_ks", "a_vs", "a_kw", "a_vw", "a_g", "b_q", "b_k", "b_v", "b_r", "b_lr",
             "c_q", "c_k", "c_v", "mg")
    src0 = dict(zip(names, np.cumsum((0,) + sizes[:-1]).tolist()))
    width = dict(zip(names, sizes))
    cols = np.full((NP_COLS,), -1, np.int64)
    for name in names:
        off, _ = SEC[name]
        if name == "a_q":
            for r in range(NSA_REP):
                for g in range(NSA_KV_GROUPS):
                    d0 = off + r * LANES + g * NSA_HEAD_DIM
                    s0 = src0[name] + (g * NSA_REP + r) * NSA_HEAD_DIM
                    cols[d0:d0 + NSA_HEAD_DIM] = np.arange(s0, s0 + NSA_HEAD_DIM)
        else:
            cols[off:off + width[name]] = np.arange(src0[name], src0[name] + width[name])
    return cols, int(sum(sizes))


def _cparams(sem):
    return pltpu.CompilerParams(dimension_semantics=sem, vmem_limit_bytes=VMEM_LIMIT)


def _t5_bucket(dist):
    n = jnp.maximum(dist, 0)
    max_exact = NUM_BUCKETS // 2
    log_ratio = jnp.log(jnp.maximum(n, 1).astype(F32) / max_exact) / math.log(T5_MAX_DISTANCE / max_exact)
    large = max_exact + (log_ratio * (NUM_BUCKETS - max_exact)).astype(jnp.int32)
    return jnp.where(n < max_exact, n, jnp.minimum(large, NUM_BUCKETS - 1))


def _rms(y, w):
    return y * lax.rsqrt(jnp.mean(y * y, axis=-1, keepdims=True) + NORM_EPS) * w


def _norm_matmul_kernel(x_ref, nw_ref, w_ref, o_ref, h_ref):
    @pl.when(pl.program_id(1) == 0)
    def _():
        h_ref[...] = _rms(x_ref[...], nw_ref[...]).astype(BF16)

    o_ref[...] = jnp.dot(h_ref[...], w_ref[...], preferred_element_type=F32).astype(o_ref.dtype)


def _norm_matmul(x, nw, w, tm, tn):
    t, d = x.shape
    n = w.shape[1]
    return pl.pallas_call(
        _norm_matmul_kernel,
        grid=(t // tm, n // tn),
        in_specs=[pl.BlockSpec((tm, d), lambda i, j: (i, 0)),
                  pl.BlockSpec((1, d), lambda i, j: (0, 0)),
                  pl.BlockSpec((d, tn), lambda i, j: (0, j))],
        out_specs=pl.BlockSpec((tm, tn), lambda i, j: (i, j)),
        out_shape=jax.ShapeDtypeStruct((t, n), BF16),
        scratch_shapes=[pltpu.VMEM((tm, d), BF16)],
        compiler_params=_cparams(("parallel", "arbitrary")),
        name="norm_proj",
    )(x, nw.reshape(1, d), w)


def _compress_kernel(x_ref, pe_ref, w1_ref, w2_ref, o_ref):
    x = x_ref[...]
    nc = x.shape[0]
    w1t = w1_ref[0]
    w1b = w1_ref[1]
    a = jnp.dot(x, w1t, preferred_element_type=F32)
    b = jnp.dot(x, w1b, preferred_element_type=F32)
    pe = pe_ref[...]
    pe_term = (jnp.dot(pe[0], w1t, preferred_element_type=F32)
               + jnp.dot(pe[1], w1b, preferred_element_type=F32))[0:1]
    pre = a + pltpu.roll(b, nc - 1, 0) + pe_term
    hid = jax.nn.gelu(pre)
    o_ref[...] = jnp.dot(hid.astype(BF16), w2_ref[...], preferred_element_type=F32).astype(o_ref.dtype)


def _compress(xkv, pe, w1, w2):
    _, b, nc, kw = xkv.shape
    return pl.pallas_call(
        _compress_kernel,
        grid=(2, b),
        in_specs=[pl.BlockSpec((None, None, nc, kw), lambda s, i: (s, i, 0, 0)),
                  pl.BlockSpec((None, 2, 8, kw), lambda s, i: (s, 0, 0, 0)),
                  pl.BlockSpec((None, 2, kw, LANES), lambda s, i: (s, 0, 0, 0)),
                  pl.BlockSpec((None, LANES, LANES), lambda s, i: (s, 0, 0))],
        out_specs=pl.BlockSpec((None, None, nc, LANES), lambda s, i: (s, i, 0, 0)),
        out_shape=jax.ShapeDtypeStruct((2, b, nc, LANES), BF16),
        compiler_params=_cparams(("parallel", "parallel")),
        name="nsa_compress",
    )(xkv, pe, w1, w2)


def _flash_init(m_ref, l_ref, acc_ref):
    m_ref[...] = jnp.full(m_ref.shape, NEG_INF, F32)
    l_ref[...] = jnp.zeros(l_ref.shape, F32)
    acc_ref[...] = jnp.zeros(acc_ref.shape, F32)


def _flash_step(s, v, m_ref, l_ref, acc_ref):
    m_prev = m_ref[...]
    m_new = jnp.maximum(m_prev, jnp.max(s, axis=-1, keepdims=True))
    alpha = jnp.exp(m_prev - m_new)
    p = jnp.exp(s - jnp.tile(m_new, (1, s.shape[1] // LANES)))
    l_ref[...] = alpha * l_ref[...] + jnp.sum(p, axis=-1, keepdims=True)
    acc_ref[...] = alpha * acc_ref[...] + jnp.dot(p.astype(BF16), v, preferred_element_type=F32)
    m_ref[...] = m_new


def _qk(lhs, rhs):
    return lax.dot_general(lhs, rhs, (((1,), (1,)), ((), ())), preferred_element_type=F32)


def _topk_mask(vals, k):
    lane = lax.broadcasted_iota(jnp.int32, vals.shape, 1).astype(F32)
    sel = jnp.zeros(vals.shape, jnp.bool_)
    for _ in range(k):
        m = jnp.max(vals, axis=-1, keepdims=True)
        idx = jnp.min(jnp.where(vals == m, lane, float(LANES)), axis=-1, keepdims=True)
        hit = lane == idx
        sel = jnp.logical_or(sel, hit)
        vals = jnp.where(hit, -jnp.inf, vals)
    return sel


def _split3_dot(p, w):
    hi = p.astype(BF16)
    r1 = p - hi.astype(F32)
    mid = r1.astype(BF16)
    lo = (r1 - mid.astype(F32)).astype(BF16)
    return (jnp.dot(hi, w, preferred_element_type=F32) + jnp.dot(mid, w, preferred_element_type=F32)
            + jnp.dot(lo, w, preferred_element_type=F32))


def _nsa_kernel(q_ref, ks_ref, vs_ref, kw_ref, vw_ref, g_ref, kc_ref, vc_ref, ov_ref, tc_ref, dn_ref,
                o_ref, lhs_ref, m_ref, l_ref, acc_ref, out_ref):
    tq = ATT_TILE
    tk = ATT_TILE
    nh = NSA_HEADS
    rows = nh * tq
    c = pl.program_id(1)
    t0 = c * tq
    ncp = kc_ref.shape[0]

    lane = lax.broadcasted_iota(jnp.int32, (tq, LANES), 1)
    rowi = lax.broadcasted_iota(jnp.int32, (tq, LANES), 0)
    half = lane // NSA_HEAD_DIM

    q = q_ref[...] * jnp.asarray(NSA_HEAD_DIM ** -0.5, BF16)
    for g in range(NSA_KV_GROUPS):
        for r in range(NSA_REP):
            h = g * NSA_REP + r
            qb = q[:, r * LANES:(r + 1) * LANES]
            lhs_ref[h * tq:(h + 1) * tq, 0:LANES] = jnp.where(half == g, qb, jnp.zeros_like(qb))

    gates = jax.nn.sigmoid(g_ref[...].astype(F32))

    def gate_col(h, br):
        col = h * N_NSA_BRANCH + br
        return gates[:, col:col + 1]

    kc = kc_ref[...]
    vc = vc_ref[...]
    ci = lax.broadcasted_iota(jnp.int32, (tq, ncp), 1)
    ri = lax.broadcasted_iota(jnp.int32, (tq, ncp), 0)
    valid = (ri + (t0 - (CMP_LEN - 1))) >= ci * CMP_STRIDE
    shift = lax.rem(c * (tq // CMP_STRIDE) + (ncp - tq // CMP_STRIDE), ncp)
    cur = (rowi + t0) // SEL_BLOCK
    forced = (lane == 0) | (lane == cur) | (lane == cur - 1)
    for g in range(NSA_KV_GROUPS):
        psum = jnp.zeros((tq, ncp), F32)
        for r in range(NSA_REP):
            h = g * NSA_REP + r
            s = _qk(lhs_ref[h * tq:(h + 1) * tq, 0:LANES], kc)
            s = s + pltpu.roll(tc_ref[h], shift, 1)
            s = jnp.where(valid, s, NEG_INF)
            m = jnp.max(s, axis=-1, keepdims=True)
            e = jnp.where(valid, jnp.exp(s - m), 0.0)
            p = e / jnp.maximum(jnp.sum(e, axis=-1, keepdims=True), 1e-20)
            psum = psum + p
            o_cmp = jnp.dot(p.astype(BF16), vc, preferred_element_type=F32)
            out_ref[h * tq:(h + 1) * tq, :] = gate_col(h, 0) * o_cmp
        imp = _split3_dot(psum, ov_ref[...])
        imp = jnp.where(forced, TOP_BONUS, jnp.where(lane <= cur, imp, NEG_INF))
        sel = _topk_mask(imp, SEL_TOPK)
        negmask = jnp.where(sel, 0.0, NEG_INF).astype(BF16)
        for r in range(NSA_REP):
            h = g * NSA_REP + r
            lhs_ref[h * tq:(h + 1) * tq, LANES:2 * LANES] = negmask

    rq = lax.broadcasted_iota(jnp.int32, (rows, tk), 0) % tq
    ck = lax.broadcasted_iota(jnp.int32, (rows, tk), 1)
    causal = ck <= rq
    krow = lax.broadcasted_iota(jnp.int32, (tk, LANES), 0) // SEL_BLOCK
    klane = lax.broadcasted_iota(jnp.int32, (tk, LANES), 1)

    def sel_scores(j):
        start = pl.multiple_of(j * tk, tk)
        onehot = (klane == krow + j * (tk // SEL_BLOCK)).astype(BF16)
        rhs = jnp.concatenate([ks_ref[pl.ds(start, tk), :], onehot], axis=1)
        return _qk(lhs_ref[...], rhs), vs_ref[pl.ds(start, tk), :]

    def finish(br):
        o = acc_ref[...] / l_ref[...]
        for h in range(nh):
            sl = slice(h * tq, (h + 1) * tq)
            out_ref[sl, :] = out_ref[sl, :] + gate_col(h, br) * o[sl]

    _flash_init(m_ref, l_ref, acc_ref)

    def far_body(j, carry):
        s, v = sel_scores(j)
        _flash_step(s, v, m_ref, l_ref, acc_ref)
        return carry

    lax.fori_loop(0, jnp.maximum(c - 1, 0), far_body, 0)

    @pl.when(c >= 1)
    def _():
        s, v = sel_scores(c - 1)
        _flash_step(s + dn_ref[0], v, m_ref, l_ref, acc_ref)

    s, v = sel_scores(c)
    _flash_step(jnp.where(causal, s + dn_ref[1], NEG_INF), v, m_ref, l_ref, acc_ref)
    finish(1)

    _flash_init(m_ref, l_ref, acc_ref)

    def win_scores(j):
        start = pl.multiple_of(j * tk, tk)
        return _qk(lhs_ref[:, 0:LANES], kw_ref[pl.ds(start, tk), :]), vw_ref[pl.ds(start, tk), :]

    @pl.when(c >= 2)
    def _():
        s, v = win_scores(c - 2)
        _flash_step(jnp.where(ck > rq, s, NEG_INF), v, m_ref, l_ref, acc_ref)

    @pl.when(c >= 1)
    def _():
        s, v = win_scores(c - 1)
        _flash_step(s + dn_ref[0], v, m_ref, l_ref, acc_ref)

    s, v = win_scores(c)
    _flash_step(jnp.where(causal, s + dn_ref[1], NEG_INF), v, m_ref, l_ref, acc_ref)
    finish(2)

    for r in range(NSA_REP):
        o0 = out_ref[r * tq:(r + 1) * tq, :]
        o1 = out_ref[(NSA_REP + r) * tq:(NSA_REP + r + 1) * tq, :]
        o_ref[:, r * LANES:(r + 1) * LANES] = jnp.where(half == 0, o0, o1).astype(o_ref.dtype)


def _nsa_attention(proj, kcmp, vcmp, overlap, tcmp, dnear):
    b, s, _ = proj.shape
    tq = ATT_TILE
    ncp = kcmp.shape[1]
    rows = NSA_HEADS * tq

    def col(name, width):
        return SEC[name][0] // width

    full = lambda name: pl.BlockSpec((None, s, LANES), lambda i, c, n=name: (i, 0, col(n, LANES)))
    return pl.pallas_call(
        _nsa_kernel,
        grid=(b, s // tq),
        in_specs=[pl.BlockSpec((None, tq, 512), lambda i, c: (i, c, col("a_q", 512))),
                  full("a_ks"), full("a_vs"), full("a_kw"), full("a_vw"),
                  pl.BlockSpec((None, tq, LANES), lambda i, c: (i, c, col("a_g", LANES))),
                  pl.BlockSpec((None, ncp, LANES), lambda i, c: (i, 0, 0)),
                  pl.BlockSpec((None, ncp, LANES), lambda i, c: (i, 0, 0)),
                  pl.BlockSpec((ncp, LANES), lambda i, c: (0, 0)),
                  pl.BlockSpec((NSA_HEADS, tq, ncp), lambda i, c: (0, 0, 0)),
                  pl.BlockSpec((2, rows, tq), lambda i, c: (0, 0, 0))],
        out_specs=pl.BlockSpec((None, tq, 512), lambda i, c: (i, c, 0)),
        out_shape=jax.ShapeDtypeStruct((b, s, 512), BF16),
        scratch_shapes=[pltpu.VMEM((rows, 2 * LANES), BF16),
                        pltpu.VMEM((rows, LANES), F32), pltpu.VMEM((rows, LANES), F32),
                        pltpu.VMEM((rows, LANES), F32), pltpu.VMEM((rows, LANES), F32)],
        compiler_params=_cparams(("parallel", "arbitrary")),
        name="nsa_attention",
    )(proj, proj, proj, proj, proj, proj, kcmp, vcmp, overlap, tcmp, dnear)


def _gla_kernel(q_ref, k_ref, v_ref, r_ref, lr_ref, w2_ref, gb_ref, gn_ref, o_ref, st_ref):
    ch = GLA_CHUNK
    lb = q_ref.shape[0]
    hp = lax.Precision.HIGHEST

    @pl.when(pl.program_id(1) == 0)
    def _():
        st_ref[...] = jnp.zeros(st_ref.shape, F32)

    x = jnp.dot(lr_ref[...], w2_ref[...], preferred_element_type=F32) + gb_ref[...]
    log_a = (jnp.minimum(x, 0.0) - jnp.log1p(jnp.exp(-jnp.abs(x)))) / GLA_GATE_TAU

    ti = lax.broadcasted_iota(jnp.int32, (ch, ch), 0)
    tj = lax.broadcasted_iota(jnp.int32, (ch, ch), 1)
    tril = tj <= ti
    tri = tril.astype(BF16)
    half = lax.broadcasted_iota(jnp.int32, (ch, LANES), 1) // GLA_KEY_DIM
    eye = (lax.broadcasted_iota(jnp.int32, (LANES, LANES), 0)
           == lax.broadcasted_iota(jnp.int32, (LANES, LANES), 1))
    gn = gn_ref[...]

    for cc in range(lb // ch):
        sl = slice(cc * ch, (cc + 1) * ch)
        bcum = _split3_dot_left(tri, log_a[sl])
        for p in range(GLA_HEADS // 2):
            cols = slice(p * LANES, (p + 1) * LANES)
            bc = bcum[:, cols]
            blast = bc[ch - 1:ch, :]
            qf = q_ref[sl, cols].astype(F32) * (GLA_KEY_DIM ** -0.5)
            kf = k_ref[sl, cols].astype(F32)
            qe = qf * jnp.exp(bc)
            kinv = kf * jnp.exp(-bc)
            klast = kf * jnp.exp(blast - bc)
            decay = jnp.where(eye, jnp.broadcast_to(jnp.exp(blast), (LANES, LANES)), 0.0)
            for a in range(2):
                h = 2 * p + a
                hc = slice(h * LANES, (h + 1) * LANES)
                qa = jnp.where(half == a, qe, 0.0)
                attn = lax.dot_general(qa, kinv, (((1,), (1,)), ((), ())), precision=hp,
                                       preferred_element_type=F32)
                attn = jnp.where(tril, attn, 0.0)
                v = v_ref[sl, hc].astype(F32)
                st = st_ref[h]
                o = (jnp.dot(qa, st, precision=hp, preferred_element_type=F32)
                     + jnp.dot(attn, v, precision=hp, preferred_element_type=F32))
                st_ref[h] = (jnp.dot(decay, st, precision=hp, preferred_element_type=F32)
                             + lax.dot_general(klast, v, (((0,), (0,)), ((), ())), precision=hp,
                                               preferred_element_type=F32))
                rg = r_ref[sl, hc].astype(F32)
                o_ref[sl, hc] = (_rms(o, gn) * (rg * jax.nn.sigmoid(rg))).astype(o_ref.dtype)


def _split3_dot_left(w, x):
    hi = x.astype(BF16)
    r1 = x - hi.astype(F32)
    mid = r1.astype(BF16)
    lo = (r1 - mid.astype(F32)).astype(BF16)
    return (jnp.dot(w, hi, preferred_element_type=F32) + jnp.dot(w, mid, preferred_element_type=F32)
            + jnp.dot(w, lo, preferred_element_type=F32))


def _gla(proj, w2, gb, gn, lb):
    b, s, _ = proj.shape

    def spec(name, width):
        return pl.BlockSpec((None, lb, width), lambda i, c, n=name, w=width: (i, c, SEC[n][0] // w))

    return pl.pallas_call(
        _gla_kernel,
        grid=(b, s // lb),
        in_specs=[spec("b_q", 256), spec("b_k", 256), spec("b_v", 512), spec("b_r", 512), spec("b_lr", LANES),
                  pl.BlockSpec((LANES, 256), lambda i, c: (0, 0)),
                  pl.BlockSpec((1, 256), lambda i, c: (0, 0)),
                  pl.BlockSpec((1, LANES), lambda i, c: (0, 0))],
        out_specs=pl.BlockSpec((None, lb, 512), lambda i, c: (i, c, 0)),
        out_shape=jax.ShapeDtypeStruct((b, s, 512), BF16),
        scratch_shapes=[pltpu.VMEM((GLA_HEADS, LANES, LANES), F32)],
        compiler_params=_cparams(("parallel", "arbitrary")),
        name="gla",
    )(proj, proj, proj, proj, proj, w2, gb, gn)


def _kmean_kernel(a_ref, k_ref, o_ref):
    o_ref[...] = jnp.dot(a_ref[...], k_ref[...], preferred_element_type=F32).astype(o_ref.dtype)


def _moba_kmean(proj, avg):
    b, s, _ = proj.shape
    return pl.pallas_call(
        _kmean_kernel,
        grid=(b,),
        in_specs=[pl.BlockSpec((LANES, s), lambda i: (0, 0)),
                  pl.BlockSpec((None, s, 512), lambda i: (i, 0, SEC["c_k"][0] // 512))],
        out_specs=pl.BlockSpec((None, LANES, 512), lambda i: (i, 0, 0)),
        out_shape=jax.ShapeDtypeStruct((b, LANES, 512), BF16),
        compiler_params=_cparams(("parallel",)),
        name="moba_kmean",
    )(avg, proj)


def _moba_kernel(q_ref, k_ref, v_ref, km_ref, dn_ref, o_ref, lhs_ref, m_ref, l_ref, acc_ref):
    tq = ATT_TILE
    tk = ATT_TILE
    rows = 2 * tq
    c = pl.program_id(2)

    lane = lax.broadcasted_iota(jnp.int32, (tq, LANES), 1)
    half = lane // MOBA_HEAD_DIM
    q = q_ref[...] * jnp.asarray(MOBA_HEAD_DIM ** -0.5, BF16)
    for a in range(2):
        lhs_ref[a * tq:(a + 1) * tq, 0:LANES] = jnp.where(half == a, q, jnp.zeros_like(q))

    lane2 = lax.broadcasted_iota(jnp.int32, (rows, LANES), 1)
    score = _qk(lhs_ref[:, 0:LANES], km_ref[...])
    score = jnp.where(lane2 < c, score, NEG_INF)
    sel = _topk_mask(score, MOBA_TOPK)
    allowed = (sel & (lane2 < c)) | (lane2 == c)
    lhs_ref[:, LANES:2 * LANES] = jnp.where(allowed, 0.0, NEG_INF).astype(BF16)

    rq = lax.broadcasted_iota(jnp.int32, (rows, tk), 0) % tq
    ck = lax.broadcasted_iota(jnp.int32, (rows, tk), 1)
    klane = lax.broadcasted_iota(jnp.int32, (tk, LANES), 1)

    def scores(j):
        start = pl.multiple_of(j * tk, tk)
        onehot = (klane == j).astype(BF16)
        rhs = jnp.concatenate([k_ref[pl.ds(start, tk), :], onehot], axis=1)
        return _qk(lhs_ref[...], rhs), v_ref[pl.ds(start, tk), :]

    _flash_init(m_ref, l_ref, acc_ref)

    def far_body(j, carry):
        s, v = scores(j)
        _flash_step(s, v, m_ref, l_ref, acc_ref)
        return carry

    lax.fori_loop(0, jnp.maximum(c - 1, 0), far_body, 0)

    @pl.when(c >= 1)
    def _():
        s, v = scores(c - 1)
        _flash_step(s + dn_ref[0], v, m_ref, l_ref, acc_ref)

    s, v = scores(c)
    _flash_step(jnp.where(ck <= rq, s + dn_ref[1], NEG_INF), v, m_ref, l_ref, acc_ref)

    o = acc_ref[...] / l_ref[...]
    o_ref[...] = jnp.where(half == 0, o[0:tq], o[tq:2 * tq]).astype(o_ref.dtype)


def _moba_attention(proj, kmean, dnear):
    b, s, _ = proj.shape
    tq = ATT_TILE
    rows = 2 * tq
    npair = MOBA_HEADS // 2

    def col(name):
        return SEC[name][0] // LANES

    return pl.pallas_call(
        _moba_kernel,
        grid=(b, npair, s // tq),
        in_specs=[pl.BlockSpec((None, tq, LANES), lambda i, p, c: (i, c, col("c_q") + p)),
                  pl.BlockSpec((None, s, LANES), lambda i, p, c: (i, 0, col("c_k") + p)),
                  pl.BlockSpec((None, s, LANES), lambda i, p, c: (i, 0, col("c_v") + p)),
                  pl.BlockSpec((None, LANES, LANES), lambda i, p, c: (i, 0, p)),
                  pl.BlockSpec((None, 2, rows, tq), lambda i, p, c: (p, 0, 0, 0))],
        out_specs=pl.BlockSpec((None, tq, LANES), lambda i, p, c: (i, c, p)),
        out_shape=jax.ShapeDtypeStruct((b, s, 512), BF16),
        scratch_shapes=[pltpu.VMEM((rows, 2 * LANES), BF16),
                        pltpu.VMEM((rows, LANES), F32), pltpu.VMEM((rows, LANES), F32),
                        pltpu.VMEM((rows, LANES), F32)],
        compiler_params=_cparams(("parallel", "parallel", "arbitrary")),
        name="moba_attention",
    )(proj, proj, proj, kmean, dnear)


def _merge_kernel(oa_ref, ob_ref, oc_ref, g0_ref, g1_ref, g2_ref, x_ref, wa_ref, wb_ref, wc_ref, wo_ref, nw_ref,
                  o_ref):
    def branch(o, w, g):
        return jax.nn.sigmoid(g[...].astype(F32)) * jnp.dot(o[...], w[...], preferred_element_type=F32)

    merged = branch(oa_ref, wa_ref, g0_ref) + branch(ob_ref, wb_ref, g1_ref) + branch(oc_ref, wc_ref, g2_ref)
    y = jnp.dot(merged.astype(BF16), wo_ref[...], preferred_element_type=F32)
    o_ref[...] = x_ref[...] + _rms(y, nw_ref[...])


def _merge(oa, ob, oc, proj, x, wa, wb, wc, wo, nw, tm):
    t, d = x.shape
    w = oa.shape[1]
    row = lambda width, j=0: pl.BlockSpec((tm, width), lambda i, j=j: (i, j))
    const = lambda shape: pl.BlockSpec(shape, lambda i: (0, 0))
    return pl.pallas_call(
        _merge_kernel,
        grid=(t // tm,),
        in_specs=[row(w), row(w), row(w), row(d, 0), row(d, 1), row(d, 2), row(d),
                  const((w, d)), const((w, d)), const((w, d)), const((d, d)), const((1, d))],
        out_specs=row(d),
        out_shape=jax.ShapeDtypeStruct((t, d), F32),
        compiler_params=_cparams(("parallel",)),
        name="merge_out",
    )(oa, ob, oc, proj, proj, proj, x, wa, wb, wc, wo, nw.reshape(1, d))


def _ffn_kernel(x_ref, npre_ref, wg_ref, wu_ref, wo_ref, npost_ref, o_ref, h_ref, acc_ref):
    j = pl.program_id(1)

    @pl.when(j == 0)
    def _():
        h_ref[...] = _rms(x_ref[...], npre_ref[...]).astype(BF16)
        acc_ref[...] = jnp.zeros(acc_ref.shape, F32)

    h = h_ref[...]
    gate = jnp.dot(h, wg_ref[...], preferred_element_type=F32)
    up = jnp.dot(h, wu_ref[...], preferred_element_type=F32)
    act = (gate * jax.nn.sigmoid(gate) * up).astype(BF16)
    acc_ref[...] += jnp.dot(act, wo_ref[...], preferred_element_type=F32)

    @pl.when(j == pl.num_programs(1) - 1)
    def _():
        o_ref[...] = x_ref[...] + _rms(acc_ref[...], npost_ref[...])


def _ffn(x, npre, wg, wu, wo, npost, tm, th):
    t, d = x.shape
    hid = wg.shape[1]
    return pl.pallas_call(
        _ffn_kernel,
        grid=(t // tm, hid // th),
        in_specs=[pl.BlockSpec((tm, d), lambda i, j: (i, 0)),
                  pl.BlockSpec((1, d), lambda i, j: (0, 0)),
                  pl.BlockSpec((d, th), lambda i, j: (0, j)),
                  pl.BlockSpec((d, th), lambda i, j: (0, j)),
                  pl.BlockSpec((th, d), lambda i, j: (j, 0)),
                  pl.BlockSpec((1, d), lambda i, j: (0, 0))],
        out_specs=pl.BlockSpec((tm, d), lambda i, j: (i, 0)),
        out_shape=jax.ShapeDtypeStruct((t, d), F32),
        scratch_shapes=[pltpu.VMEM((tm, d), BF16), pltpu.VMEM((tm, d), F32)],
        compiler_params=_cparams(("parallel", "arbitrary")),
        name="ffn",
    )(x, npre.reshape(1, d), wg, wu, wo, npost.reshape(1, d))


def _near_bias(rel, t):
    qi = jnp.arange(t)[:, None]
    ki = jnp.arange(t)[None, :]
    far = rel[NUM_BUCKETS - 1]

    def tab(d):
        b = rel[_t5_bucket(d)] - far
        b = jnp.where((d >= 0)[..., None], b, 0.0)
        return b.transpose(2, 0, 1).reshape(-1, t)

    return jnp.stack([tab(qi + t - ki), tab(qi - ki)]).astype(F32)


def _cmp_bias(rel, t, ncp):
    w = t // CMP_STRIDE
    qi = jnp.arange(t)[:, None]
    j = jnp.arange(ncp)[None, :]
    d = qi - CMP_STRIDE * (j - w) - (CMP_LEN - 1)
    b = rel[_t5_bucket(d)] - rel[NUM_BUCKETS - 1]
    b = jnp.where(((d >= 0) & (j < 2 * w))[..., None], b, 0.0)
    return b.transpose(2, 0, 1).astype(F32)


def _overlap(s, nc):
    n_cmp = (s - CMP_LEN) // CMP_STRIDE + 1
    cmp_end = jnp.arange(nc) * CMP_STRIDE + CMP_LEN - 1
    cmp_start = cmp_end - (CMP_LEN - 1)
    sb_start = jnp.arange(LANES) * SEL_BLOCK
    ov = (cmp_start[:, None] < sb_start[None, :] + SEL_BLOCK) & (cmp_end[:, None] >= sb_start[None, :])
    ov = ov & (jnp.arange(nc)[:, None] < n_cmp) & (jnp.arange(LANES)[None, :] < s // SEL_BLOCK)
    return ov.astype(BF16)


def _pair_diag(w):
    z = jnp.zeros_like(w)
    return jnp.concatenate([jnp.concatenate([w, z], axis=-1), jnp.concatenate([z, w], axis=-1)], axis=-2)


def kernel(x, rel_bias, norm_mix_pre, norm_mix_post, norm_ffn_pre, norm_ffn_post, w_in, nsa_pe_k, nsa_pe_v, nsa_cmp_k_w1, nsa_cmp_k_w2, nsa_cmp_v_w1, nsa_cmp_v_w2, gla_gate_w2, gla_gate_b, gla_norm, w_branch_a, w_branch_b, w_branch_c, w_out, w_ffn_in, w_ffn_out):
    b, s, d = x.shape
    depth = w_in.shape[0]
    t = b * s
    dk = NSA_HEAD_DIM
    assert d == D_MODEL and s % ATT_TILE == 0 and WINDOW == 2 * ATT_TILE and MOBA_BLOCK == ATT_TILE
    assert s // SEL_BLOCK <= LANES and s // MOBA_BLOCK <= LANES and s // SEL_BLOCK >= SEL_TOPK
    nc = s // CMP_STRIDE
    assert nc % LANES == 0

    cols, d_in = _src_columns()
    assert w_in.shape[2] == d_in
    w_in_p = jnp.take(jnp.concatenate([w_in, jnp.zeros((depth, d, 1), w_in.dtype)], axis=2),
                      jnp.asarray(np.where(cols < 0, d_in, cols)), axis=2).astype(BF16)
    rel_a = rel_bias[:, :NSA_HEADS]
    rel_c = rel_bias[:, NSA_HEADS:]
    dnear_a = _near_bias(rel_a, ATT_TILE)
    dnear_c = _near_bias(rel_c, ATT_TILE).reshape(2, MOBA_HEADS // 2, 2 * ATT_TILE, ATT_TILE).transpose(1, 0, 2, 3)
    tcmp = _cmp_bias(rel_a, ATT_TILE, nc)
    overlap = _overlap(s, nc)
    avg = ((jnp.arange(LANES)[:, None] == jnp.arange(s)[None, :] // MOBA_BLOCK).astype(F32) / MOBA_BLOCK).astype(BF16)

    def cmp_w1(w1):
        w = _pair_diag(w1.reshape(depth, 2, CMP_STRIDE, dk, dk))
        return w.reshape(depth, 2, CMP_STRIDE * 2 * dk, 2 * dk)

    def cmp_pe(pe):
        p2 = jnp.concatenate([pe, pe], axis=-1).reshape(depth, 2, 1, CMP_STRIDE * 2 * dk)
        return jnp.broadcast_to(p2, (depth, 2, 8, CMP_STRIDE * 2 * dk))

    cw1 = jnp.stack([cmp_w1(nsa_cmp_k_w1), cmp_w1(nsa_cmp_v_w1)], axis=1).astype(BF16)
    cw2 = jnp.stack([_pair_diag(nsa_cmp_k_w2), _pair_diag(nsa_cmp_v_w2)], axis=1).astype(BF16)
    cpe = jnp.stack([cmp_pe(nsa_pe_k), cmp_pe(nsa_pe_v)], axis=1).astype(BF16)

    gw2 = jnp.concatenate([gla_gate_w2, jnp.zeros((depth, LANES - GLA_GATE_RANK, gla_gate_w2.shape[2]), F32)],
                          axis=1).astype(BF16)
    perm_a = np.array([(g * NSA_REP + r) * dk + e for r in range(NSA_REP) for g in range(NSA_KV_GROUPS)
                       for e in range(dk)])
    wa = w_branch_a[:, perm_a, :].astype(BF16)
    wb = w_branch_b.astype(BF16)
    wc = w_branch_c.astype(BF16)
    wo = w_out.astype(BF16)
    hid = w_ffn_out.shape[1]
    wg = w_ffn_in[:, :, :hid].astype(BF16)
    wu = w_ffn_in[:, :, hid:].astype(BF16)
    wf = w_ffn_out.astype(BF16)

    tm = min(1024, t)
    tn = NP_COLS // 5
    xf = x.reshape(t, d)
    for layer in range(depth):
        proj = _norm_matmul(xf, norm_mix_pre[layer], w_in_p[layer], tm, tn).reshape(b, s, NP_COLS)
        kc0, vc0 = SEC["a_kc"][0], SEC["a_vc"][0]
        xkv = jnp.stack([proj[:, :, kc0:kc0 + LANES], proj[:, :, vc0:vc0 + LANES]])
        xkv = xkv.reshape(2, b, nc, CMP_STRIDE * LANES)
        cmp = _compress(xkv, cpe[layer], cw1[layer], cw2[layer])
        o_a = _nsa_attention(proj, cmp[0], cmp[1], overlap, tcmp, dnear_a)
        o_b = _gla(proj, gw2[layer], gla_gate_b[layer].reshape(1, -1), gla_norm[layer].reshape(1, -1),
                   min(256, s))
        kmean = _moba_kmean(proj, avg)
        o_c = _moba_attention(proj, kmean, dnear_c)
        xf = _merge(o_a.reshape(t, -1), o_b.reshape(t, -1), o_c.reshape(t, -1), proj.reshape(t, NP_COLS), xf,
                    wa[layer], wb[layer], wc[layer], wo[layer], norm_mix_post[layer], min(512, t))
        xf = _ffn(xf, norm_ffn_pre[layer], wg[layer], wu[layer], wf[layer], norm_ffn_post[layer],
                  min(512, t), hid // 2)
    return xf.reshape(b, s, d)
```

```python
import math

import numpy as np
import jax
import jax.numpy as jnp
from jax import lax
from jax.experimental import pallas as pl
from jax.experimental.pallas import tpu as pltpu

F32 = jnp.float32
BF16 = jnp.bfloat16

NORM_EPS = 1e-6
NEG_INF = -1e30
TOP_BONUS = 1e9
NUM_BUCKETS = 32
T5_MAX_DISTANCE = 128
NSA_HEADS = 8
NSA_KV_GROUPS = 2
NSA_REP = NSA_HEADS // NSA_KV_GROUPS
NSA_HEAD_DIM = 64
CMP_LEN = 32
CMP_STRIDE = 16
SEL_BLOCK = 64
SEL_TOPK = 16
WINDOW = 512
N_NSA_BRANCH = 3
GLA_HEADS = 4
GLA_KEY_DIM = 64
GLA_VAL_DIM = 128
GLA_GATE_RANK = 16
GLA_GATE_TAU = 16.0
GLA_CHUNK = 64
MOBA_HEADS = 8
MOBA_HEAD_DIM = 64
MOBA_BLOCK = 256
MOBA_TOPK = 3
N_BRANCHES = 3
LOG2E = math.log2(math.e)

LANES = 128
ATT_TILE = 256
NSA_FAR_TILE = 512
MOBA_FAR_TILE = 1024
VMEM_LIMIT = 56 * 1024 * 1024

D_MODEL = 1024
SEC = {}
_off = 0
for _name, _w in (("mg", 3 * D_MODEL), ("a_q", 512), ("c_q", 512), ("c_k", 512), ("c_v", 512),
                  ("b_v", 512), ("b_r", 512), ("b_q", 256), ("b_k", 256),
                  ("a_kc", 128), ("a_vc", 128), ("a_ks", 128), ("a_vs", 128), ("a_kw", 128), ("a_vw", 128),
                  ("a_g", 128), ("b_lr", 128)):
    SEC[_name] = (_off, _w)
    _off += _w
NP_COLS = _off

_SRC_NAMES = ("a_q", "a_kc", "a_vc", "a_ks", "a_vs", "a_kw", "a_vw", "a_g", "b_q", "b_k", "b_v", "b_r", "b_lr",
              "c_q", "c_k", "c_v", "mg")
_SRC_SIZES = (512, 128, 128, 128, 128, 128, 128, 24, 256, 256, 512, 512, 16, 512, 512, 512, 3 * D_MODEL)
_SRC_OFF = dict(zip(_SRC_NAMES, np.cumsum((0,) + _SRC_SIZES[:-1]).tolist()))
_SRC_W = dict(zip(_SRC_NAMES, _SRC_SIZES))
D_IN = int(sum(_SRC_SIZES))


def _layout_w_in(w_in):
    depth, d, _ = w_in.shape

    def src(name, lo=0, hi=None):
        hi = _SRC_W[name] if hi is None else hi
        return w_in[:, :, _SRC_OFF[name] + lo:_SRC_OFF[name] + hi]

    parts = []
    for name, (_, width) in SEC.items():
        if name == "a_q":
            for r in range(NSA_REP):
                for g in range(NSA_KV_GROUPS):
                    h = g * NSA_REP + r
                    parts.append(src(name, h * NSA_HEAD_DIM, (h + 1) * NSA_HEAD_DIM) * (NSA_HEAD_DIM ** -0.5 * LOG2E))
        elif name == "c_q":
            parts.append(src(name) * (MOBA_HEAD_DIM ** -0.5 * LOG2E))
        else:
            parts.append(src(name))
            if _SRC_W[name] < width:
                parts.append(jnp.zeros((depth, d, width - _SRC_W[name]), w_in.dtype))
    return jnp.concatenate(parts, axis=2).astype(BF16)


def _cparams(sem):
    return pltpu.CompilerParams(dimension_semantics=sem, vmem_limit_bytes=VMEM_LIMIT)


def _const_spec(shape, index_map):
    return pl.BlockSpec(shape, index_map, pipeline_mode=pl.Buffered(1))


def _rms(y, w):
    return y * lax.rsqrt(jnp.mean(y * y, axis=-1, keepdims=True) + NORM_EPS) * w


def _norm_matmul_kernel(x_ref, nw_ref, w_ref, o_ref, h_ref):
    @pl.when(pl.program_id(1) == 0)
    def _():
        h_ref[...] = _rms(x_ref[...], nw_ref[...]).astype(BF16)

    o_ref[...] = jnp.dot(h_ref[...], w_ref[...], preferred_element_type=F32).astype(o_ref.dtype)


def _norm_matmul(x, nw, w, tm, tn):
    t, d = x.shape
    n = w.shape[1]
    return pl.pallas_call(
        _norm_matmul_kernel,
        grid=(t // tm, n // tn),
        in_specs=[pl.BlockSpec((tm, d), lambda i, j: (i, 0)),
                  pl.BlockSpec((1, d), lambda i, j: (0, 0)),
                  pl.BlockSpec((d, tn), lambda i, j: (0, j))],
        out_specs=pl.BlockSpec((tm, tn), lambda i, j: (i, j)),
        out_shape=jax.ShapeDtypeStruct((t, n), BF16),
        scratch_shapes=[pltpu.VMEM((tm, d), BF16)],
        compiler_params=_cparams(("parallel", "arbitrary")),
        name="norm_proj",
    )(x, nw.reshape(1, d), w)


def _compress_kernel(x_ref, pe_ref, w1_ref, w2_ref, o_ref):
    x = x_ref[...]
    nc = x.shape[0]
    w1t = w1_ref[0]
    w1b = w1_ref[1]
    a = jnp.dot(x, w1t, preferred_element_type=F32)
    b = jnp.dot(x, w1b, preferred_element_type=F32)
    pe = pe_ref[...]
    pe_term = (jnp.dot(pe[0], w1t, preferred_element_type=F32)
               + jnp.dot(pe[1], w1b, preferred_element_type=F32))[0:1]
    pre = a + pltpu.roll(b, nc - 1, 0) + pe_term
    hid = jax.nn.gelu(pre)
    o_ref[...] = jnp.dot(hid.astype(BF16), w2_ref[...], preferred_element_type=F32).astype(o_ref.dtype)


def _compress(xkv, pe, w1, w2):
    _, b, nc, kw = xkv.shape
    return pl.pallas_call(
        _compress_kernel,
        grid=(2, b),
        in_specs=[pl.BlockSpec((None, None, nc, kw), lambda s, i: (s, i, 0, 0)),
                  pl.BlockSpec((None, 2, 8, kw), lambda s, i: (s, 0, 0, 0)),
                  pl.BlockSpec((None, 2, kw, LANES), lambda s, i: (s, 0, 0, 0)),
                  pl.BlockSpec((None, LANES, LANES), lambda s, i: (s, 0, 0))],
        out_specs=pl.BlockSpec((None, None, nc, LANES), lambda s, i: (s, i, 0, 0)),
        out_shape=jax.ShapeDtypeStruct((2, b, nc, LANES), BF16),
        compiler_params=_cparams(("parallel", "parallel")),
        name="nsa_compress",
    )(xkv, pe, w1, w2)


def _flash_init(m_ref, acc_ref):
    m_ref[...] = jnp.full(m_ref.shape, NEG_INF, F32)
    acc_ref[...] = jnp.zeros(acc_ref.shape, F32)


def _with_ones(v):
    return jnp.concatenate([v, jnp.ones(v.shape, v.dtype)], axis=1)


def _flash_step(s, v1, m_ref, acc_ref):
    m_prev = m_ref[...]
    m_new = jnp.maximum(m_prev, jnp.max(s, axis=-1, keepdims=True))
    alpha = jnp.exp2(m_prev - m_new)
    p = jnp.exp2(s - jnp.tile(m_new, (1, s.shape[1] // LANES))).astype(BF16)
    acc_ref[...] = jnp.tile(alpha, (1, 2)) * acc_ref[...] + jnp.dot(p, v1, preferred_element_type=F32)
    m_ref[...] = m_new


def _flash_out(acc_ref):
    return acc_ref[:, 0:LANES] / acc_ref[:, LANES:2 * LANES]


def _qk(lhs, rhs):
    return lax.dot_general(lhs, rhs, (((1,), (1,)), ((), ())), preferred_element_type=F32)


def _block_onehot(width, block, first_block):
    krow = lax.broadcasted_iota(jnp.int32, (width, LANES), 0) // block
    klane = lax.broadcasted_iota(jnp.int32, (width, LANES), 1)
    return (klane == krow + first_block).astype(BF16)


def _topk_mask(vals, k):
    lane = lax.broadcasted_iota(jnp.int32, vals.shape, 1).astype(F32)
    sel = jnp.zeros(vals.shape, jnp.bool_)
    for _ in range(k):
        m = jnp.max(vals, axis=-1, keepdims=True)
        idx = jnp.min(jnp.where(vals == m, lane, float(LANES)), axis=-1, keepdims=True)
        hit = lane == idx
        sel = jnp.logical_or(sel, hit)
        vals = jnp.where(hit, -jnp.inf, vals)
    return sel


def _split3_dot(p, w):
    hi = p.astype(BF16)
    r1 = p - hi.astype(F32)
    mid = r1.astype(BF16)
    lo = (r1 - mid.astype(F32)).astype(BF16)
    return (jnp.dot(hi, w, preferred_element_type=F32) + jnp.dot(mid, w, preferred_element_type=F32)
            + jnp.dot(lo, w, preferred_element_type=F32))


def _nsa_kernel(q_ref, ks_ref, vs_ref, kw_ref, vw_ref, g_ref, kc_ref, vc_ref, ov_ref, tc_ref, dn_ref, wt_ref,
                o_ref, lhs_ref, nm_ref, m_ref, acc_ref, out_ref):
    tq = ATT_TILE
    nh = NSA_HEADS
    c = pl.program_id(1)
    t0 = c * tq
    ncp = kc_ref.shape[0]

    lane = lax.broadcasted_iota(jnp.int32, (tq, LANES), 1)
    rowi = lax.broadcasted_iota(jnp.int32, (tq, LANES), 0)
    half = lane // NSA_HEAD_DIM

    q = q_ref[...]
    for g in range(NSA_KV_GROUPS):
        for r in range(NSA_REP):
            h = g * NSA_REP + r
            qb = q[:, r * LANES:(r + 1) * LANES]
            lhs_ref[h * tq:(h + 1) * tq, 0:LANES] = jnp.where(half == g, qb, jnp.zeros_like(qb))

    gates = jax.nn.sigmoid(g_ref[...].astype(F32))

    def gate_col(h, br):
        col = h * N_NSA_BRANCH + br
        return gates[:, col:col + 1]

    kc = kc_ref[...]
    vc = vc_ref[...]
    ci = lax.broadcasted_iota(jnp.int32, (tq, ncp), 1)
    ri = lax.broadcasted_iota(jnp.int32, (tq, ncp), 0)
    valid = (ri + (t0 - (CMP_LEN - 1))) >= ci * CMP_STRIDE
    shift = lax.rem(c * (tq // CMP_STRIDE) + (ncp - tq // CMP_STRIDE), ncp)
    cur = (rowi + t0) // SEL_BLOCK
    forced = (lane == 0) | (lane == cur) | (lane == cur - 1)
    far_blocks = (c - 1) * (tq // SEL_BLOCK)
    for g in range(NSA_KV_GROUPS):
        psum = jnp.zeros((tq, ncp), F32)
        for r in range(NSA_REP):
            h = g * NSA_REP + r
            s = _qk(lhs_ref[h * tq:(h + 1) * tq, 0:LANES], kc)
            s = s + pltpu.roll(tc_ref[h], shift, 1)
            s = jnp.where(valid, s, NEG_INF)
            m = jnp.max(s, axis=-1, keepdims=True)
            e = jnp.where(valid, jnp.exp2(s - m), 0.0)
            p = e / jnp.maximum(jnp.sum(e, axis=-1, keepdims=True), 1e-20)
            psum = psum + p
            o_cmp = jnp.dot(p.astype(BF16), vc, preferred_element_type=F32)
            out_ref[h * tq:(h + 1) * tq, :] = gate_col(h, 0) * o_cmp
        imp = _split3_dot(psum, ov_ref[...])
        imp = jnp.where(forced, TOP_BONUS, jnp.where(lane <= cur, imp, NEG_INF))
        sel = _topk_mask(imp, SEL_TOPK)
        nm_ref[g] = jnp.where(sel, 0.0, NEG_INF).astype(BF16)
        far_mask = jnp.where(sel & (lane < far_blocks), 0.0, NEG_INF).astype(BF16)
        for r in range(NSA_REP):
            h = g * NSA_REP + r
            lhs_ref[h * tq:(h + 1) * tq, LANES:2 * LANES] = far_mask

    def sel_scores(start, width):
        rhs = jnp.concatenate([ks_ref[pl.ds(start, width), :],
                               _block_onehot(width, SEL_BLOCK, start // SEL_BLOCK)], axis=1)
        return _qk(lhs_ref[...], rhs), _with_ones(vs_ref[pl.ds(start, width), :])

    def win_scores(start, width):
        return (_qk(lhs_ref[:, 0:LANES], kw_ref[pl.ds(start, width), :]),
                _with_ones(vw_ref[pl.ds(start, width), :]))

    def near_step(scores):
        @pl.when(c >= 1)
        def _():
            s, v1 = scores(pl.multiple_of((c - 1) * tq, tq), 2 * tq)
            _flash_step(s + dn_ref[...], v1, m_ref, acc_ref)

        @pl.when(c == 0)
        def _():
            s, v1 = scores(0, tq)
            _flash_step(s + dn_ref[:, tq:2 * tq], v1, m_ref, acc_ref)

    def finish(br):
        o = _flash_out(acc_ref)
        for h in range(nh):
            sl = slice(h * tq, (h + 1) * tq)
            out_ref[sl, :] = out_ref[sl, :] + gate_col(h, br) * o[sl]

    _flash_init(m_ref, acc_ref)

    def far_body(j, carry):
        s, v1 = sel_scores(pl.multiple_of(j * NSA_FAR_TILE, NSA_FAR_TILE), NSA_FAR_TILE)
        _flash_step(s, v1, m_ref, acc_ref)
        return carry

    lax.fori_loop(0, c // (NSA_FAR_TILE // tq), far_body, 0)
    for h in range(nh):
        lhs_ref[h * tq:(h + 1) * tq, LANES:2 * LANES] = nm_ref[h // NSA_REP]
    near_step(sel_scores)
    finish(1)

    _flash_init(m_ref, acc_ref)

    @pl.when(c >= 2)
    def _():
        s, v1 = win_scores(pl.multiple_of((c - 2) * tq, tq), tq)
        _flash_step(s + jnp.tile(wt_ref[...], (nh, 1)), v1, m_ref, acc_ref)

    near_step(win_scores)
    finish(2)

    for r in range(NSA_REP):
        o0 = out_ref[r * tq:(r + 1) * tq, :]
        o1 = out_ref[(NSA_REP + r) * tq:(NSA_REP + r + 1) * tq, :]
        o_ref[:, r * LANES:(r + 1) * LANES] = jnp.where(half == 0, o0, o1).astype(o_ref.dtype)


def _nsa_attention(proj, kcmp, vcmp, overlap, tcmp, dnear, wtab):
    b, s, _ = proj.shape
    tq = ATT_TILE
    ncp = kcmp.shape[1]
    rows = NSA_HEADS * tq

    def col(name, width):
        return SEC[name][0] // width

    full = lambda name: _const_spec((None, s, LANES), lambda i, c, n=name: (i, 0, col(n, LANES)))
    return pl.pallas_call(
        _nsa_kernel,
        grid=(b, s // tq),
        in_specs=[pl.BlockSpec((None, tq, 512), lambda i, c: (i, c, col("a_q", 512))),
                  full("a_ks"), full("a_vs"), full("a_kw"), full("a_vw"),
                  pl.BlockSpec((None, tq, LANES), lambda i, c: (i, c, col("a_g", LANES))),
                  _const_spec((None, ncp, LANES), lambda i, c: (i, 0, 0)),
                  _const_spec((None, ncp, LANES), lambda i, c: (i, 0, 0)),
                  _const_spec((ncp, LANES), lambda i, c: (0, 0)),
                  _const_spec((NSA_HEADS, tq, ncp), lambda i, c: (0, 0, 0)),
                  _const_spec((rows, 2 * tq), lambda i, c: (0, 0)),
                  _const_spec((tq, tq), lambda i, c: (0, 0))],
        out_specs=pl.BlockSpec((None, tq, 512), lambda i, c: (i, c, 0)),
        out_shape=jax.ShapeDtypeStruct((b, s, 512), BF16),
        scratch_shapes=[pltpu.VMEM((rows, 2 * LANES), BF16),
                        pltpu.VMEM((NSA_KV_GROUPS, tq, LANES), BF16),
                        pltpu.VMEM((rows, LANES), F32), pltpu.VMEM((rows, 2 * LANES), F32),
                        pltpu.VMEM((rows, LANES), F32)],
        compiler_params=_cparams(("parallel", "arbitrary")),
        name="nsa_attention",
    )(proj, proj, proj, proj, proj, proj, kcmp, vcmp, overlap, tcmp, dnear, wtab)


def _gla_kernel(q_ref, k_ref, v_ref, r_ref, lr_ref, w2_ref, gb_ref, gn_ref, o_ref, st_ref):
    ch = GLA_CHUNK
    lb = q_ref.shape[0]
    hp = lax.Precision.HIGHEST

    @pl.when(pl.program_id(1) == 0)
    def _():
        st_ref[...] = jnp.zeros(st_ref.shape, F32)

    x = jnp.dot(lr_ref[...], w2_ref[...], preferred_element_type=F32) + gb_ref[...]
    log_a = (jnp.minimum(x, 0.0) - jnp.log1p(jnp.exp(-jnp.abs(x)))) / GLA_GATE_TAU

    ti = lax.broadcasted_iota(jnp.int32, (ch, ch), 0)
    tj = lax.broadcasted_iota(jnp.int32, (ch, ch), 1)
    tril = tj <= ti
    tri = tril.astype(BF16)
    half = lax.broadcasted_iota(jnp.int32, (ch, LANES), 1) // GLA_KEY_DIM
    eye = (lax.broadcasted_iota(jnp.int32, (LANES, LANES), 0)
           == lax.broadcasted_iota(jnp.int32, (LANES, LANES), 1))
    gn = gn_ref[...]

    for cc in range(lb // ch):
        sl = slice(cc * ch, (cc + 1) * ch)
        bcum = _split3_dot_left(tri, log_a[sl])
        for p in range(GLA_HEADS // 2):
            cols = slice(p * LANES, (p + 1) * LANES)
            bc = bcum[:, cols]
            blast = bc[ch - 1:ch, :]
            qf = q_ref[sl, cols].astype(F32) * (GLA_KEY_DIM ** -0.5)
            kf = k_ref[sl, cols].astype(F32)
            qe = qf * jnp.exp(bc)
            kinv = kf * jnp.exp(-bc)
            klast = kf * jnp.exp(blast - bc)
            decay = jnp.where(eye, jnp.broadcast_to(jnp.exp(blast), (LANES, LANES)), 0.0)
            for a in range(2):
                h = 2 * p + a
                hc = slice(h * LANES, (h + 1) * LANES)
                qa = jnp.where(half == a, qe, 0.0)
                attn = lax.dot_general(qa, kinv, (((1,), (1,)), ((), ())), precision=hp,
                                       preferred_element_type=F32)
                attn = jnp.where(tril, attn, 0.0)
                v = v_ref[sl, hc].astype(F32)
                st = st_ref[h]
                o = (jnp.dot(qa, st, precision=hp, preferred_element_type=F32)
                     + jnp.dot(attn, v, precision=hp, preferred_element_type=F32))
                st_ref[h] = (jnp.dot(decay, st, precision=hp, preferred_element_type=F32)
                             + lax.dot_general(klast, v, (((0,), (0,)), ((), ())), precision=hp,
                                               preferred_element_type=F32))
                rg = r_ref[sl, hc].astype(F32)
                o_ref[sl, hc] = (_rms(o, gn) * (rg * jax.nn.sigmoid(rg))).astype(o_ref.dtype)


def _split3_dot_left(w, x):
    hi = x.astype(BF16)
    r1 = x - hi.astype(F32)
    mid = r1.astype(BF16)
    lo = (r1 - mid.astype(F32)).astype(BF16)
    return (jnp.dot(w, hi, preferred_element_type=F32) + jnp.dot(w, mid, preferred_element_type=F32)
            + jnp.dot(w, lo, preferred_element_type=F32))


def _gla(proj, w2, gb, gn, lb):
    b, s, _ = proj.shape

    def spec(name, width):
        return pl.BlockSpec((None, lb, width), lambda i, c, n=name, w=width: (i, c, SEC[n][0] // w))

    return pl.pallas_call(
        _gla_kernel,
        grid=(b, s // lb),
        in_specs=[spec("b_q", 256), spec("b_k", 256), spec("b_v", 512), spec("b_r", 512), spec("b_lr", LANES),
                  pl.BlockSpec((LANES, 256), lambda i, c: (0, 0)),
                  pl.BlockSpec((1, 256), lambda i, c: (0, 0)),
                  pl.BlockSpec((1, LANES), lambda i, c: (0, 0))],
        out_specs=pl.BlockSpec((None, lb, 512), lambda i, c: (i, c, 0)),
        out_shape=jax.ShapeDtypeStruct((b, s, 512), BF16),
        scratch_shapes=[pltpu.VMEM((GLA_HEADS, LANES, LANES), F32)],
        compiler_params=_cparams(("parallel", "arbitrary")),
        name="gla",
    )(proj, proj, proj, proj, proj, w2, gb, gn)


def _kmean_kernel(a_ref, k_ref, o_ref):
    o_ref[...] = jnp.dot(a_ref[...], k_ref[...], preferred_element_type=F32).astype(o_ref.dtype)


def _moba_kmean(proj, avg):
    b, s, _ = proj.shape
    return pl.pallas_call(
        _kmean_kernel,
        grid=(b,),
        in_specs=[pl.BlockSpec((LANES, s), lambda i: (0, 0)),
                  pl.BlockSpec((None, s, 512), lambda i: (i, 0, SEC["c_k"][0] // 512))],
        out_specs=pl.BlockSpec((None, LANES, 512), lambda i: (i, 0, 0)),
        out_shape=jax.ShapeDtypeStruct((b, LANES, 512), BF16),
        compiler_params=_cparams(("parallel",)),
        name="moba_kmean",
    )(avg, proj)


def _moba_kernel(q_ref, k_ref, v_ref, km_ref, dn_ref, o_ref, lhs_ref, m_ref, acc_ref):
    tq = ATT_TILE
    rows = 2 * tq
    c = pl.program_id(2)

    lane = lax.broadcasted_iota(jnp.int32, (tq, LANES), 1)
    half = lane // MOBA_HEAD_DIM
    q = q_ref[...]
    for a in range(2):
        lhs_ref[a * tq:(a + 1) * tq, 0:LANES] = jnp.where(half == a, q, jnp.zeros_like(q))

    lane2 = lax.broadcasted_iota(jnp.int32, (rows, LANES), 1)
    score = _qk(lhs_ref[:, 0:LANES], km_ref[...])
    score = jnp.where(lane2 < c, score, NEG_INF)
    past = _topk_mask(score, MOBA_TOPK) & (lane2 < c)
    lhs_ref[:, LANES:2 * LANES] = jnp.where(past & (lane2 < c - 1), 0.0, NEG_INF).astype(BF16)

    def scores(start, width):
        rhs = jnp.concatenate([k_ref[pl.ds(start, width), :],
                               _block_onehot(width, MOBA_BLOCK, start // MOBA_BLOCK)], axis=1)
        return _qk(lhs_ref[...], rhs), _with_ones(v_ref[pl.ds(start, width), :])

    _flash_init(m_ref, acc_ref)

    def far_body(j, carry):
        s, v1 = scores(pl.multiple_of(j * MOBA_FAR_TILE, MOBA_FAR_TILE), MOBA_FAR_TILE)
        _flash_step(s, v1, m_ref, acc_ref)
        return carry

    per_step = MOBA_FAR_TILE // MOBA_BLOCK
    lax.fori_loop(0, (c - 1 + per_step - 1) // per_step, far_body, 0)

    lhs_ref[:, LANES:2 * LANES] = jnp.where(past | (lane2 == c), 0.0, NEG_INF).astype(BF16)

    @pl.when(c >= 1)
    def _():
        s, v1 = scores(pl.multiple_of((c - 1) * tq, tq), 2 * tq)
        _flash_step(s + dn_ref[...], v1, m_ref, acc_ref)

    @pl.when(c == 0)
    def _():
        s, v1 = scores(0, tq)
        _flash_step(s + dn_ref[:, tq:2 * tq], v1, m_ref, acc_ref)

    o = _flash_out(acc_ref)
    o_ref[...] = jnp.where(half == 0, o[0:tq], o[tq:2 * tq]).astype(o_ref.dtype)


def _moba_attention(proj, kmean, dnear):
    b, s, _ = proj.shape
    tq = ATT_TILE
    rows = 2 * tq
    npair = MOBA_HEADS // 2

    def col(name):
        return SEC[name][0] // LANES

    return pl.pallas_call(
        _moba_kernel,
        grid=(b, npair, s // tq),
        in_specs=[pl.BlockSpec((None, tq, LANES), lambda i, p, c: (i, c, col("c_q") + p)),
                  _const_spec((None, s, LANES), lambda i, p, c: (i, 0, col("c_k") + p)),
                  _const_spec((None, s, LANES), lambda i, p, c: (i, 0, col("c_v") + p)),
                  _const_spec((None, LANES, LANES), lambda i, p, c: (i, 0, p)),
                  _const_spec((None, rows, 2 * tq), lambda i, p, c: (p, 0, 0))],
        out_specs=pl.BlockSpec((None, tq, LANES), lambda i, p, c: (i, c, p)),
        out_shape=jax.ShapeDtypeStruct((b, s, 512), BF16),
        scratch_shapes=[pltpu.VMEM((rows, 2 * LANES), BF16),
                        pltpu.VMEM((rows, LANES), F32), pltpu.VMEM((rows, 2 * LANES), F32)],
        compiler_params=_cparams(("parallel", "parallel", "arbitrary")),
        name="moba_attention",
    )(proj, proj, proj, kmean, dnear)


def _merge_kernel(oa_ref, ob_ref, oc_ref, g0_ref, g1_ref, g2_ref, x_ref, wa_ref, wb_ref, wc_ref, wo_ref, nw_ref,
                  o_ref):
    def branch(o, w, g):
        return jax.nn.sigmoid(g[...].astype(F32)) * jnp.dot(o[...], w[...], preferred_element_type=F32)

    merged = branch(oa_ref, wa_ref, g0_ref) + branch(ob_ref, wb_ref, g1_ref) + branch(oc_ref, wc_ref, g2_ref)
    y = jnp.dot(merged.astype(BF16), wo_ref[...], preferred_element_type=F32)
    o_ref[...] = x_ref[...] + _rms(y, nw_ref[...])


def _merge(oa, ob, oc, proj, x, wa, wb, wc, wo, nw, tm):
    t, d = x.shape
    w = oa.shape[1]
    row = lambda width, j=0: pl.BlockSpec((tm, width), lambda i, j=j: (i, j))
    const = lambda shape: pl.BlockSpec(shape, lambda i: (0, 0))
    return pl.pallas_call(
        _merge_kernel,
        grid=(t // tm,),
        in_specs=[row(w), row(w), row(w), row(d, 0), row(d, 1), row(d, 2), row(d),
                  const((w, d)), const((w, d)), const((w, d)), const((d, d)), const((1, d))],
        out_specs=row(d),
        out_shape=jax.ShapeDtypeStruct((t, d), F32),
        compiler_params=_cparams(("parallel",)),
        name="merge_out",
    )(oa, ob, oc, proj, proj, proj, x, wa, wb, wc, wo, nw.reshape(1, d))


def _ffn_kernel(x_ref, npre_ref, wg_ref, wu_ref, wo_ref, npost_ref, o_ref, h_ref, acc_ref):
    j = pl.program_id(1)

    @pl.when(j == 0)
    def _():
        h_ref[...] = _rms(x_ref[...], npre_ref[...]).astype(BF16)
        acc_ref[...] = jnp.zeros(acc_ref.shape, F32)

    h = h_ref[...]
    gate = jnp.dot(h, wg_ref[...], preferred_element_type=F32)
    up = jnp.dot(h, wu_ref[...], preferred_element_type=F32)
    act = (gate * jax.nn.sigmoid(gate) * up).astype(BF16)
    acc_ref[...] += jnp.dot(act, wo_ref[...], preferred_element_type=F32)

    @pl.when(j == pl.num_programs(1) - 1)
    def _():
        o_ref[...] = x_ref[...] + _rms(acc_ref[...], npost_ref[...])


def _ffn(x, npre, wg, wu, wo, npost, tm, th):
    t, d = x.shape
    hid = wg.shape[1]
    return pl.pallas_call(
        _ffn_kernel,
        grid=(t // tm, hid // th),
        in_specs=[pl.BlockSpec((tm, d), lambda i, j: (i, 0)),
                  pl.BlockSpec((1, d), lambda i, j: (0, 0)),
                  pl.BlockSpec((d, th), lambda i, j: (0, j)),
                  pl.BlockSpec((d, th), lambda i, j: (0, j)),
                  pl.BlockSpec((th, d), lambda i, j: (j, 0)),
                  pl.BlockSpec((1, d), lambda i, j: (0, 0))],
        out_specs=pl.BlockSpec((tm, d), lambda i, j: (i, 0)),
        out_shape=jax.ShapeDtypeStruct((t, d), F32),
        scratch_shapes=[pltpu.VMEM((tm, d), BF16), pltpu.VMEM((tm, d), F32)],
        compiler_params=_cparams(("parallel", "arbitrary")),
        name="ffn",
    )(x, npre.reshape(1, d), wg, wu, wo, npost.reshape(1, d))


def _t5_bucket_table(dist):
    n = np.maximum(dist, 0)
    max_exact = NUM_BUCKETS // 2
    log_ratio = np.log(np.maximum(n, 1).astype(np.float32) / max_exact) / math.log(T5_MAX_DISTANCE / max_exact)
    large = max_exact + (log_ratio * (NUM_BUCKETS - max_exact)).astype(np.int32)
    return np.where(n < max_exact, n, np.minimum(large, NUM_BUCKETS - 1)).astype(np.int32)


def _bias_lookup(rel, dist):
    onehot = (jnp.asarray(_t5_bucket_table(dist))[..., None] == jnp.arange(NUM_BUCKETS)).astype(F32)
    return jnp.dot(onehot, (rel - rel[NUM_BUCKETS - 1]) * LOG2E, precision=lax.Precision.HIGHEST)


def _near_bias(rel, t):
    qi = np.arange(t)[:, None]
    ki = np.arange(t)[None, :]
    prev = _bias_lookup(rel, qi + t - ki)
    diag = jnp.where((qi >= ki)[..., None], _bias_lookup(rel, qi - ki), NEG_INF)
    return jnp.concatenate([prev, diag], axis=1).transpose(2, 0, 1)


def _cmp_bias(rel, t, ncp):
    w = t // CMP_STRIDE
    qi = np.arange(t)[:, None]
    j = np.arange(ncp)[None, :]
    d = qi - CMP_STRIDE * (j - w) - (CMP_LEN - 1)
    b = jnp.where(((d >= 0) & (j < 2 * w))[..., None], _bias_lookup(rel, d), 0.0)
    return b.transpose(2, 0, 1)


def _overlap(s, nc):
    n_cmp = (s - CMP_LEN) // CMP_STRIDE + 1
    cmp_end = np.arange(nc) * CMP_STRIDE + CMP_LEN - 1
    cmp_start = cmp_end - (CMP_LEN - 1)
    sb_start = np.arange(LANES) * SEL_BLOCK
    ov = (cmp_start[:, None] < sb_start[None, :] + SEL_BLOCK) & (cmp_end[:, None] >= sb_start[None, :])
    ov = ov & (np.arange(nc)[:, None] < n_cmp) & (np.arange(LANES)[None, :] < s // SEL_BLOCK)
    return jnp.asarray(ov, BF16)


def _pair_diag(w):
    z = jnp.zeros_like(w)
    return jnp.concatenate([jnp.concatenate([w, z], axis=-1), jnp.concatenate([z, w], axis=-1)], axis=-2)


def kernel(x, rel_bias, norm_mix_pre, norm_mix_post, norm_ffn_pre, norm_ffn_post, w_in, nsa_pe_k, nsa_pe_v, nsa_cmp_k_w1, nsa_cmp_k_w2, nsa_cmp_v_w1, nsa_cmp_v_w2, gla_gate_w2, gla_gate_b, gla_norm, w_branch_a, w_branch_b, w_branch_c, w_out, w_ffn_in, w_ffn_out):
    b, s, d = x.shape
    depth = w_in.shape[0]
    t = b * s
    dk = NSA_HEAD_DIM
    tq = ATT_TILE
    assert d == D_MODEL and w_in.shape[2] == D_IN
    assert WINDOW == 2 * tq and MOBA_BLOCK == tq and s % MOBA_FAR_TILE == 0 and s % NSA_FAR_TILE == 0
    assert SEL_TOPK <= s // SEL_BLOCK <= LANES and s // MOBA_BLOCK <= LANES
    nc = s // CMP_STRIDE
    assert nc % LANES == 0

    w_in_p = _layout_w_in(w_in)
    rel_a = rel_bias[:, :NSA_HEADS]
    rel_c = rel_bias[:, NSA_HEADS:]
    dnear_a = _near_bias(rel_a, tq).reshape(NSA_HEADS * tq, 2 * tq)
    dnear_c = _near_bias(rel_c, tq).reshape(MOBA_HEADS // 2, 2 * tq, 2 * tq)
    tcmp = _cmp_bias(rel_a, tq, nc)
    overlap = _overlap(s, nc)
    wtab = jnp.asarray(np.where(np.arange(tq)[None, :] > np.arange(tq)[:, None], 0.0, NEG_INF), F32)
    avg = jnp.asarray((np.arange(LANES)[:, None] == np.arange(s)[None, :] // MOBA_BLOCK) / MOBA_BLOCK, BF16)

    def cmp_w1(w1):
        w = _pair_diag(w1.reshape(depth, 2, CMP_STRIDE, dk, dk))
        return w.reshape(depth, 2, CMP_STRIDE * 2 * dk, 2 * dk)

    def cmp_pe(pe):
        p2 = jnp.concatenate([pe, pe], axis=-1).reshape(depth, 2, 1, CMP_STRIDE * 2 * dk)
        return jnp.broadcast_to(p2, (depth, 2, 8, CMP_STRIDE * 2 * dk))

    cw1 = jnp.stack([cmp_w1(nsa_cmp_k_w1), cmp_w1(nsa_cmp_v_w1)], axis=1).astype(BF16)
    cw2 = jnp.stack([_pair_diag(nsa_cmp_k_w2), _pair_diag(nsa_cmp_v_w2)], axis=1).astype(BF16)
    cpe = jnp.stack([cmp_pe(nsa_pe_k), cmp_pe(nsa_pe_v)], axis=1).astype(BF16)

    gw2 = jnp.concatenate([gla_gate_w2, jnp.zeros((depth, LANES - GLA_GATE_RANK, gla_gate_w2.shape[2]), F32)],
                          axis=1).astype(BF16)
    wa = (w_branch_a.reshape(depth, NSA_KV_GROUPS, NSA_REP, dk, d).transpose(0, 2, 1, 3, 4)
          .reshape(depth, NSA_HEADS * dk, d).astype(BF16))
    wb = w_branch_b.astype(BF16)
    wc = w_branch_c.astype(BF16)
    wo = w_out.astype(BF16)
    hid = w_ffn_out.shape[1]
    wg = w_ffn_in[:, :, :hid].astype(BF16)
    wu = w_ffn_in[:, :, hid:].astype(BF16)
    wf = w_ffn_out.astype(BF16)

    tm = min(1024, t)
    tn = NP_COLS // 5
    xf = x.reshape(t, d)
    for layer in range(depth):
        proj = _norm_matmul(xf, norm_mix_pre[layer], w_in_p[layer], tm, tn).reshape(b, s, NP_COLS)
        kc0, vc0 = SEC["a_kc"][0], SEC["a_vc"][0]
        xkv = jnp.stack([proj[:, :, kc0:kc0 + LANES], proj[:, :, vc0:vc0 + LANES]])
        xkv = xkv.reshape(2, b, nc, CMP_STRIDE * LANES)
        cmp = _compress(xkv, cpe[layer], cw1[layer], cw2[layer])
        o_a = _nsa_attention(proj, cmp[0], cmp[1], overlap, tcmp, dnear_a, wtab)
        o_b = _gla(proj, gw2[layer], gla_gate_b[layer].reshape(1, -1), gla_norm[layer].reshape(1, -1),
                   min(256, s))
        kmean = _moba_kmean(proj, avg)
        o_c = _moba_attention(proj, kmean, dnear_c)
        xf = _merge(o_a.reshape(t, -1), o_b.reshape(t, -1), o_c.reshape(t, -1), proj.reshape(t, NP_COLS), xf,
                    wa[layer], wb[layer], wc[layer], wo[layer], norm_mix_post[layer], min(512, t))
        xf = _ffn(xf, norm_ffn_pre[layer], wg[layer], wu[layer], wf[layer], norm_ffn_post[layer],
                  min(512, t), hid // 2)
    return xf.reshape(b, s, d)
```

```python
import math

import numpy as np
import jax
import jax.numpy as jnp
from jax import lax
from jax.experimental import pallas as pl
from jax.experimental.pallas import tpu as pltpu

F32 = jnp.float32
BF16 = jnp.bfloat16

NORM_EPS = 1e-6
NEG_INF = -1e30
TOP_BONUS = 1e9
NUM_BUCKETS = 32
T5_MAX_DISTANCE = 128
NSA_HEADS = 8
NSA_KV_GROUPS = 2
NSA_REP = NSA_HEADS // NSA_KV_GROUPS
NSA_HEAD_DIM = 64
CMP_LEN = 32
CMP_STRIDE = 16
SEL_BLOCK = 64
SEL_TOPK = 16
WINDOW = 512
N_NSA_BRANCH = 3
GLA_HEADS = 4
GLA_KEY_DIM = 64
GLA_VAL_DIM = 128
GLA_GATE_RANK = 16
GLA_GATE_TAU = 16.0
GLA_CHUNK = 64
MOBA_HEADS = 8
MOBA_HEAD_DIM = 64
MOBA_BLOCK = 256
MOBA_TOPK = 3
N_BRANCHES = 3
LOG2E = math.log2(math.e)

LANES = 128
ATT_TILE = 256
MOBA_FAR_TILE = 1024
VMEM_LIMIT = 56 * 1024 * 1024

D_MODEL = 1024
SEC = {}
_off = 0
for _name, _w in (("mg", 3 * D_MODEL), ("a_q", 512), ("c_q", 512), ("c_k", 512), ("c_v", 512),
                  ("b_v", 512), ("b_r", 512), ("b_q", 256), ("b_k", 256),
                  ("a_kc", 128), ("a_vc", 128), ("a_ks", 128), ("a_vs", 128), ("a_kw", 128), ("a_vw", 128),
                  ("a_g", 128), ("b_lr", 128)):
    SEC[_name] = (_off, _w)
    _off += _w
NP_COLS = _off

_SRC_NAMES = ("a_q", "a_kc", "a_vc", "a_ks", "a_vs", "a_kw", "a_vw", "a_g", "b_q", "b_k", "b_v", "b_r", "b_lr",
              "c_q", "c_k", "c_v", "mg")
_SRC_SIZES = (512, 128, 128, 128, 128, 128, 128, 24, 256, 256, 512, 512, 16, 512, 512, 512, 3 * D_MODEL)
_SRC_OFF = dict(zip(_SRC_NAMES, np.cumsum((0,) + _SRC_SIZES[:-1]).tolist()))
_SRC_W = dict(zip(_SRC_NAMES, _SRC_SIZES))
D_IN = int(sum(_SRC_SIZES))


def _layout_w_in(w_in):
    depth, d, _ = w_in.shape

    def src(name, lo=0, hi=None):
        hi = _SRC_W[name] if hi is None else hi
        return w_in[:, :, _SRC_OFF[name] + lo:_SRC_OFF[name] + hi]

    parts = []
    for name, (_, width) in SEC.items():
        if name == "a_q":
            for r in range(NSA_REP):
                for g in range(NSA_KV_GROUPS):
                    h = g * NSA_REP + r
                    parts.append(src(name, h * NSA_HEAD_DIM, (h + 1) * NSA_HEAD_DIM) * (NSA_HEAD_DIM ** -0.5 * LOG2E))
        elif name == "c_q":
            parts.append(src(name) * (MOBA_HEAD_DIM ** -0.5 * LOG2E))
        else:
            parts.append(src(name))
            if _SRC_W[name] < width:
                parts.append(jnp.zeros((depth, d, width - _SRC_W[name]), w_in.dtype))
    return jnp.concatenate(parts, axis=2).astype(BF16)


def _cparams(sem):
    return pltpu.CompilerParams(dimension_semantics=sem, vmem_limit_bytes=VMEM_LIMIT)


def _const_spec(shape, index_map):
    return pl.BlockSpec(shape, index_map, pipeline_mode=pl.Buffered(1))


def _rms(y, w):
    return y * lax.rsqrt(jnp.mean(y * y, axis=-1, keepdims=True) + NORM_EPS) * w


def _norm_matmul_kernel(x_ref, nw_ref, w_ref, o_ref, h_ref):
    @pl.when(pl.program_id(1) == 0)
    def _():
        h_ref[...] = _rms(x_ref[...], nw_ref[...]).astype(BF16)

    o_ref[...] = jnp.dot(h_ref[...], w_ref[...], preferred_element_type=F32).astype(o_ref.dtype)


def _norm_matmul(x, nw, w, tm, tn):
    t, d = x.shape
    n = w.shape[1]
    return pl.pallas_call(
        _norm_matmul_kernel,
        grid=(t // tm, n // tn),
        in_specs=[pl.BlockSpec((tm, d), lambda i, j: (i, 0)),
                  pl.BlockSpec((1, d), lambda i, j: (0, 0)),
                  pl.BlockSpec((d, tn), lambda i, j: (0, j))],
        out_specs=pl.BlockSpec((tm, tn), lambda i, j: (i, j)),
        out_shape=jax.ShapeDtypeStruct((t, n), BF16),
        scratch_shapes=[pltpu.VMEM((tm, d), BF16)],
        compiler_params=_cparams(("parallel", "arbitrary")),
        name="norm_proj",
    )(x, nw.reshape(1, d), w)


def _compress_kernel(x_ref, pe_ref, w1_ref, w2_ref, o_ref):
    x = x_ref[...]
    nc = x.shape[0]
    w1t = w1_ref[0]
    w1b = w1_ref[1]
    a = jnp.dot(x, w1t, preferred_element_type=F32)
    b = jnp.dot(x, w1b, preferred_element_type=F32)
    pe = pe_ref[...]
    pe_term = (jnp.dot(pe[0], w1t, preferred_element_type=F32)
               + jnp.dot(pe[1], w1b, preferred_element_type=F32))[0:1]
    pre = a + pltpu.roll(b, nc - 1, 0) + pe_term
    hid = jax.nn.gelu(pre)
    o_ref[...] = jnp.dot(hid.astype(BF16), w2_ref[...], preferred_element_type=F32).astype(o_ref.dtype)


def _compress(xkv, pe, w1, w2):
    _, b, nc, kw = xkv.shape
    return pl.pallas_call(
        _compress_kernel,
        grid=(2, b),
        in_specs=[pl.BlockSpec((None, None, nc, kw), lambda s, i: (s, i, 0, 0)),
                  pl.BlockSpec((None, 2, 8, kw), lambda s, i: (s, 0, 0, 0)),
                  pl.BlockSpec((None, 2, kw, LANES), lambda s, i: (s, 0, 0, 0)),
                  pl.BlockSpec((None, LANES, LANES), lambda s, i: (s, 0, 0))],
        out_specs=pl.BlockSpec((None, None, nc, LANES), lambda s, i: (s, i, 0, 0)),
        out_shape=jax.ShapeDtypeStruct((2, b, nc, LANES), BF16),
        compiler_params=_cparams(("parallel", "parallel")),
        name="nsa_compress",
    )(xkv, pe, w1, w2)


def _with_ones(v):
    return jnp.concatenate([v, jnp.ones(v.shape, v.dtype)], axis=1)


def _flash_step(s, v1, m_ref, acc_ref):
    m_prev = m_ref[...]
    m_new = jnp.maximum(m_prev, jnp.max(s, axis=-1, keepdims=True))
    alpha = jnp.exp2(m_prev - m_new)
    p = jnp.exp2(s - jnp.tile(m_new, (1, s.shape[1] // LANES))).astype(BF16)
    acc_ref[...] = jnp.tile(alpha, (1, 2)) * acc_ref[...] + jnp.dot(p, v1, preferred_element_type=F32)
    m_ref[...] = m_new


def _flash_out(acc_ref):
    return acc_ref[:, 0:LANES] / acc_ref[:, LANES:2 * LANES]


def _softmax_values(s, v1):
    p = jnp.exp2(s - jnp.max(s, axis=-1, keepdims=True)).astype(BF16)
    r = jnp.dot(p, v1, preferred_element_type=F32)
    return r[:, 0:LANES] / r[:, LANES:2 * LANES]


def _qk(lhs, rhs):
    return lax.dot_general(lhs, rhs, (((1,), (1,)), ((), ())), preferred_element_type=F32)


def _lane_onehot(width, lane_idx):
    klane = lax.broadcasted_iota(jnp.int32, (width, LANES), 1)
    return (klane == lane_idx).astype(BF16)


def _far_loop(n_far, tk, block, lhs_ref, k_ref, v_ref, s_refs, p_refs, a_refs, m_ref, acc_ref):
    rows = lhs_ref.shape[0]
    j_max = k_ref.shape[0] // tk - 1
    per = tk // block
    krow = lax.broadcasted_iota(jnp.int32, (tk, LANES), 0) // block
    klane = lax.broadcasted_iota(jnp.int32, (tk, LANES), 1)

    s_refs[1][...] = jnp.full((rows, tk), -jnp.inf, F32)
    p_refs[0][...] = jnp.zeros((rows, tk), BF16)
    a_refs[0][...] = jnp.ones((rows, LANES), F32)

    def stage(i, slot):
        jv = jnp.clip(i - 2, 0, j_max)
        v1 = _with_ones(v_ref[pl.ds(pl.multiple_of(jv * tk, tk), tk), :])
        acc_ref[...] = (jnp.tile(a_refs[slot][...], (1, 2)) * acc_ref[...]
                        + jnp.dot(p_refs[slot][...], v1, preferred_element_type=F32))
        s = s_refs[1 - slot][...]
        m_prev = m_ref[...]
        m_new = jnp.maximum(m_prev, jnp.max(s, axis=-1, keepdims=True))
        a_refs[1 - slot][...] = jnp.exp2(m_prev - m_new)
        p_refs[1 - slot][...] = jnp.exp2(s - jnp.tile(m_new, (1, tk // LANES))).astype(BF16)
        m_ref[...] = m_new
        jk = jnp.minimum(i, j_max)
        onehot = (klane == jnp.where(i < n_far, krow + i * per, LANES - 1)).astype(BF16)
        rhs = jnp.concatenate([k_ref[pl.ds(pl.multiple_of(jk * tk, tk), tk), :], onehot], axis=1)
        s_refs[slot][...] = _qk(lhs_ref[...], rhs)

    def body(ii, carry):
        stage(2 * ii, 0)
        stage(2 * ii + 1, 1)
        return carry

    lax.fori_loop(0, jnp.where(n_far > 0, (n_far + 3) // 2, 0), body, 0)


def _topk_mask(vals, k):
    lane = lax.broadcasted_iota(jnp.int32, vals.shape, 1).astype(F32)
    sel = jnp.zeros(vals.shape, jnp.bool_)
    for _ in range(k):
        m = jnp.max(vals, axis=-1, keepdims=True)
        idx = jnp.min(jnp.where(vals == m, lane, float(LANES)), axis=-1, keepdims=True)
        hit = lane == idx
        sel = jnp.logical_or(sel, hit)
        vals = jnp.where(hit, -jnp.inf, vals)
    return sel


def _nsa_kernel(q_ref, ks_ref, vs_ref, kw_ref, vw_ref, g_ref, kc_ref, vc_ref, ov_ref, tc_ref, dp_ref, dd_ref, wt_ref,
                o_ref, lhs_ref, nm_ref, m_ref, acc_ref, out_ref, s0_ref, s1_ref, p0_ref, p1_ref, a0_ref, a1_ref):
    tq = ATT_TILE
    nh = NSA_HEADS
    rows = nh * tq
    c = pl.program_id(1)
    t0 = c * tq
    ncp = kc_ref.shape[0]
    wcols = 2 * (tq // CMP_STRIDE)

    lane = lax.broadcasted_iota(jnp.int32, (tq, LANES), 1)
    rowi = lax.broadcasted_iota(jnp.int32, (tq, LANES), 0)
    half = lane // NSA_HEAD_DIM

    q = q_ref[...]
    for g in range(NSA_KV_GROUPS):
        for r in range(NSA_REP):
            h = g * NSA_REP + r
            qb = q[:, r * LANES:(r + 1) * LANES]
            lhs_ref[h * tq:(h + 1) * tq, 0:LANES] = jnp.where(half == g, qb, jnp.zeros_like(qb))
    qrows = lhs_ref[:, 0:LANES]

    gates = jax.nn.sigmoid(g_ref[...].astype(F32))

    def gate_col(h, br):
        col = h * N_NSA_BRANCH + br
        return gates[:, col:col + 1]

    prev_start = pl.multiple_of(jnp.maximum(c - 1, 0) * tq, tq)
    diag_start = pl.multiple_of(c * tq, tq)

    wpat = jnp.where(lax.broadcasted_iota(jnp.int32, (rows, LANES), 1) == 1, NEG_INF, 0.0).astype(BF16)
    back2_start = pl.multiple_of(jnp.maximum(c - 2, 0) * tq, tq)
    kwin = jnp.concatenate([kw_ref[pl.ds(back2_start, tq), :], kw_ref[pl.ds(prev_start, tq), :],
                            kw_ref[pl.ds(diag_start, tq), :]], axis=0)
    vwin = jnp.concatenate([vw_ref[pl.ds(back2_start, tq), :], vw_ref[pl.ds(prev_start, tq), :],
                            vw_ref[pl.ds(diag_start, tq), :]], axis=0)
    hot = jnp.concatenate([_lane_onehot(tq, jnp.where(c >= 2, 0, 1)), _lane_onehot(tq, jnp.where(c >= 1, 0, 1)),
                           _lane_onehot(tq, 0)], axis=0)
    s = _qk(jnp.concatenate([qrows, wpat], axis=1), jnp.concatenate([kwin, hot], axis=1))
    s = s + jnp.concatenate([jnp.tile(wt_ref[...], (nh, 1)), dp_ref[...], dd_ref[...]], axis=1)
    o_win = _softmax_values(s, _with_ones(vwin))
    for h in range(nh):
        sl = slice(h * tq, (h + 1) * tq)
        out_ref[sl, :] = gate_col(h, 2) * o_win[sl]

    ii = lax.broadcasted_iota(jnp.int32, (ncp, LANES), 0) - (c * (tq // CMP_STRIDE) - tq // CMP_STRIDE)
    ww = lax.broadcasted_iota(jnp.int32, (ncp, LANES), 1)
    place = (((ww < wcols) & (ii == ww)) | ((ww == wcols) & (ii >= wcols))).astype(BF16)
    s = _qk(jnp.concatenate([qrows, tc_ref[...]], axis=1), jnp.concatenate([kc_ref[...], place], axis=1))
    m = jnp.max(s, axis=-1, keepdims=True)
    p = jnp.exp2(s - m)
    p_hi = p.astype(BF16)
    p_lo = (p - p_hi.astype(F32)).astype(BF16)
    ov = ov_ref[...]
    r1 = jnp.dot(p_hi, jnp.concatenate([_with_ones(vc_ref[...]), ov], axis=1), preferred_element_type=F32)
    r2 = jnp.dot(p_lo, ov, preferred_element_type=F32)
    has_key = m > 0.5 * NEG_INF
    inv_l = jnp.where(has_key, 1.0 / r1[:, LANES:2 * LANES], 0.0)
    o_cmp = r1[:, 0:LANES] * inv_l
    imp_h = (r1[:, 2 * LANES:3 * LANES] + r2) * inv_l
    for h in range(nh):
        sl = slice(h * tq, (h + 1) * tq)
        out_ref[sl, :] = out_ref[sl, :] + gate_col(h, 0) * o_cmp[sl]

    cur = (rowi + t0) // SEL_BLOCK
    forced = (lane == 0) | (lane == cur) | (lane == cur - 1)
    far_blocks = (c - 1) * (tq // SEL_BLOCK)
    for g in range(NSA_KV_GROUPS):
        imp = imp_h[g * NSA_REP * tq:(g * NSA_REP + 1) * tq]
        for r in range(1, NSA_REP):
            imp = imp + imp_h[(g * NSA_REP + r) * tq:(g * NSA_REP + r + 1) * tq]
        imp = jnp.where(forced, TOP_BONUS, jnp.where(lane <= cur, imp, NEG_INF))
        sel = _topk_mask(imp, SEL_TOPK)
        nm_ref[g] = jnp.where(sel, 0.0, NEG_INF).astype(BF16)
        far_mask = jnp.where(sel & (lane < far_blocks), 0.0, NEG_INF).astype(BF16)
        for r in range(NSA_REP):
            h = g * NSA_REP + r
            lhs_ref[h * tq:(h + 1) * tq, LANES:2 * LANES] = far_mask

    m_ref[...] = jnp.full(m_ref.shape, NEG_INF, F32)
    acc_ref[...] = jnp.zeros(acc_ref.shape, F32)
    _far_loop(jnp.maximum(c - 1, 0), tq, SEL_BLOCK, lhs_ref, ks_ref, vs_ref,
              (s0_ref, s1_ref), (p0_ref, p1_ref), (a0_ref, a1_ref), m_ref, acc_ref)

    for h in range(nh):
        lhs_ref[h * tq:(h + 1) * tq, LANES:2 * LANES] = nm_ref[h // NSA_REP]
    per = tq // SEL_BLOCK
    krow = lax.broadcasted_iota(jnp.int32, (tq, LANES), 0) // SEL_BLOCK
    klane = lax.broadcasted_iota(jnp.int32, (tq, LANES), 1)
    hot = jnp.concatenate([(klane == jnp.where(c >= 1, krow + (c - 1) * per, LANES - 1)).astype(BF16),
                           (klane == krow + c * per).astype(BF16)], axis=0)
    knear = jnp.concatenate([ks_ref[pl.ds(prev_start, tq), :], ks_ref[pl.ds(diag_start, tq), :]], axis=0)
    vnear = jnp.concatenate([vs_ref[pl.ds(prev_start, tq), :], vs_ref[pl.ds(diag_start, tq), :]], axis=0)
    s = _qk(lhs_ref[...], jnp.concatenate([knear, hot], axis=1))
    s = s + jnp.concatenate([dp_ref[...], dd_ref[...]], axis=1)
    _flash_step(s, _with_ones(vnear), m_ref, acc_ref)
    o_sel = _flash_out(acc_ref)

    for r in range(NSA_REP):
        h0, h1 = r, NSA_REP + r
        o0 = out_ref[h0 * tq:(h0 + 1) * tq, :] + gate_col(h0, 1) * o_sel[h0 * tq:(h0 + 1) * tq]
        o1 = out_ref[h1 * tq:(h1 + 1) * tq, :] + gate_col(h1, 1) * o_sel[h1 * tq:(h1 + 1) * tq]
        o_ref[:, r * LANES:(r + 1) * LANES] = jnp.where(half == 0, o0, o1).astype(o_ref.dtype)


def _nsa_attention(proj, kcmp, vcmp, overlap, tcmp, dprev, ddiag, wtab):
    b, s, _ = proj.shape
    tq = ATT_TILE
    ncp = kcmp.shape[1]
    rows = NSA_HEADS * tq

    def col(name, width):
        return SEC[name][0] // width

    full = lambda name: _const_spec((None, s, LANES), lambda i, c, n=name: (i, 0, col(n, LANES)))
    return pl.pallas_call(
        _nsa_kernel,
        grid=(b, s // tq),
        in_specs=[pl.BlockSpec((None, tq, 512), lambda i, c: (i, c, col("a_q", 512))),
                  full("a_ks"), full("a_vs"), full("a_kw"), full("a_vw"),
                  pl.BlockSpec((None, tq, LANES), lambda i, c: (i, c, col("a_g", LANES))),
                  _const_spec((None, ncp, LANES), lambda i, c: (i, 0, 0)),
                  _const_spec((None, ncp, LANES), lambda i, c: (i, 0, 0)),
                  _const_spec((ncp, LANES), lambda i, c: (0, 0)),
                  _const_spec((rows, LANES), lambda i, c: (0, 0)),
                  _const_spec((rows, tq), lambda i, c: (0, 0)),
                  _const_spec((rows, tq), lambda i, c: (0, 0)),
                  _const_spec((tq, tq), lambda i, c: (0, 0))],
        out_specs=pl.BlockSpec((None, tq, 512), lambda i, c: (i, c, 0)),
        out_shape=jax.ShapeDtypeStruct((b, s, 512), BF16),
        scratch_shapes=[pltpu.VMEM((rows, 2 * LANES), BF16),
                        pltpu.VMEM((NSA_KV_GROUPS, tq, LANES), BF16),
                        pltpu.VMEM((rows, LANES), F32), pltpu.VMEM((rows, 2 * LANES), F32),
                        pltpu.VMEM((rows, LANES), F32),
                        pltpu.VMEM((rows, tq), F32), pltpu.VMEM((rows, tq), F32),
                        pltpu.VMEM((rows, tq), BF16), pltpu.VMEM((rows, tq), BF16),
                        pltpu.VMEM((rows, LANES), F32), pltpu.VMEM((rows, LANES), F32)],
        compiler_params=_cparams(("parallel", "arbitrary")),
        name="nsa_attention",
    )(proj, proj, proj, proj, proj, proj, kcmp, vcmp, overlap, tcmp, dprev, ddiag, wtab)


def _gla_kernel(q_ref, k_ref, v_ref, r_ref, lr_ref, w2_ref, gb_ref, gn_ref, o_ref, st_ref):
    ch = GLA_CHUNK
    lb = q_ref.shape[0]
    hp = lax.Precision.HIGHEST

    @pl.when(pl.program_id(1) == 0)
    def _():
        st_ref[...] = jnp.zeros(st_ref.shape, F32)

    x = jnp.dot(lr_ref[...], w2_ref[...], preferred_element_type=F32) + gb_ref[...]
    log_a = (jnp.minimum(x, 0.0) - jnp.log1p(jnp.exp(-jnp.abs(x)))) / GLA_GATE_TAU

    ti = lax.broadcasted_iota(jnp.int32, (ch, ch), 0)
    tj = lax.broadcasted_iota(jnp.int32, (ch, ch), 1)
    tril = tj <= ti
    tri = tril.astype(BF16)
    half = lax.broadcasted_iota(jnp.int32, (ch, LANES), 1) // GLA_KEY_DIM
    eye = (lax.broadcasted_iota(jnp.int32, (LANES, LANES), 0)
           == lax.broadcasted_iota(jnp.int32, (LANES, LANES), 1))
    gn = gn_ref[...]

    for cc in range(lb // ch):
        sl = slice(cc * ch, (cc + 1) * ch)
        bcum = _split3_dot_left(tri, log_a[sl])
        for p in range(GLA_HEADS // 2):
            cols = slice(p * LANES, (p + 1) * LANES)
            bc = bcum[:, cols]
            blast = bc[ch - 1:ch, :]
            qf = q_ref[sl, cols].astype(F32) * (GLA_KEY_DIM ** -0.5)
            kf = k_ref[sl, cols].astype(F32)
            qe = qf * jnp.exp(bc)
            kinv = kf * jnp.exp(-bc)
            klast = kf * jnp.exp(blast - bc)
            decay = jnp.where(eye, jnp.broadcast_to(jnp.exp(blast), (LANES, LANES)), 0.0)
            for a in range(2):
                h = 2 * p + a
                hc = slice(h * LANES, (h + 1) * LANES)
                qa = jnp.where(half == a, qe, 0.0)
                attn = lax.dot_general(qa, kinv, (((1,), (1,)), ((), ())), precision=hp,
                                       preferred_element_type=F32)
                attn = jnp.where(tril, attn, 0.0)
                v = v_ref[sl, hc].astype(F32)
                st = st_ref[h]
                o = (jnp.dot(qa, st, precision=hp, preferred_element_type=F32)
                     + jnp.dot(attn, v, precision=hp, preferred_element_type=F32))
                st_ref[h] = (jnp.dot(decay, st, precision=hp, preferred_element_type=F32)
                             + lax.dot_general(klast, v, (((0,), (0,)), ((), ())), precision=hp,
                                               preferred_element_type=F32))
                rg = r_ref[sl, hc].astype(F32)
                o_ref[sl, hc] = (_rms(o, gn) * (rg * jax.nn.sigmoid(rg))).astype(o_ref.dtype)


def _split3_dot_left(w, x):
    hi = x.astype(BF16)
    r1 = x - hi.astype(F32)
    mid = r1.astype(BF16)
    lo = (r1 - mid.astype(F32)).astype(BF16)
    return (jnp.dot(w, hi, preferred_element_type=F32) + jnp.dot(w, mid, preferred_element_type=F32)
            + jnp.dot(w, lo, preferred_element_type=F32))


def _gla(proj, w2, gb, gn, lb):
    b, s, _ = proj.shape

    def spec(name, width):
        return pl.BlockSpec((None, lb, width), lambda i, c, n=name, w=width: (i, c, SEC[n][0] // w))

    return pl.pallas_call(
        _gla_kernel,
        grid=(b, s // lb),
        in_specs=[spec("b_q", 256), spec("b_k", 256), spec("b_v", 512), spec("b_r", 512), spec("b_lr", LANES),
                  pl.BlockSpec((LANES, 256), lambda i, c: (0, 0)),
                  pl.BlockSpec((1, 256), lambda i, c: (0, 0)),
                  pl.BlockSpec((1, LANES), lambda i, c: (0, 0))],
        out_specs=pl.BlockSpec((None, lb, 512), lambda i, c: (i, c, 0)),
        out_shape=jax.ShapeDtypeStruct((b, s, 512), BF16),
        scratch_shapes=[pltpu.VMEM((GLA_HEADS, LANES, LANES), F32)],
        compiler_params=_cparams(("parallel", "arbitrary")),
        name="gla",
    )(proj, proj, proj, proj, proj, w2, gb, gn)


def _kmean_kernel(a_ref, k_ref, o_ref):
    o_ref[...] = jnp.dot(a_ref[...], k_ref[...], preferred_element_type=F32).astype(o_ref.dtype)


def _moba_kmean(proj, avg):
    b, s, _ = proj.shape
    return pl.pallas_call(
        _kmean_kernel,
        grid=(b,),
        in_specs=[pl.BlockSpec((LANES, s), lambda i: (0, 0)),
                  pl.BlockSpec((None, s, 512), lambda i: (i, 0, SEC["c_k"][0] // 512))],
        out_specs=pl.BlockSpec((None, LANES, 512), lambda i: (i, 0, 0)),
        out_shape=jax.ShapeDtypeStruct((b, LANES, 512), BF16),
        compiler_params=_cparams(("parallel",)),
        name="moba_kmean",
    )(avg, proj)


def _moba_kernel(q_ref, k_ref, v_ref, km_ref, dp_ref, dd_ref, o_ref, lhs_ref, m_ref, acc_ref,
                 s0_ref, s1_ref, p0_ref, p1_ref, a0_ref, a1_ref):
    tq = ATT_TILE
    rows = 2 * tq
    c = pl.program_id(2)

    lane = lax.broadcasted_iota(jnp.int32, (tq, LANES), 1)
    half = lane // MOBA_HEAD_DIM
    q = q_ref[...]
    for a in range(2):
        lhs_ref[a * tq:(a + 1) * tq, 0:LANES] = jnp.where(half == a, q, jnp.zeros_like(q))

    lane2 = lax.broadcasted_iota(jnp.int32, (rows, LANES), 1)
    score = _qk(lhs_ref[:, 0:LANES], km_ref[...])
    score = jnp.where(lane2 < c, score, NEG_INF)
    past = _topk_mask(score, MOBA_TOPK) & (lane2 < c)
    lhs_ref[:, LANES:2 * LANES] = jnp.where(past & (lane2 < c - 1), 0.0, NEG_INF).astype(BF16)

    m_ref[...] = jnp.full(m_ref.shape, NEG_INF, F32)
    acc_ref[...] = jnp.zeros(acc_ref.shape, F32)
    per_step = MOBA_FAR_TILE // MOBA_BLOCK
    _far_loop((jnp.maximum(c - 1, 0) + per_step - 1) // per_step, MOBA_FAR_TILE, MOBA_BLOCK, lhs_ref, k_ref, v_ref,
              (s0_ref, s1_ref), (p0_ref, p1_ref), (a0_ref, a1_ref), m_ref, acc_ref)

    lhs_ref[:, LANES:2 * LANES] = jnp.where(past | (lane2 == c), 0.0, NEG_INF).astype(BF16)
    prev_start = pl.multiple_of(jnp.maximum(c - 1, 0) * tq, tq)
    diag_start = pl.multiple_of(c * tq, tq)
    hot = jnp.concatenate([_lane_onehot(tq, jnp.where(c >= 1, c - 1, LANES - 1)), _lane_onehot(tq, c)], axis=0)
    knear = jnp.concatenate([k_ref[pl.ds(prev_start, tq), :], k_ref[pl.ds(diag_start, tq), :]], axis=0)
    vnear = jnp.concatenate([v_ref[pl.ds(prev_start, tq), :], v_ref[pl.ds(diag_start, tq), :]], axis=0)
    s = _qk(lhs_ref[...], jnp.concatenate([knear, hot], axis=1))
    s = s + jnp.concatenate([dp_ref[...], dd_ref[...]], axis=1)
    _flash_step(s, _with_ones(vnear), m_ref, acc_ref)

    o = _flash_out(acc_ref)
    o_ref[...] = jnp.where(half == 0, o[0:tq], o[tq:2 * tq]).astype(o_ref.dtype)


def _moba_attention(proj, kmean, dprev, ddiag):
    b, s, _ = proj.shape
    tq = ATT_TILE
    tkf = MOBA_FAR_TILE
    rows = 2 * tq
    npair = MOBA_HEADS // 2

    def col(name):
        return SEC[name][0] // LANES

    return pl.pallas_call(
        _moba_kernel,
        grid=(b, npair, s // tq),
        in_specs=[pl.BlockSpec((None, tq, LANES), lambda i, p, c: (i, c, col("c_q") + p)),
                  _const_spec((None, s, LANES), lambda i, p, c: (i, 0, col("c_k") + p)),
                  _const_spec((None, s, LANES), lambda i, p, c: (i, 0, col("c_v") + p)),
                  _const_spec((None, LANES, LANES), lambda i, p, c: (i, 0, p)),
                  _const_spec((None, rows, tq), lambda i, p, c: (p, 0, 0)),
                  _const_spec((None, rows, tq), lambda i, p, c: (p, 0, 0))],
        out_specs=pl.BlockSpec((None, tq, LANES), lambda i, p, c: (i, c, p)),
        out_shape=jax.ShapeDtypeStruct((b, s, 512), BF16),
        scratch_shapes=[pltpu.VMEM((rows, 2 * LANES), BF16),
                        pltpu.VMEM((rows, LANES), F32), pltpu.VMEM((rows, 2 * LANES), F32),
                        pltpu.VMEM((rows, tkf), F32), pltpu.VMEM((rows, tkf), F32),
                        pltpu.VMEM((rows, tkf), BF16), pltpu.VMEM((rows, tkf), BF16),
                        pltpu.VMEM((rows, LANES), F32), pltpu.VMEM((rows, LANES), F32)],
        compiler_params=_cparams(("parallel", "parallel", "arbitrary")),
        name="moba_attention",
    )(proj, proj, proj, kmean, dprev, ddiag)


def _merge_kernel(oa_ref, ob_ref, oc_ref, g0_ref, g1_ref, g2_ref, x_ref, wa_ref, wb_ref, wc_ref, wo_ref, nw_ref,
                  o_ref):
    def branch(o, w, g):
        return jax.nn.sigmoid(g[...].astype(F32)) * jnp.dot(o[...], w[...], preferred_element_type=F32)

    merged = branch(oa_ref, wa_ref, g0_ref) + branch(ob_ref, wb_ref, g1_ref) + branch(oc_ref, wc_ref, g2_ref)
    y = jnp.dot(merged.astype(BF16), wo_ref[...], preferred_element_type=F32)
    o_ref[...] = x_ref[...] + _rms(y, nw_ref[...])


def _merge(oa, ob, oc, proj, x, wa, wb, wc, wo, nw, tm):
    t, d = x.shape
    w = oa.shape[1]
    row = lambda width, j=0: pl.BlockSpec((tm, width), lambda i, j=j: (i, j))
    const = lambda shape: pl.BlockSpec(shape, lambda i: (0, 0))
    return pl.pallas_call(
        _merge_kernel,
        grid=(t // tm,),
        in_specs=[row(w), row(w), row(w), row(d, 0), row(d, 1), row(d, 2), row(d),
                  const((w, d)), const((w, d)), const((w, d)), const((d, d)), const((1, d))],
        out_specs=row(d),
        out_shape=jax.ShapeDtypeStruct((t, d), F32),
        compiler_params=_cparams(("parallel",)),
        name="merge_out",
    )(oa, ob, oc, proj, proj, proj, x, wa, wb, wc, wo, nw.reshape(1, d))


def _ffn_kernel(x_ref, npre_ref, wg_ref, wu_ref, wo_ref, npost_ref, o_ref, h_ref, acc_ref):
    j = pl.program_id(1)

    @pl.when(j == 0)
    def _():
        h_ref[...] = _rms(x_ref[...], npre_ref[...]).astype(BF16)
        acc_ref[...] = jnp.zeros(acc_ref.shape, F32)

    h = h_ref[...]
    gate = jnp.dot(h, wg_ref[...], preferred_element_type=F32)
    up = jnp.dot(h, wu_ref[...], preferred_element_type=F32)
    act = (gate * jax.nn.sigmoid(gate) * up).astype(BF16)
    acc_ref[...] += jnp.dot(act, wo_ref[...], preferred_element_type=F32)

    @pl.when(j == pl.num_programs(1) - 1)
    def _():
        o_ref[...] = x_ref[...] + _rms(acc_ref[...], npost_ref[...])


def _ffn(x, npre, wg, wu, wo, npost, tm, th):
    t, d = x.shape
    hid = wg.shape[1]
    return pl.pallas_call(
        _ffn_kernel,
        grid=(t // tm, hid // th),
        in_specs=[pl.BlockSpec((tm, d), lambda i, j: (i, 0)),
                  pl.BlockSpec((1, d), lambda i, j: (0, 0)),
                  pl.BlockSpec((d, th), lambda i, j: (0, j)),
                  pl.BlockSpec((d, th), lambda i, j: (0, j)),
                  pl.BlockSpec((th, d), lambda i, j: (j, 0)),
                  pl.BlockSpec((1, d), lambda i, j: (0, 0))],
        out_specs=pl.BlockSpec((tm, d), lambda i, j: (i, 0)),
        out_shape=jax.ShapeDtypeStruct((t, d), F32),
        scratch_shapes=[pltpu.VMEM((tm, d), BF16), pltpu.VMEM((tm, d), F32)],
        compiler_params=_cparams(("parallel", "arbitrary")),
        name="ffn",
    )(x, npre.reshape(1, d), wg, wu, wo, npost.reshape(1, d))


def _t5_bucket_table(dist):
    n = np.maximum(dist, 0)
    max_exact = NUM_BUCKETS // 2
    log_ratio = np.log(np.maximum(n, 1).astype(np.float32) / max_exact) / math.log(T5_MAX_DISTANCE / max_exact)
    large = max_exact + (log_ratio * (NUM_BUCKETS - max_exact)).astype(np.int32)
    return np.where(n < max_exact, n, np.minimum(large, NUM_BUCKETS - 1)).astype(np.int32)


def _bias_lookup(rel, dist):
    onehot = (jnp.asarray(_t5_bucket_table(dist))[..., None] == jnp.arange(NUM_BUCKETS)).astype(F32)
    return jnp.dot(onehot, (rel - rel[NUM_BUCKETS - 1]) * LOG2E, precision=lax.Precision.HIGHEST)


def _near_bias(rel, t):
    qi = np.arange(t)[:, None]
    ki = np.arange(t)[None, :]
    prev = _bias_lookup(rel, qi + t - ki)
    diag = jnp.where((qi >= ki)[..., None], _bias_lookup(rel, qi - ki), NEG_INF)
    flat = lambda b: b.transpose(2, 0, 1).reshape(-1, t)
    return flat(prev), flat(diag)


def _cmp_bias(rel, t):
    w = t // CMP_STRIDE
    qi = np.arange(t)[:, None]
    j = np.arange(LANES)[None, :]
    d = qi - CMP_STRIDE * (j - w) - (CMP_LEN - 1)
    b = jnp.where(((j < 2 * w) & (d >= 0))[..., None], _bias_lookup(rel, d),
                  jnp.asarray(np.where(j <= 2 * w, NEG_INF, 0.0) * np.ones_like(d), F32)[..., None])
    return b.transpose(2, 0, 1).reshape(-1, LANES).astype(BF16)


def _overlap(s, nc):
    n_cmp = (s - CMP_LEN) // CMP_STRIDE + 1
    cmp_end = np.arange(nc) * CMP_STRIDE + CMP_LEN - 1
    cmp_start = cmp_end - (CMP_LEN - 1)
    sb_start = np.arange(LANES) * SEL_BLOCK
    ov = (cmp_start[:, None] < sb_start[None, :] + SEL_BLOCK) & (cmp_end[:, None] >= sb_start[None, :])
    ov = ov & (np.arange(nc)[:, None] < n_cmp) & (np.arange(LANES)[None, :] < s // SEL_BLOCK)
    return jnp.asarray(ov, BF16)


def _pair_diag(w):
    z = jnp.zeros_like(w)
    return jnp.concatenate([jnp.concatenate([w, z], axis=-1), jnp.concatenate([z, w], axis=-1)], axis=-2)


def kernel(x, rel_bias, norm_mix_pre, norm_mix_post, norm_ffn_pre, norm_ffn_post, w_in, nsa_pe_k, nsa_pe_v, nsa_cmp_k_w1, nsa_cmp_k_w2, nsa_cmp_v_w1, nsa_cmp_v_w2, gla_gate_w2, gla_gate_b, gla_norm, w_branch_a, w_branch_b, w_branch_c, w_out, w_ffn_in, w_ffn_out):
    b, s, d = x.shape
    depth = w_in.shape[0]
    t = b * s
    dk = NSA_HEAD_DIM
    tq = ATT_TILE
    assert d == D_MODEL and w_in.shape[2] == D_IN
    assert WINDOW == 2 * tq and MOBA_BLOCK == tq and s % MOBA_FAR_TILE == 0
    assert SEL_TOPK <= s // SEL_BLOCK <= LANES and s // MOBA_BLOCK <= LANES
    nc = s // CMP_STRIDE
    assert nc % LANES == 0

    w_in_p = _layout_w_in(w_in)
    rel_a = rel_bias[:, :NSA_HEADS]
    rel_c = rel_bias[:, NSA_HEADS:]
    dprev_a, ddiag_a = _near_bias(rel_a, tq)
    dprev_c, ddiag_c = (tab.reshape(MOBA_HEADS // 2, 2 * tq, tq) for tab in _near_bias(rel_c, tq))
    tcmp = _cmp_bias(rel_a, tq)
    overlap = _overlap(s, nc)
    wtab = jnp.asarray(np.where(np.arange(tq)[None, :] > np.arange(tq)[:, None], 0.0, NEG_INF), F32)
    avg = jnp.asarray((np.arange(LANES)[:, None] == np.arange(s)[None, :] // MOBA_BLOCK) / MOBA_BLOCK, BF16)

    def cmp_w1(w1):
        w = _pair_diag(w1.reshape(depth, 2, CMP_STRIDE, dk, dk))
        return w.reshape(depth, 2, CMP_STRIDE * 2 * dk, 2 * dk)

    def cmp_pe(pe):
        p2 = jnp.concatenate([pe, pe], axis=-1).reshape(depth, 2, 1, CMP_STRIDE * 2 * dk)
        return jnp.broadcast_to(p2, (depth, 2, 8, CMP_STRIDE * 2 * dk))

    cw1 = jnp.stack([cmp_w1(nsa_cmp_k_w1), cmp_w1(nsa_cmp_v_w1)], axis=1).astype(BF16)
    cw2 = jnp.stack([_pair_diag(nsa_cmp_k_w2), _pair_diag(nsa_cmp_v_w2)], axis=1).astype(BF16)
    cpe = jnp.stack([cmp_pe(nsa_pe_k), cmp_pe(nsa_pe_v)], axis=1).astype(BF16)

    gw2 = jnp.concatenate([gla_gate_w2, jnp.zeros((depth, LANES - GLA_GATE_RANK, gla_gate_w2.shape[2]), F32)],
                          axis=1).astype(BF16)
    wa = (w_branch_a.reshape(depth, NSA_KV_GROUPS, NSA_REP, dk, d).transpose(0, 2, 1, 3, 4)
          .reshape(depth, NSA_HEADS * dk, d).astype(BF16))
    wb = w_branch_b.astype(BF16)
    wc = w_branch_c.astype(BF16)
    wo = w_out.astype(BF16)
    hid = w_ffn_out.shape[1]
    wg = w_ffn_in[:, :, :hid].astype(BF16)
    wu = w_ffn_in[:, :, hid:].astype(BF16)
    wf = w_ffn_out.astype(BF16)

    tm = min(1024, t)
    tn = NP_COLS // 5
    xf = x.reshape(t, d)
    for layer in range(depth):
        proj = _norm_matmul(xf, norm_mix_pre[layer], w_in_p[layer], tm, tn).reshape(b, s, NP_COLS)
        kc0, vc0 = SEC["a_kc"][0], SEC["a_vc"][0]
        xkv = jnp.stack([proj[:, :, kc0:kc0 + LANES], proj[:, :, vc0:vc0 + LANES]])
        xkv = xkv.reshape(2, b, nc, CMP_STRIDE * LANES)
        cmp = _compress(xkv, cpe[layer], cw1[layer], cw2[layer])
        o_a = _nsa_attention(proj, cmp[0], cmp[1], overlap, tcmp, dprev_a, ddiag_a, wtab)
        o_b = _gla(proj, gw2[layer], gla_gate_b[layer].reshape(1, -1), gla_norm[layer].reshape(1, -1),
                   min(256, s))
        kmean = _moba_kmean(proj, avg)
        o_c = _moba_attention(proj, kmean, dprev_c, ddiag_c)
        xf = _merge(o_a.reshape(t, -1), o_b.reshape(t, -1), o_c.reshape(t, -1), proj.reshape(t, NP_COLS), xf,
                    wa[layer], wb[layer], wc[layer], wo[layer], norm_mix_post[layer], min(512, t))
        xf = _ffn(xf, norm_ffn_pre[layer], wg[layer], wu[layer], wf[layer], norm_ffn_post[layer],
                  min(512, t), hid // 2)
    return xf.reshape(b, s, d)
```

```python
import math

import numpy as np
import jax
import jax.numpy as jnp
from jax import lax
from jax.experimental import pallas as pl
from jax.experimental.pallas import tpu as pltpu

F32 = jnp.float32
BF16 = jnp.bfloat16

NORM_EPS = 1e-6
NEG_INF = -1e30
TOP_BONUS = 1e9
NUM_BUCKETS = 32
T5_MAX_DISTANCE = 128
NSA_HEADS = 8
NSA_KV_GROUPS = 2
NSA_REP = NSA_HEADS // NSA_KV_GROUPS
NSA_HEAD_DIM = 64
CMP_LEN = 32
CMP_STRIDE = 16
SEL_BLOCK = 64
SEL_TOPK = 16
WINDOW = 512
N_NSA_BRANCH = 3
GLA_HEADS = 4
GLA_KEY_DIM = 64
GLA_VAL_DIM = 128
GLA_GATE_RANK = 16
GLA_GATE_TAU = 16.0
GLA_CHUNK = 64
MOBA_HEADS = 8
MOBA_HEAD_DIM = 64
MOBA_BLOCK = 256
MOBA_TOPK = 3
N_BRANCHES = 3
LOG2E = math.log2(math.e)

LANES = 128
ATT_TILE = 256
FAR_TILE = 2 * ATT_TILE
VMEM_LIMIT = 56 * 1024 * 1024

D_MODEL = 1024
SEC = {}
_off = 0
for _name, _w in (("mg", 3 * D_MODEL), ("a_q", 512), ("c_q", 512), ("c_k", 512), ("c_v", 512),
                  ("b_v", 512), ("b_r", 512), ("b_q", 256), ("b_k", 256),
                  ("a_kc", 128), ("a_vc", 128), ("a_ks", 128), ("a_vs", 128), ("a_kw", 128), ("a_vw", 128),
                  ("a_g", 128), ("b_lr", 128)):
    SEC[_name] = (_off, _w)
    _off += _w
NP_COLS = _off

_SRC_NAMES = ("a_q", "a_kc", "a_vc", "a_ks", "a_vs", "a_kw", "a_vw", "a_g", "b_q", "b_k", "b_v", "b_r", "b_lr",
              "c_q", "c_k", "c_v", "mg")
_SRC_SIZES = (512, 128, 128, 128, 128, 128, 128, 24, 256, 256, 512, 512, 16, 512, 512, 512, 3 * D_MODEL)
_SRC_OFF = dict(zip(_SRC_NAMES, np.cumsum((0,) + _SRC_SIZES[:-1]).tolist()))
_SRC_W = dict(zip(_SRC_NAMES, _SRC_SIZES))
D_IN = int(sum(_SRC_SIZES))


def _layout_w_in(w_in):
    depth, d, _ = w_in.shape

    def src(name, lo=0, hi=None):
        hi = _SRC_W[name] if hi is None else hi
        return w_in[:, :, _SRC_OFF[name] + lo:_SRC_OFF[name] + hi]

    parts = []
    for name, (_, width) in SEC.items():
        if name == "a_q":
            for r in range(NSA_REP):
                for g in range(NSA_KV_GROUPS):
                    h = g * NSA_REP + r
                    parts.append(src(name, h * NSA_HEAD_DIM, (h + 1) * NSA_HEAD_DIM) * (NSA_HEAD_DIM ** -0.5 * LOG2E))
        elif name == "c_q":
            parts.append(src(name) * (MOBA_HEAD_DIM ** -0.5 * LOG2E))
        else:
            parts.append(src(name))
            if _SRC_W[name] < width:
                parts.append(jnp.zeros((depth, d, width - _SRC_W[name]), w_in.dtype))
    return jnp.concatenate(parts, axis=2).astype(BF16)


def _cparams(sem):
    return pltpu.CompilerParams(dimension_semantics=sem, vmem_limit_bytes=VMEM_LIMIT)


def _const_spec(shape, index_map):
    return pl.BlockSpec(shape, index_map, pipeline_mode=pl.Buffered(1))


def _rms(y, w):
    return y * lax.rsqrt(jnp.mean(y * y, axis=-1, keepdims=True) + NORM_EPS) * w


def _norm_matmul_kernel(x_ref, nw_ref, w_ref, o_ref, h_ref):
    @pl.when(pl.program_id(1) == 0)
    def _():
        h_ref[...] = _rms(x_ref[...], nw_ref[...]).astype(BF16)

    o_ref[...] = jnp.dot(h_ref[...], w_ref[...], preferred_element_type=F32).astype(o_ref.dtype)


def _norm_matmul(x, nw, w, tm, tn):
    t, d = x.shape
    n = w.shape[1]
    return pl.pallas_call(
        _norm_matmul_kernel,
        grid=(t // tm, n // tn),
        in_specs=[pl.BlockSpec((tm, d), lambda i, j: (i, 0)),
                  pl.BlockSpec((1, d), lambda i, j: (0, 0)),
                  pl.BlockSpec((d, tn), lambda i, j: (0, j))],
        out_specs=pl.BlockSpec((tm, tn), lambda i, j: (i, j)),
        out_shape=jax.ShapeDtypeStruct((t, n), BF16),
        scratch_shapes=[pltpu.VMEM((tm, d), BF16)],
        compiler_params=_cparams(("parallel", "arbitrary")),
        name="norm_proj",
    )(x, nw.reshape(1, d), w)


def _compress_kernel(x_ref, pe_ref, w1_ref, w2_ref, o_ref):
    x = x_ref[...]
    nc = x.shape[0]
    w1t = w1_ref[0]
    w1b = w1_ref[1]
    a = jnp.dot(x, w1t, preferred_element_type=F32)
    b = jnp.dot(x, w1b, preferred_element_type=F32)
    pe = pe_ref[...]
    pe_term = (jnp.dot(pe[0], w1t, preferred_element_type=F32)
               + jnp.dot(pe[1], w1b, preferred_element_type=F32))[0:1]
    pre = a + pltpu.roll(b, nc - 1, 0) + pe_term
    hid = jax.nn.gelu(pre)
    o_ref[...] = jnp.dot(hid.astype(BF16), w2_ref[...], preferred_element_type=F32).astype(o_ref.dtype)


def _compress(xkv, pe, w1, w2):
    _, b, nc, kw = xkv.shape
    return pl.pallas_call(
        _compress_kernel,
        grid=(2, b),
        in_specs=[pl.BlockSpec((None, None, nc, kw), lambda s, i: (s, i, 0, 0)),
                  pl.BlockSpec((None, 2, 8, kw), lambda s, i: (s, 0, 0, 0)),
                  pl.BlockSpec((None, 2, kw, LANES), lambda s, i: (s, 0, 0, 0)),
                  pl.BlockSpec((None, LANES, LANES), lambda s, i: (s, 0, 0))],
        out_specs=pl.BlockSpec((None, None, nc, LANES), lambda s, i: (s, i, 0, 0)),
        out_shape=jax.ShapeDtypeStruct((2, b, nc, LANES), BF16),
        compiler_params=_cparams(("parallel", "parallel")),
        name="nsa_compress",
    )(xkv, pe, w1, w2)


def _with_ones(v):
    return jnp.concatenate([v, jnp.ones(v.shape, v.dtype)], axis=1)


def _softmax_values(s, v1):
    p = jnp.exp2(s - jnp.max(s, axis=-1, keepdims=True)).astype(BF16)
    r = jnp.dot(p, v1, preferred_element_type=F32)
    return r[:, 0:LANES] / r[:, LANES:2 * LANES]


def _qk(lhs, rhs):
    return lax.dot_general(lhs, rhs, (((1,), (1,)), ((), ())), preferred_element_type=F32)


def _lane_onehot(width, lane_idx):
    klane = lax.broadcasted_iota(jnp.int32, (width, LANES), 1)
    return (klane == lane_idx).astype(BF16)


def _pipelined_attention(c, block, groups, lhs_ref, write_near_mask, k_ref, v_ref, dp_ref, dd_ref,
                         s_refs, p_refs, a_refs, m_ref, acc_ref):
    tq = ATT_TILE
    tk = FAR_TILE
    rows = lhs_ref.shape[0]
    grows = rows // groups
    j_max = k_ref.shape[0] // tk - 1
    per = tk // block
    n_far = (jnp.maximum(c - 1, 0) * tq + tk - 1) // tk
    n_loop = 2 * ((n_far + 1) // 2)
    krow = lax.broadcasted_iota(jnp.int32, (tk, LANES), 0) // block
    klane = lax.broadcasted_iota(jnp.int32, (tk, LANES), 1)

    m_ref[...] = jnp.full(m_ref.shape, NEG_INF, F32)
    acc_ref[...] = jnp.zeros(acc_ref.shape, F32)
    s_refs[1][...] = jnp.full((rows, tk), -jnp.inf, F32)
    p_refs[0][...] = jnp.zeros((rows, tk), BF16)
    a_refs[0][...] = jnp.ones((rows, LANES), F32)

    def far_rows(j):
        start = pl.multiple_of(jnp.clip(j, 0, j_max) * tk, tk)
        return lambda ref, g: ref[pl.ds(start, tk), g * LANES:(g + 1) * LANES]

    def near_rows(ref, g):
        cols = slice(g * LANES, (g + 1) * LANES)
        prev = ref[pl.ds(pl.multiple_of(jnp.maximum(c - 1, 0) * tq, tq), tq), cols]
        return jnp.concatenate([prev, ref[pl.ds(pl.multiple_of(c * tq, tq), tq), cols]], axis=0)

    def value_stage(slot, window):
        for g in range(groups):
            gr = slice(g * grows, (g + 1) * grows)
            acc_ref[gr, :] = (jnp.tile(a_refs[slot][gr, :], (1, 2)) * acc_ref[gr, :]
                              + jnp.dot(p_refs[slot][gr, :], _with_ones(window(v_ref, g)),
                                        preferred_element_type=F32))

    def softmax_stage(slot, table=None):
        s = s_refs[slot][...]
        if table is not None:
            s = s + table
        m_prev = m_ref[...]
        m_new = jnp.maximum(m_prev, jnp.max(s, axis=-1, keepdims=True))
        a_refs[slot][...] = jnp.exp2(m_prev - m_new)
        p_refs[slot][...] = jnp.exp2(s - jnp.tile(m_new, (1, tk // LANES))).astype(BF16)
        m_ref[...] = m_new

    def score_stage(slot, window, lanes):
        onehot = (klane == lanes).astype(BF16)
        for g in range(groups):
            gr = slice(g * grows, (g + 1) * grows)
            s_refs[slot][gr, :] = _qk(lhs_ref[gr, :], jnp.concatenate([window(k_ref, g), onehot], axis=1))

    def loop_stage(i, slot):
        value_stage(slot, far_rows(i - 2))
        softmax_stage(1 - slot)
        score_stage(slot, far_rows(i), jnp.where(i < n_far, krow + i * per, LANES - 1))

    def body(ii, carry):
        loop_stage(2 * ii, 0)
        loop_stage(2 * ii + 1, 1)
        return carry

    lax.fori_loop(0, n_loop // 2, body, 0)

    value_stage(0, far_rows(n_loop - 2))
    softmax_stage(1)
    write_near_mask()
    near_lanes = jnp.where((krow >= tq // block) | (c >= 1), krow + (c - 1) * (tq // block), LANES - 1)
    score_stage(0, near_rows, near_lanes)
    value_stage(1, far_rows(n_loop - 1))
    softmax_stage(0, jnp.concatenate([dp_ref[...], dd_ref[...]], axis=1))
    value_stage(0, near_rows)
    return acc_ref[:, 0:LANES] / acc_ref[:, LANES:2 * LANES]


def _topk_mask(vals, k):
    lane = lax.broadcasted_iota(jnp.int32, vals.shape, 1).astype(F32)
    sel = jnp.zeros(vals.shape, jnp.bool_)
    for _ in range(k):
        m = jnp.max(vals, axis=-1, keepdims=True)
        idx = jnp.min(jnp.where(vals == m, lane, float(LANES)), axis=-1, keepdims=True)
        hit = lane == idx
        sel = jnp.logical_or(sel, hit)
        vals = jnp.where(hit, -jnp.inf, vals)
    return sel


def _nsa_kernel(q_ref, ks_ref, vs_ref, kw_ref, vw_ref, g_ref, kc_ref, vc_ref, ov_ref, tc_ref, dp_ref, dd_ref, wt_ref,
                o_ref, lhs_ref, nm_ref, m_ref, acc_ref, out_ref, s0_ref, s1_ref, p0_ref, p1_ref, a0_ref, a1_ref):
    tq = ATT_TILE
    nh = NSA_HEADS
    rows = nh * tq
    c = pl.program_id(1)
    t0 = c * tq
    ncp = kc_ref.shape[0]
    wcols = 2 * (tq // CMP_STRIDE)

    lane = lax.broadcasted_iota(jnp.int32, (tq, LANES), 1)
    rowi = lax.broadcasted_iota(jnp.int32, (tq, LANES), 0)
    half = lane // NSA_HEAD_DIM

    q = q_ref[...]
    for g in range(NSA_KV_GROUPS):
        for r in range(NSA_REP):
            h = g * NSA_REP + r
            qb = q[:, r * LANES:(r + 1) * LANES]
            lhs_ref[h * tq:(h + 1) * tq, 0:LANES] = jnp.where(half == g, qb, jnp.zeros_like(qb))
    qrows = lhs_ref[:, 0:LANES]

    gates = jax.nn.sigmoid(g_ref[...].astype(F32))

    def gate_col(h, br):
        col = h * N_NSA_BRANCH + br
        return gates[:, col:col + 1]

    prev_start = pl.multiple_of(jnp.maximum(c - 1, 0) * tq, tq)
    diag_start = pl.multiple_of(c * tq, tq)

    wpat = jnp.where(lax.broadcasted_iota(jnp.int32, (rows, LANES), 1) == 1, NEG_INF, 0.0).astype(BF16)
    back2_start = pl.multiple_of(jnp.maximum(c - 2, 0) * tq, tq)
    kwin = jnp.concatenate([kw_ref[pl.ds(back2_start, tq), :], kw_ref[pl.ds(prev_start, tq), :],
                            kw_ref[pl.ds(diag_start, tq), :]], axis=0)
    vwin = jnp.concatenate([vw_ref[pl.ds(back2_start, tq), :], vw_ref[pl.ds(prev_start, tq), :],
                            vw_ref[pl.ds(diag_start, tq), :]], axis=0)
    hot = jnp.concatenate([_lane_onehot(tq, jnp.where(c >= 2, 0, 1)), _lane_onehot(tq, jnp.where(c >= 1, 0, 1)),
                           _lane_onehot(tq, 0)], axis=0)
    s = _qk(jnp.concatenate([qrows, wpat], axis=1), jnp.concatenate([kwin, hot], axis=1))
    s = s + jnp.concatenate([jnp.tile(wt_ref[...], (nh, 1)), dp_ref[...], dd_ref[...]], axis=1)
    o_win = _softmax_values(s, _with_ones(vwin))
    for h in range(nh):
        sl = slice(h * tq, (h + 1) * tq)
        out_ref[sl, :] = gate_col(h, 2) * o_win[sl]

    ii = lax.broadcasted_iota(jnp.int32, (ncp, LANES), 0) - (c * (tq // CMP_STRIDE) - tq // CMP_STRIDE)
    ww = lax.broadcasted_iota(jnp.int32, (ncp, LANES), 1)
    place = (((ww < wcols) & (ii == ww)) | ((ww == wcols) & (ii >= wcols))).astype(BF16)
    s = _qk(jnp.concatenate([qrows, tc_ref[...]], axis=1), jnp.concatenate([kc_ref[...], place], axis=1))
    m = jnp.max(s, axis=-1, keepdims=True)
    p = jnp.exp2(s - m)
    p_hi = p.astype(BF16)
    p_lo = (p - p_hi.astype(F32)).astype(BF16)
    ov = ov_ref[...]
    r1 = jnp.dot(p_hi, jnp.concatenate([_with_ones(vc_ref[...]), ov], axis=1), preferred_element_type=F32)
    r2 = jnp.dot(p_lo, ov, preferred_element_type=F32)
    has_key = m > 0.5 * NEG_INF
    inv_l = jnp.where(has_key, 1.0 / r1[:, LANES:2 * LANES], 0.0)
    o_cmp = r1[:, 0:LANES] * inv_l
    imp_h = (r1[:, 2 * LANES:3 * LANES] + r2) * inv_l
    for h in range(nh):
        sl = slice(h * tq, (h + 1) * tq)
        out_ref[sl, :] = out_ref[sl, :] + gate_col(h, 0) * o_cmp[sl]

    cur = (rowi + t0) // SEL_BLOCK
    forced = (lane == 0) | (lane == cur) | (lane == cur - 1)
    far_blocks = (c - 1) * (tq // SEL_BLOCK)
    for g in range(NSA_KV_GROUPS):
        imp = imp_h[g * NSA_REP * tq:(g * NSA_REP + 1) * tq]
        for r in range(1, NSA_REP):
            imp = imp + imp_h[(g * NSA_REP + r) * tq:(g * NSA_REP + r + 1) * tq]
        imp = jnp.where(forced, TOP_BONUS, jnp.where(lane <= cur, imp, NEG_INF))
        sel = _topk_mask(imp, SEL_TOPK)
        nm_ref[g] = jnp.where(sel, 0.0, NEG_INF).astype(BF16)
        far_mask = jnp.where(sel & (lane < far_blocks), 0.0, NEG_INF).astype(BF16)
        for r in range(NSA_REP):
            h = g * NSA_REP + r
            lhs_ref[h * tq:(h + 1) * tq, LANES:2 * LANES] = far_mask

    def write_near_mask():
        for h in range(nh):
            lhs_ref[h * tq:(h + 1) * tq, LANES:2 * LANES] = nm_ref[h // NSA_REP]

    o_sel = _pipelined_attention(c, SEL_BLOCK, 1, lhs_ref, write_near_mask, ks_ref, vs_ref, dp_ref, dd_ref,
                                 (s0_ref, s1_ref), (p0_ref, p1_ref), (a0_ref, a1_ref), m_ref, acc_ref)

    for r in range(NSA_REP):
        h0, h1 = r, NSA_REP + r
        o0 = out_ref[h0 * tq:(h0 + 1) * tq, :] + gate_col(h0, 1) * o_sel[h0 * tq:(h0 + 1) * tq]
        o1 = out_ref[h1 * tq:(h1 + 1) * tq, :] + gate_col(h1, 1) * o_sel[h1 * tq:(h1 + 1) * tq]
        o_ref[:, r * LANES:(r + 1) * LANES] = jnp.where(half == 0, o0, o1).astype(o_ref.dtype)


def _nsa_attention(proj, kcmp, vcmp, overlap, tcmp, dprev, ddiag, wtab):
    b, s, _ = proj.shape
    tq = ATT_TILE
    ncp = kcmp.shape[1]
    rows = NSA_HEADS * tq

    def col(name, width):
        return SEC[name][0] // width

    full = lambda name: _const_spec((None, s, LANES), lambda i, c, n=name: (i, 0, col(n, LANES)))
    return pl.pallas_call(
        _nsa_kernel,
        grid=(b, s // tq),
        in_specs=[pl.BlockSpec((None, tq, 512), lambda i, c: (i, c, col("a_q", 512))),
                  full("a_ks"), full("a_vs"), full("a_kw"), full("a_vw"),
                  pl.BlockSpec((None, tq, LANES), lambda i, c: (i, c, col("a_g", LANES))),
                  _const_spec((None, ncp, LANES), lambda i, c: (i, 0, 0)),
                  _const_spec((None, ncp, LANES), lambda i, c: (i, 0, 0)),
                  _const_spec((ncp, LANES), lambda i, c: (0, 0)),
                  _const_spec((rows, LANES), lambda i, c: (0, 0)),
                  _const_spec((rows, tq), lambda i, c: (0, 0)),
                  _const_spec((rows, tq), lambda i, c: (0, 0)),
                  _const_spec((tq, tq), lambda i, c: (0, 0))],
        out_specs=pl.BlockSpec((None, tq, 512), lambda i, c: (i, c, 0)),
        out_shape=jax.ShapeDtypeStruct((b, s, 512), BF16),
        scratch_shapes=[pltpu.VMEM((rows, 2 * LANES), BF16),
                        pltpu.VMEM((NSA_KV_GROUPS, tq, LANES), BF16),
                        pltpu.VMEM((rows, LANES), F32), pltpu.VMEM((rows, 2 * LANES), F32),
                        pltpu.VMEM((rows, LANES), F32),
                        pltpu.VMEM((rows, FAR_TILE), F32), pltpu.VMEM((rows, FAR_TILE), F32),
                        pltpu.VMEM((rows, FAR_TILE), BF16), pltpu.VMEM((rows, FAR_TILE), BF16),
                        pltpu.VMEM((rows, LANES), F32), pltpu.VMEM((rows, LANES), F32)],
        compiler_params=_cparams(("parallel", "arbitrary")),
        name="nsa_attention",
    )(proj, proj, proj, proj, proj, proj, kcmp, vcmp, overlap, tcmp, dprev, ddiag, wtab)


def _gla_kernel(q_ref, k_ref, v_ref, r_ref, lr_ref, w2_ref, gb_ref, gn_ref, o_ref, st_ref):
    ch = GLA_CHUNK
    lb = q_ref.shape[0]
    hp = lax.Precision.HIGHEST

    @pl.when(pl.program_id(1) == 0)
    def _():
        st_ref[...] = jnp.zeros(st_ref.shape, F32)

    x = jnp.dot(lr_ref[...], w2_ref[...], preferred_element_type=F32) + gb_ref[...]
    log_a = (jnp.minimum(x, 0.0) - jnp.log1p(jnp.exp(-jnp.abs(x)))) / GLA_GATE_TAU

    ti = lax.broadcasted_iota(jnp.int32, (ch, ch), 0)
    tj = lax.broadcasted_iota(jnp.int32, (ch, ch), 1)
    tril = tj <= ti
    tri = tril.astype(BF16)
    half = lax.broadcasted_iota(jnp.int32, (ch, LANES), 1) // GLA_KEY_DIM
    eye = (lax.broadcasted_iota(jnp.int32, (LANES, LANES), 0)
           == lax.broadcasted_iota(jnp.int32, (LANES, LANES), 1))
    gn = gn_ref[...]

    for cc in range(lb // ch):
        sl = slice(cc * ch, (cc + 1) * ch)
        bcum = _split3_dot_left(tri, log_a[sl])
        for p in range(GLA_HEADS // 2):
            cols = slice(p * LANES, (p + 1) * LANES)
            bc = bcum[:, cols]
            blast = bc[ch - 1:ch, :]
            qf = q_ref[sl, cols].astype(F32) * (GLA_KEY_DIM ** -0.5)
            kf = k_ref[sl, cols].astype(F32)
            qe = qf * jnp.exp(bc)
            kinv = kf * jnp.exp(-bc)
            klast = kf * jnp.exp(blast - bc)
            decay = jnp.where(eye, jnp.broadcast_to(jnp.exp(blast), (LANES, LANES)), 0.0)
            for a in range(2):
                h = 2 * p + a
                hc = slice(h * LANES, (h + 1) * LANES)
                qa = jnp.where(half == a, qe, 0.0)
                attn = lax.dot_general(qa, kinv, (((1,), (1,)), ((), ())), precision=hp,
                                       preferred_element_type=F32)
                attn = jnp.where(tril, attn, 0.0)
                v = v_ref[sl, hc].astype(F32)
                st = st_ref[h]
                o = (jnp.dot(qa, st, precision=hp, preferred_element_type=F32)
                     + jnp.dot(attn, v, precision=hp, preferred_element_type=F32))
                st_ref[h] = (jnp.dot(decay, st, precision=hp, preferred_element_type=F32)
                             + lax.dot_general(klast, v, (((0,), (0,)), ((), ())), precision=hp,
                                               preferred_element_type=F32))
                rg = r_ref[sl, hc].astype(F32)
                o_ref[sl, hc] = (_rms(o, gn) * (rg * jax.nn.sigmoid(rg))).astype(o_ref.dtype)


def _split3_dot_left(w, x):
    hi = x.astype(BF16)
    r1 = x - hi.astype(F32)
    mid = r1.astype(BF16)
    lo = (r1 - mid.astype(F32)).astype(BF16)
    return (jnp.dot(w, hi, preferred_element_type=F32) + jnp.dot(w, mid, preferred_element_type=F32)
            + jnp.dot(w, lo, preferred_element_type=F32))


def _gla(proj, w2, gb, gn, lb):
    b, s, _ = proj.shape

    def spec(name, width):
        return pl.BlockSpec((None, lb, width), lambda i, c, n=name, w=width: (i, c, SEC[n][0] // w))

    return pl.pallas_call(
        _gla_kernel,
        grid=(b, s // lb),
        in_specs=[spec("b_q", 256), spec("b_k", 256), spec("b_v", 512), spec("b_r", 512), spec("b_lr", LANES),
                  pl.BlockSpec((LANES, 256), lambda i, c: (0, 0)),
                  pl.BlockSpec((1, 256), lambda i, c: (0, 0)),
                  pl.BlockSpec((1, LANES), lambda i, c: (0, 0))],
        out_specs=pl.BlockSpec((None, lb, 512), lambda i, c: (i, c, 0)),
        out_shape=jax.ShapeDtypeStruct((b, s, 512), BF16),
        scratch_shapes=[pltpu.VMEM((GLA_HEADS, LANES, LANES), F32)],
        compiler_params=_cparams(("parallel", "arbitrary")),
        name="gla",
    )(proj, proj, proj, proj, proj, w2, gb, gn)


def _kmean_kernel(a_ref, k_ref, o_ref):
    o_ref[...] = jnp.dot(a_ref[...], k_ref[...], preferred_element_type=F32).astype(o_ref.dtype)


def _moba_kmean(proj, avg):
    b, s, _ = proj.shape
    return pl.pallas_call(
        _kmean_kernel,
        grid=(b,),
        in_specs=[pl.BlockSpec((LANES, s), lambda i: (0, 0)),
                  pl.BlockSpec((None, s, 512), lambda i: (i, 0, SEC["c_k"][0] // 512))],
        out_specs=pl.BlockSpec((None, LANES, 512), lambda i: (i, 0, 0)),
        out_shape=jax.ShapeDtypeStruct((b, LANES, 512), BF16),
        compiler_params=_cparams(("parallel",)),
        name="moba_kmean",
    )(avg, proj)


def _moba_kernel(q_ref, k_ref, v_ref, km_ref, dp_ref, dd_ref, o_ref, lhs_ref, m_ref, acc_ref,
                 s0_ref, s1_ref, p0_ref, p1_ref, a0_ref, a1_ref):
    tq = ATT_TILE
    nh = MOBA_HEADS
    npair = nh // 2
    rows = nh * tq
    c = pl.program_id(1)

    lane = lax.broadcasted_iota(jnp.int32, (tq, LANES), 1)
    half = lane // MOBA_HEAD_DIM
    for h in range(nh):
        qb = q_ref[:, (h // 2) * LANES:(h // 2 + 1) * LANES]
        lhs_ref[h * tq:(h + 1) * tq, 0:LANES] = jnp.where(half == h % 2, qb, jnp.zeros_like(qb))

    lane2 = lax.broadcasted_iota(jnp.int32, (rows, LANES), 1)
    score = jnp.concatenate(
        [_qk(lhs_ref[2 * p * tq:2 * (p + 1) * tq, 0:LANES], km_ref[:, p * LANES:(p + 1) * LANES])
         for p in range(npair)], axis=0)
    score = jnp.where(lane2 < c, score, NEG_INF)
    past = _topk_mask(score, MOBA_TOPK) & (lane2 < c)
    lhs_ref[:, LANES:2 * LANES] = jnp.where(past & (lane2 < c - 1), 0.0, NEG_INF).astype(BF16)

    def write_near_mask():
        lhs_ref[:, LANES:2 * LANES] = jnp.where(past | (lane2 == c), 0.0, NEG_INF).astype(BF16)

    o = _pipelined_attention(c, MOBA_BLOCK, npair, lhs_ref, write_near_mask, k_ref, v_ref, dp_ref, dd_ref,
                             (s0_ref, s1_ref), (p0_ref, p1_ref), (a0_ref, a1_ref), m_ref, acc_ref)
    for p in range(npair):
        o0 = o[2 * p * tq:(2 * p + 1) * tq]
        o1 = o[(2 * p + 1) * tq:(2 * p + 2) * tq]
        o_ref[:, p * LANES:(p + 1) * LANES] = jnp.where(half == 0, o0, o1).astype(o_ref.dtype)


def _moba_attention(proj, kmean, dprev, ddiag):
    b, s, _ = proj.shape
    tq = ATT_TILE
    rows = MOBA_HEADS * tq

    def col(name):
        return SEC[name][0] // 512

    return pl.pallas_call(
        _moba_kernel,
        grid=(b, s // tq),
        in_specs=[pl.BlockSpec((None, tq, 512), lambda i, c: (i, c, col("c_q"))),
                  _const_spec((None, s, 512), lambda i, c: (i, 0, col("c_k"))),
                  _const_spec((None, s, 512), lambda i, c: (i, 0, col("c_v"))),
                  _const_spec((None, LANES, 512), lambda i, c: (i, 0, 0)),
                  _const_spec((rows, tq), lambda i, c: (0, 0)),
                  _const_spec((rows, tq), lambda i, c: (0, 0))],
        out_specs=pl.BlockSpec((None, tq, 512), lambda i, c: (i, c, 0)),
        out_shape=jax.ShapeDtypeStruct((b, s, 512), BF16),
        scratch_shapes=[pltpu.VMEM((rows, 2 * LANES), BF16),
                        pltpu.VMEM((rows, LANES), F32), pltpu.VMEM((rows, 2 * LANES), F32),
                        pltpu.VMEM((rows, FAR_TILE), F32), pltpu.VMEM((rows, FAR_TILE), F32),
                        pltpu.VMEM((rows, FAR_TILE), BF16), pltpu.VMEM((rows, FAR_TILE), BF16),
                        pltpu.VMEM((rows, LANES), F32), pltpu.VMEM((rows, LANES), F32)],
        compiler_params=_cparams(("parallel", "arbitrary")),
        name="moba_attention",
    )(proj, proj, proj, kmean, dprev, ddiag)


def _merge_kernel(oa_ref, ob_ref, oc_ref, g0_ref, g1_ref, g2_ref, x_ref, wa_ref, wb_ref, wc_ref, wo_ref, nw_ref,
                  o_ref):
    def branch(o, w, g):
        return jax.nn.sigmoid(g[...].astype(F32)) * jnp.dot(o[...], w[...], preferred_element_type=F32)

    merged = branch(oa_ref, wa_ref, g0_ref) + branch(ob_ref, wb_ref, g1_ref) + branch(oc_ref, wc_ref, g2_ref)
    y = jnp.dot(merged.astype(BF16), wo_ref[...], preferred_element_type=F32)
    o_ref[...] = x_ref[...] + _rms(y, nw_ref[...])


def _merge(oa, ob, oc, proj, x, wa, wb, wc, wo, nw, tm):
    t, d = x.shape
    w = oa.shape[1]
    row = lambda width, j=0: pl.BlockSpec((tm, width), lambda i, j=j: (i, j))
    const = lambda shape: pl.BlockSpec(shape, lambda i: (0, 0))
    return pl.pallas_call(
        _merge_kernel,
        grid=(t // tm,),
        in_specs=[row(w), row(w), row(w), row(d, 0), row(d, 1), row(d, 2), row(d),
                  const((w, d)), const((w, d)), const((w, d)), const((d, d)), const((1, d))],
        out_specs=row(d),
        out_shape=jax.ShapeDtypeStruct((t, d), F32),
        compiler_params=_cparams(("parallel",)),
        name="merge_out",
    )(oa, ob, oc, proj, proj, proj, x, wa, wb, wc, wo, nw.reshape(1, d))


def _ffn_kernel(x_ref, npre_ref, wg_ref, wu_ref, wo_ref, npost_ref, o_ref, h_ref, acc_ref):
    j = pl.program_id(1)

    @pl.when(j == 0)
    def _():
        h_ref[...] = _rms(x_ref[...], npre_ref[...]).astype(BF16)
        acc_ref[...] = jnp.zeros(acc_ref.shape, F32)

    h = h_ref[...]
    gate = jnp.dot(h, wg_ref[...], preferred_element_type=F32)
    up = jnp.dot(h, wu_ref[...], preferred_element_type=F32)
    act = (gate * jax.nn.sigmoid(gate) * up).astype(BF16)
    acc_ref[...] += jnp.dot(act, wo_ref[...], preferred_element_type=F32)

    @pl.when(j == pl.num_programs(1) - 1)
    def _():
        o_ref[...] = x_ref[...] + _rms(acc_ref[...], npost_ref[...])


def _ffn(x, npre, wg, wu, wo, npost, tm, th):
    t, d = x.shape
    hid = wg.shape[1]
    return pl.pallas_call(
        _ffn_kernel,
        grid=(t // tm, hid // th),
        in_specs=[pl.BlockSpec((tm, d), lambda i, j: (i, 0)),
                  pl.BlockSpec((1, d), lambda i, j: (0, 0)),
                  pl.BlockSpec((d, th), lambda i, j: (0, j)),
                  pl.BlockSpec((d, th), lambda i, j: (0, j)),
                  pl.BlockSpec((th, d), lambda i, j: (j, 0)),
                  pl.BlockSpec((1, d), lambda i, j: (0, 0))],
        out_specs=pl.BlockSpec((tm, d), lambda i, j: (i, 0)),
        out_shape=jax.ShapeDtypeStruct((t, d), F32),
        scratch_shapes=[pltpu.VMEM((tm, d), BF16), pltpu.VMEM((tm, d), F32)],
        compiler_params=_cparams(("parallel", "arbitrary")),
        name="ffn",
    )(x, npre.reshape(1, d), wg, wu, wo, npost.reshape(1, d))


def _t5_bucket_table(dist):
    n = np.maximum(dist, 0)
    max_exact = NUM_BUCKETS // 2
    log_ratio = np.log(np.maximum(n, 1).astype(np.float32) / max_exact) / math.log(T5_MAX_DISTANCE / max_exact)
    large = max_exact + (log_ratio * (NUM_BUCKETS - max_exact)).astype(np.int32)
    return np.where(n < max_exact, n, np.minimum(large, NUM_BUCKETS - 1)).astype(np.int32)


def _bias_lookup(rel, dist):
    onehot = (jnp.asarray(_t5_bucket_table(dist))[..., None] == jnp.arange(NUM_BUCKETS)).astype(F32)
    return jnp.dot(onehot, (rel - rel[NUM_BUCKETS - 1]) * LOG2E, precision=lax.Precision.HIGHEST)


def _near_bias(rel, t):
    qi = np.arange(t)[:, None]
    ki = np.arange(t)[None, :]
    prev = _bias_lookup(rel, qi + t - ki)
    diag = jnp.where((qi >= ki)[..., None], _bias_lookup(rel, qi - ki), NEG_INF)
    flat = lambda b: b.transpose(2, 0, 1).reshape(-1, t)
    return flat(prev), flat(diag)


def _cmp_bias(rel, t):
    w = t // CMP_STRIDE
    qi = np.arange(t)[:, None]
    j = np.arange(LANES)[None, :]
    d = qi - CMP_STRIDE * (j - w) - (CMP_LEN - 1)
    b = jnp.where(((j < 2 * w) & (d >= 0))[..., None], _bias_lookup(rel, d),
                  jnp.asarray(np.where(j <= 2 * w, NEG_INF, 0.0) * np.ones_like(d), F32)[..., None])
    return b.transpose(2, 0, 1).reshape(-1, LANES).astype(BF16)


def _overlap(s, nc):
    n_cmp = (s - CMP_LEN) // CMP_STRIDE + 1
    cmp_end = np.arange(nc) * CMP_STRIDE + CMP_LEN - 1
    cmp_start = cmp_end - (CMP_LEN - 1)
    sb_start = np.arange(LANES) * SEL_BLOCK
    ov = (cmp_start[:, None] < sb_start[None, :] + SEL_BLOCK) & (cmp_end[:, None] >= sb_start[None, :])
    ov = ov & (np.arange(nc)[:, None] < n_cmp) & (np.arange(LANES)[None, :] < s // SEL_BLOCK)
    return jnp.asarray(ov, BF16)


def _pair_diag(w):
    z = jnp.zeros_like(w)
    return jnp.concatenate([jnp.concatenate([w, z], axis=-1), jnp.concatenate([z, w], axis=-1)], axis=-2)


def kernel(x, rel_bias, norm_mix_pre, norm_mix_post, norm_ffn_pre, norm_ffn_post, w_in, nsa_pe_k, nsa_pe_v, nsa_cmp_k_w1, nsa_cmp_k_w2, nsa_cmp_v_w1, nsa_cmp_v_w2, gla_gate_w2, gla_gate_b, gla_norm, w_branch_a, w_branch_b, w_branch_c, w_out, w_ffn_in, w_ffn_out):
    b, s, d = x.shape
    depth = w_in.shape[0]
    t = b * s
    dk = NSA_HEAD_DIM
    tq = ATT_TILE
    assert d == D_MODEL and w_in.shape[2] == D_IN
    assert WINDOW == 2 * tq and MOBA_BLOCK == tq and s % tq == 0
    assert SEL_TOPK <= s // SEL_BLOCK <= LANES and s // MOBA_BLOCK <= LANES
    nc = s // CMP_STRIDE
    assert nc % LANES == 0

    w_in_p = _layout_w_in(w_in)
    rel_a = rel_bias[:, :NSA_HEADS]
    rel_c = rel_bias[:, NSA_HEADS:]
    dprev_a, ddiag_a = _near_bias(rel_a, tq)
    dprev_c, ddiag_c = _near_bias(rel_c, tq)
    tcmp = _cmp_bias(rel_a, tq)
    overlap = _overlap(s, nc)
    wtab = jnp.asarray(np.where(np.arange(tq)[None, :] > np.arange(tq)[:, None], 0.0, NEG_INF), F32)
    avg = jnp.asarray((np.arange(LANES)[:, None] == np.arange(s)[None, :] // MOBA_BLOCK) / MOBA_BLOCK, BF16)

    def cmp_w1(w1):
        w = _pair_diag(w1.reshape(depth, 2, CMP_STRIDE, dk, dk))
        return w.reshape(depth, 2, CMP_STRIDE * 2 * dk, 2 * dk)

    def cmp_pe(pe):
        p2 = jnp.concatenate([pe, pe], axis=-1).reshape(depth, 2, 1, CMP_STRIDE * 2 * dk)
        return jnp.broadcast_to(p2, (depth, 2, 8, CMP_STRIDE * 2 * dk))

    cw1 = jnp.stack([cmp_w1(nsa_cmp_k_w1), cmp_w1(nsa_cmp_v_w1)], axis=1).astype(BF16)
    cw2 = jnp.stack([_pair_diag(nsa_cmp_k_w2), _pair_diag(nsa_cmp_v_w2)], axis=1).astype(BF16)
    cpe = jnp.stack([cmp_pe(nsa_pe_k), cmp_pe(nsa_pe_v)], axis=1).astype(BF16)

    gw2 = jnp.concatenate([gla_gate_w2, jnp.zeros((depth, LANES - GLA_GATE_RANK, gla_gate_w2.shape[2]), F32)],
                          axis=1).astype(BF16)
    wa = (w_branch_a.reshape(depth, NSA_KV_GROUPS, NSA_REP, dk, d).transpose(0, 2, 1, 3, 4)
          .reshape(depth, NSA_HEADS * dk, d).astype(BF16))
    wb = w_branch_b.astype(BF16)
    wc = w_branch_c.astype(BF16)
    wo = w_out.astype(BF16)
    hid = w_ffn_out.shape[1]
    wg = w_ffn_in[:, :, :hid].astype(BF16)
    wu = w_ffn_in[:, :, hid:].astype(BF16)
    wf = w_ffn_out.astype(BF16)

    tm = min(1024, t)
    tn = NP_COLS // 5
    xf = x.reshape(t, d)
    for layer in range(depth):
        proj = _norm_matmul(xf, norm_mix_pre[layer], w_in_p[layer], tm, tn).reshape(b, s, NP_COLS)
        kc0, vc0 = SEC["a_kc"][0], SEC["a_vc"][0]
        xkv = jnp.stack([proj[:, :, kc0:kc0 + LANES], proj[:, :, vc0:vc0 + LANES]])
        xkv = xkv.reshape(2, b, nc, CMP_STRIDE * LANES)
        cmp = _compress(xkv, cpe[layer], cw1[layer], cw2[layer])
        o_a = _nsa_attention(proj, cmp[0], cmp[1], overlap, tcmp, dprev_a, ddiag_a, wtab)
        o_b = _gla(proj, gw2[layer], gla_gate_b[layer].reshape(1, -1), gla_norm[layer].reshape(1, -1),
                   min(256, s))
        kmean = _moba_kmean(proj, avg)
        o_c = _moba_attention(proj, kmean, dprev_c, ddiag_c)
        xf = _merge(o_a.reshape(t, -1), o_b.reshape(t, -1), o_c.reshape(t, -1), proj.reshape(t, NP_COLS), xf,
                    wa[layer], wb[layer], wc[layer], wo[layer], norm_mix_post[layer], min(512, t))
        xf = _ffn(xf, norm_ffn_pre[layer], wg[layer], wu[layer], wf[layer], norm_ffn_post[layer],
                  min(512, t), hid // 2)
    return xf.reshape(b, s, d)
```

```python
import math

import numpy as np
import jax
import jax.numpy as jnp
from jax import lax
from jax.experimental import pallas as pl
from jax.experimental.pallas import tpu as pltpu

F32 = jnp.float32
BF16 = jnp.bfloat16

NORM_EPS = 1e-6
NEG_INF = -1e30
TOP_BONUS = 1e9
NUM_BUCKETS = 32
T5_MAX_DISTANCE = 128
NSA_HEADS = 8
NSA_KV_GROUPS = 2
NSA_REP = NSA_HEADS // NSA_KV_GROUPS
NSA_HEAD_DIM = 64
CMP_LEN = 32
CMP_STRIDE = 16
SEL_BLOCK = 64
SEL_TOPK = 16
WINDOW = 512
N_NSA_BRANCH = 3
GLA_HEADS = 4
GLA_KEY_DIM = 64
GLA_VAL_DIM = 128
GLA_GATE_RANK = 16
GLA_GATE_TAU = 16.0
GLA_CHUNK = 64
MOBA_HEADS = 8
MOBA_HEAD_DIM = 64
MOBA_BLOCK = 256
MOBA_TOPK = 3
N_BRANCHES = 3
LOG2E = math.log2(math.e)

LANES = 128
ATT_TILE = 256
FAR_TILE = 2 * ATT_TILE
VMEM_LIMIT = 56 * 1024 * 1024

D_MODEL = 1024
SEC = {}
_off = 0
for _name, _w in (("mg", 3 * D_MODEL), ("a_q", 512), ("c_q", 512), ("c_k", 512), ("c_v", 512),
                  ("b_v", 512), ("b_r", 512), ("b_q", 256), ("b_k", 256),
                  ("a_kc", 128), ("a_vc", 128), ("a_ks", 128), ("a_vs", 128), ("a_kw", 128), ("a_vw", 128),
                  ("a_g", 128), ("b_lr", 128)):
    SEC[_name] = (_off, _w)
    _off += _w
NP_COLS = _off

_SRC_NAMES = ("a_q", "a_kc", "a_vc", "a_ks", "a_vs", "a_kw", "a_vw", "a_g", "b_q", "b_k", "b_v", "b_r", "b_lr",
              "c_q", "c_k", "c_v", "mg")
_SRC_SIZES = (512, 128, 128, 128, 128, 128, 128, 24, 256, 256, 512, 512, 16, 512, 512, 512, 3 * D_MODEL)
_SRC_OFF = dict(zip(_SRC_NAMES, np.cumsum((0,) + _SRC_SIZES[:-1]).tolist()))
_SRC_W = dict(zip(_SRC_NAMES, _SRC_SIZES))
D_IN = int(sum(_SRC_SIZES))


def _layout_w_in(w_in):
    depth, d, _ = w_in.shape

    def src(name, lo=0, hi=None):
        hi = _SRC_W[name] if hi is None else hi
        return w_in[:, :, _SRC_OFF[name] + lo:_SRC_OFF[name] + hi]

    parts = []
    for name, (_, width) in SEC.items():
        if name == "a_q":
            for r in range(NSA_REP):
                for g in range(NSA_KV_GROUPS):
                    h = g * NSA_REP + r
                    parts.append(src(name, h * NSA_HEAD_DIM, (h + 1) * NSA_HEAD_DIM) * (NSA_HEAD_DIM ** -0.5 * LOG2E))
        elif name == "c_q":
            parts.append(src(name) * (MOBA_HEAD_DIM ** -0.5 * LOG2E))
        else:
            parts.append(src(name))
            if _SRC_W[name] < width:
                parts.append(jnp.zeros((depth, d, width - _SRC_W[name]), w_in.dtype))
    return jnp.concatenate(parts, axis=2).astype(BF16)


def _cparams(sem):
    return pltpu.CompilerParams(dimension_semantics=sem, vmem_limit_bytes=VMEM_LIMIT)


def _const_spec(shape, index_map):
    return pl.BlockSpec(shape, index_map, pipeline_mode=pl.Buffered(1))


def _rms(y, w):
    return y * lax.rsqrt(jnp.mean(y * y, axis=-1, keepdims=True) + NORM_EPS) * w


def _norm_matmul_kernel(x_ref, nw_ref, w_ref, o_ref, h_ref):
    @pl.when(pl.program_id(1) == 0)
    def _():
        h_ref[...] = _rms(x_ref[...], nw_ref[...]).astype(BF16)

    o_ref[...] = jnp.dot(h_ref[...], w_ref[...], preferred_element_type=F32).astype(o_ref.dtype)


def _norm_matmul(x, nw, w, tm, tn):
    t, d = x.shape
    n = w.shape[1]
    return pl.pallas_call(
        _norm_matmul_kernel,
        grid=(t // tm, n // tn),
        in_specs=[pl.BlockSpec((tm, d), lambda i, j: (i, 0)),
                  pl.BlockSpec((1, d), lambda i, j: (0, 0)),
                  pl.BlockSpec((d, tn), lambda i, j: (0, j))],
        out_specs=pl.BlockSpec((tm, tn), lambda i, j: (i, j)),
        out_shape=jax.ShapeDtypeStruct((t, n), BF16),
        scratch_shapes=[pltpu.VMEM((tm, d), BF16)],
        compiler_params=_cparams(("parallel", "arbitrary")),
        name="norm_proj",
    )(x, nw.reshape(1, d), w)


def _compress_kernel(x_ref, pe_ref, w1_ref, w2_ref, o_ref):
    x = x_ref[...]
    nc = x.shape[0]
    w1t = w1_ref[0]
    w1b = w1_ref[1]
    a = jnp.dot(x, w1t, preferred_element_type=F32)
    b = jnp.dot(x, w1b, preferred_element_type=F32)
    pe = pe_ref[...]
    pe_term = (jnp.dot(pe[0], w1t, preferred_element_type=F32)
               + jnp.dot(pe[1], w1b, preferred_element_type=F32))[0:1]
    pre = a + pltpu.roll(b, nc - 1, 0) + pe_term
    hid = jax.nn.gelu(pre)
    o_ref[...] = jnp.dot(hid.astype(BF16), w2_ref[...], preferred_element_type=F32).astype(o_ref.dtype)


def _compress(xkv, pe, w1, w2):
    _, b, nc, kw = xkv.shape
    return pl.pallas_call(
        _compress_kernel,
        grid=(2, b),
        in_specs=[pl.BlockSpec((None, None, nc, kw), lambda s, i: (s, i, 0, 0)),
                  pl.BlockSpec((None, 2, 8, kw), lambda s, i: (s, 0, 0, 0)),
                  pl.BlockSpec((None, 2, kw, LANES), lambda s, i: (s, 0, 0, 0)),
                  pl.BlockSpec((None, LANES, LANES), lambda s, i: (s, 0, 0))],
        out_specs=pl.BlockSpec((None, None, nc, LANES), lambda s, i: (s, i, 0, 0)),
        out_shape=jax.ShapeDtypeStruct((2, b, nc, LANES), BF16),
        compiler_params=_cparams(("parallel", "parallel")),
        name="nsa_compress",
    )(xkv, pe, w1, w2)


def _with_ones(v):
    return jnp.concatenate([v, jnp.ones(v.shape, v.dtype)], axis=1)


def _softmax_values(s, v1):
    p = jnp.exp2(s - jnp.max(s, axis=-1, keepdims=True)).astype(BF16)
    r = jnp.dot(p, v1, preferred_element_type=F32)
    return r[:, 0:LANES] / r[:, LANES:2 * LANES]


def _qk(lhs, rhs):
    return lax.dot_general(lhs, rhs, (((1,), (1,)), ((), ())), preferred_element_type=F32)


def _lane_onehot(width, lane_idx):
    klane = lax.broadcasted_iota(jnp.int32, (width, LANES), 1)
    return (klane == lane_idx).astype(BF16)


def _pipelined_attention(c, block, groups, lhs_ref, write_near_mask, k_ref, v_ref, dp_ref, dd_ref,
                         s_refs, p_refs, a_refs, m_ref, acc_ref):
    tq = ATT_TILE
    tk = FAR_TILE
    rows = lhs_ref.shape[0]
    grows = rows // groups
    j_max = k_ref.shape[0] // tk - 1
    per = tk // block
    n_far = (jnp.maximum(c - 1, 0) * tq + tk - 1) // tk
    n_loop = 2 * ((n_far + 1) // 2)
    krow = lax.broadcasted_iota(jnp.int32, (tk, LANES), 0) // block
    klane = lax.broadcasted_iota(jnp.int32, (tk, LANES), 1)

    m_ref[...] = jnp.full(m_ref.shape, NEG_INF, F32)
    acc_ref[...] = jnp.zeros(acc_ref.shape, F32)
    s_refs[1][...] = jnp.full((rows, tk), -jnp.inf, F32)
    p_refs[0][...] = jnp.zeros((rows, tk), BF16)
    a_refs[0][...] = jnp.ones((rows, LANES), F32)

    def far_rows(j):
        start = pl.multiple_of(jnp.clip(j, 0, j_max) * tk, tk)
        return lambda ref, g: ref[pl.ds(start, tk), g * LANES:(g + 1) * LANES]

    def near_rows(ref, g):
        cols = slice(g * LANES, (g + 1) * LANES)
        prev = ref[pl.ds(pl.multiple_of(jnp.maximum(c - 1, 0) * tq, tq), tq), cols]
        return jnp.concatenate([prev, ref[pl.ds(pl.multiple_of(c * tq, tq), tq), cols]], axis=0)

    def value_stage(slot, window):
        for g in range(groups):
            gr = slice(g * grows, (g + 1) * grows)
            acc_ref[gr, :] = (jnp.tile(a_refs[slot][gr, :], (1, 2)) * acc_ref[gr, :]
                              + jnp.dot(p_refs[slot][gr, :], _with_ones(window(v_ref, g)),
                                        preferred_element_type=F32))

    def softmax_stage(slot, table=None):
        s = s_refs[slot][...]
        if table is not None:
            s = s + table
        m_prev = m_ref[...]
        m_new = jnp.maximum(m_prev, jnp.max(s, axis=-1, keepdims=True))
        a_refs[slot][...] = jnp.exp2(m_prev - m_new)
        p_refs[slot][...] = jnp.exp2(s - jnp.tile(m_new, (1, tk // LANES))).astype(BF16)
        m_ref[...] = m_new

    def score_stage(slot, window, lanes):
        onehot = (klane == lanes).astype(BF16)
        for g in range(groups):
            gr = slice(g * grows, (g + 1) * grows)
            s_refs[slot][gr, :] = _qk(lhs_ref[gr, :], jnp.concatenate([window(k_ref, g), onehot], axis=1))

    def loop_stage(i, slot):
        value_stage(slot, far_rows(i - 2))
        softmax_stage(1 - slot)
        score_stage(slot, far_rows(i), jnp.where(i < n_far, krow + i * per, LANES - 1))

    def body(ii, carry):
        loop_stage(2 * ii, 0)
        loop_stage(2 * ii + 1, 1)
        return carry

    lax.fori_loop(0, n_loop // 2, body, 0)

    value_stage(0, far_rows(n_loop - 2))
    softmax_stage(1)
    write_near_mask()
    near_lanes = jnp.where((krow >= tq // block) | (c >= 1), krow + (c - 1) * (tq // block), LANES - 1)
    score_stage(0, near_rows, near_lanes)
    value_stage(1, far_rows(n_loop - 1))
    softmax_stage(0, jnp.concatenate([dp_ref[...], dd_ref[...]], axis=1))
    value_stage(0, near_rows)
    return acc_ref[:, 0:LANES] / acc_ref[:, LANES:2 * LANES]


def _topk_mask(vals, k):
    lane = lax.broadcasted_iota(jnp.int32, vals.shape, 1).astype(F32)
    sel = jnp.zeros(vals.shape, jnp.bool_)
    for _ in range(k):
        m = jnp.max(vals, axis=-1, keepdims=True)
        idx = jnp.min(jnp.where(vals == m, lane, float(LANES)), axis=-1, keepdims=True)
        hit = lane == idx
        sel = jnp.logical_or(sel, hit)
        vals = jnp.where(hit, -jnp.inf, vals)
    return sel


def _topk_mask_rows(vals, k):
    row = lax.broadcasted_iota(jnp.int32, vals.shape, 0).astype(F32)
    sel = jnp.zeros(vals.shape, jnp.bool_)
    for _ in range(k):
        m = jnp.max(vals, axis=0, keepdims=True)
        idx = jnp.min(jnp.where(vals == m, row, float(vals.shape[0])), axis=0, keepdims=True)
        hit = row == idx
        sel = jnp.logical_or(sel, hit)
        vals = jnp.where(hit, -jnp.inf, vals)
    return sel


def _nsa_kernel(q_ref, ks_ref, vs_ref, kw_ref, vw_ref, g_ref, kc_ref, vc_ref, ov_ref, tc_ref, dp_ref, dd_ref, wt_ref,
                o_ref, lhs_ref, nm_ref, m_ref, acc_ref, out_ref, s0_ref, s1_ref, p0_ref, p1_ref, a0_ref, a1_ref):
    tq = ATT_TILE
    nh = NSA_HEADS
    rows = nh * tq
    c = pl.program_id(1)
    t0 = c * tq
    ncp = kc_ref.shape[0]
    wcols = 2 * (tq // CMP_STRIDE)

    lane = lax.broadcasted_iota(jnp.int32, (tq, LANES), 1)
    rowi = lax.broadcasted_iota(jnp.int32, (tq, LANES), 0)
    half = lane // NSA_HEAD_DIM

    q = q_ref[...]
    for g in range(NSA_KV_GROUPS):
        for r in range(NSA_REP):
            h = g * NSA_REP + r
            qb = q[:, r * LANES:(r + 1) * LANES]
            lhs_ref[h * tq:(h + 1) * tq, 0:LANES] = jnp.where(half == g, qb, jnp.zeros_like(qb))
    qrows = lhs_ref[:, 0:LANES]

    gates = jax.nn.sigmoid(g_ref[...].astype(F32))

    def gate_col(h, br):
        col = h * N_NSA_BRANCH + br
        return gates[:, col:col + 1]

    prev_start = pl.multiple_of(jnp.maximum(c - 1, 0) * tq, tq)
    diag_start = pl.multiple_of(c * tq, tq)

    ii = lax.broadcasted_iota(jnp.int32, (ncp, LANES), 0) - (c * (tq // CMP_STRIDE) - tq // CMP_STRIDE)
    ww = lax.broadcasted_iota(jnp.int32, (ncp, LANES), 1)
    place = (((ww < wcols) & (ii == ww)) | ((ww == wcols) & (ii >= wcols))).astype(BF16)
    s = _qk(jnp.concatenate([qrows, tc_ref[...]], axis=1), jnp.concatenate([kc_ref[...], place], axis=1))
    m = jnp.max(s, axis=-1, keepdims=True)
    p = jnp.exp2(s - m)
    p_hi = p.astype(BF16)
    p_lo = (p - p_hi.astype(F32)).astype(BF16)
    ov = ov_ref[...]
    r1 = jnp.dot(p_hi, jnp.concatenate([_with_ones(vc_ref[...]), ov], axis=1), preferred_element_type=F32)
    r2 = jnp.dot(p_lo, ov, preferred_element_type=F32)
    has_key = m > 0.5 * NEG_INF
    inv_l = jnp.where(has_key, 1.0 / r1[:, LANES:2 * LANES], 0.0)
    o_cmp = r1[:, 0:LANES] * inv_l
    imp_h = (r1[:, 2 * LANES:3 * LANES] + r2) * inv_l
    for h in range(nh):
        sl = slice(h * tq, (h + 1) * tq)
        out_ref[sl, :] = gate_col(h, 0) * o_cmp[sl]

    cur = (rowi + t0) // SEL_BLOCK
    forced = (lane == 0) | (lane == cur) | (lane == cur - 1)
    far_blocks = (c - 1) * (tq // SEL_BLOCK)
    for g in range(NSA_KV_GROUPS):
        imp = imp_h[g * NSA_REP * tq:(g * NSA_REP + 1) * tq]
        for r in range(1, NSA_REP):
            imp = imp + imp_h[(g * NSA_REP + r) * tq:(g * NSA_REP + r + 1) * tq]
        imp = jnp.where(forced, TOP_BONUS, jnp.where(lane <= cur, imp, NEG_INF))
        sel = _topk_mask(imp, SEL_TOPK)
        nm_ref[g] = jnp.where(sel, 0.0, NEG_INF).astype(BF16)
        far_mask = jnp.where(sel & (lane < far_blocks), 0.0, NEG_INF).astype(BF16)
        for r in range(NSA_REP):
            h = g * NSA_REP + r
            lhs_ref[h * tq:(h + 1) * tq, LANES:2 * LANES] = far_mask

    wpat = jnp.where(lax.broadcasted_iota(jnp.int32, (rows, LANES), 1) == 1, NEG_INF, 0.0).astype(BF16)
    back2_start = pl.multiple_of(jnp.maximum(c - 2, 0) * tq, tq)
    kwin = jnp.concatenate([kw_ref[pl.ds(back2_start, tq), :], kw_ref[pl.ds(prev_start, tq), :],
                            kw_ref[pl.ds(diag_start, tq), :]], axis=0)
    vwin = jnp.concatenate([vw_ref[pl.ds(back2_start, tq), :], vw_ref[pl.ds(prev_start, tq), :],
                            vw_ref[pl.ds(diag_start, tq), :]], axis=0)
    hot = jnp.concatenate([_lane_onehot(tq, jnp.where(c >= 2, 0, 1)), _lane_onehot(tq, jnp.where(c >= 1, 0, 1)),
                           _lane_onehot(tq, 0)], axis=0)
    s = _qk(jnp.concatenate([qrows, wpat], axis=1), jnp.concatenate([kwin, hot], axis=1))
    s = s + jnp.concatenate([jnp.tile(wt_ref[...], (nh, 1)), dp_ref[...], dd_ref[...]], axis=1)
    o_win = _softmax_values(s, _with_ones(vwin))
    for h in range(nh):
        sl = slice(h * tq, (h + 1) * tq)
        out_ref[sl, :] = out_ref[sl, :] + gate_col(h, 2) * o_win[sl]

    def write_near_mask():
        for h in range(nh):
            lhs_ref[h * tq:(h + 1) * tq, LANES:2 * LANES] = nm_ref[h // NSA_REP]

    o_sel = _pipelined_attention(c, SEL_BLOCK, 1, lhs_ref, write_near_mask, ks_ref, vs_ref, dp_ref, dd_ref,
                                 (s0_ref, s1_ref), (p0_ref, p1_ref), (a0_ref, a1_ref), m_ref, acc_ref)

    for r in range(NSA_REP):
        h0, h1 = r, NSA_REP + r
        o0 = out_ref[h0 * tq:(h0 + 1) * tq, :] + gate_col(h0, 1) * o_sel[h0 * tq:(h0 + 1) * tq]
        o1 = out_ref[h1 * tq:(h1 + 1) * tq, :] + gate_col(h1, 1) * o_sel[h1 * tq:(h1 + 1) * tq]
        o_ref[:, r * LANES:(r + 1) * LANES] = jnp.where(half == 0, o0, o1).astype(o_ref.dtype)


def _nsa_attention(proj, kcmp, vcmp, overlap, tcmp, dprev, ddiag, wtab):
    b, s, _ = proj.shape
    tq = ATT_TILE
    ncp = kcmp.shape[1]
    rows = NSA_HEADS * tq

    def col(name, width):
        return SEC[name][0] // width

    full = lambda name: _const_spec((None, s, LANES), lambda i, c, n=name: (i, 0, col(n, LANES)))
    return pl.pallas_call(
        _nsa_kernel,
        grid=(b, s // tq),
        in_specs=[pl.BlockSpec((None, tq, 512), lambda i, c: (i, c, col("a_q", 512))),
                  full("a_ks"), full("a_vs"), full("a_kw"), full("a_vw"),
                  pl.BlockSpec((None, tq, LANES), lambda i, c: (i, c, col("a_g", LANES))),
                  _const_spec((None, ncp, LANES), lambda i, c: (i, 0, 0)),
                  _const_spec((None, ncp, LANES), lambda i, c: (i, 0, 0)),
                  _const_spec((ncp, LANES), lambda i, c: (0, 0)),
                  _const_spec((rows, LANES), lambda i, c: (0, 0)),
                  _const_spec((rows, tq), lambda i, c: (0, 0)),
                  _const_spec((rows, tq), lambda i, c: (0, 0)),
                  _const_spec((tq, tq), lambda i, c: (0, 0))],
        out_specs=pl.BlockSpec((None, tq, 512), lambda i, c: (i, c, 0)),
        out_shape=jax.ShapeDtypeStruct((b, s, 512), BF16),
        scratch_shapes=[pltpu.VMEM((rows, 2 * LANES), BF16),
                        pltpu.VMEM((NSA_KV_GROUPS, tq, LANES), BF16),
                        pltpu.VMEM((rows, LANES), F32), pltpu.VMEM((rows, 2 * LANES), F32),
                        pltpu.VMEM((rows, LANES), F32),
                        pltpu.VMEM((rows, FAR_TILE), F32), pltpu.VMEM((rows, FAR_TILE), F32),
                        pltpu.VMEM((rows, FAR_TILE), BF16), pltpu.VMEM((rows, FAR_TILE), BF16),
                        pltpu.VMEM((rows, LANES), F32), pltpu.VMEM((rows, LANES), F32)],
        compiler_params=_cparams(("parallel", "arbitrary")),
        name="nsa_attention",
    )(proj, proj, proj, proj, proj, proj, kcmp, vcmp, overlap, tcmp, dprev, ddiag, wtab)


def _gla_kernel(q_ref, k_ref, v_ref, r_ref, lr_ref, w2_ref, gb_ref, gn_ref, o_ref, st_ref):
    ch = GLA_CHUNK
    lb = q_ref.shape[0]
    nch = lb // ch

    @pl.when(pl.program_id(1) == 0)
    def _():
        st_ref[...] = jnp.zeros(st_ref.shape, F32)

    x = jnp.dot(lr_ref[...], w2_ref[...], preferred_element_type=F32) + gb_ref[...]
    log_a = (jnp.minimum(x, 0.0) - jnp.log1p(jnp.exp(-jnp.abs(x)))) / GLA_GATE_TAU

    ti = lax.broadcasted_iota(jnp.int32, (lb, lb), 0)
    tj = lax.broadcasted_iota(jnp.int32, (lb, lb), 1)
    same_chunk = (ti // ch) == (tj // ch)
    causal = same_chunk & (tj <= ti)
    g_hi, g_mid, g_lo = _split3(log_a)

    def chunk_sums(w):
        return (jnp.dot(w, g_hi, preferred_element_type=F32) + jnp.dot(w, g_mid, preferred_element_type=F32)
                + jnp.dot(w, g_lo, preferred_element_type=F32))

    bcum = chunk_sums(causal.astype(BF16))
    btot = chunk_sums(same_chunk.astype(BF16))
    half = lax.broadcasted_iota(jnp.int32, (lb, LANES), 1) // GLA_KEY_DIM
    gn = gn_ref[...]

    for p in range(GLA_HEADS // 2):
        cols = slice(p * LANES, (p + 1) * LANES)
        bc = bcum[:, cols]
        bt = btot[:, cols]
        qf = q_ref[:, cols].astype(F32) * (GLA_KEY_DIM ** -0.5)
        kf = k_ref[:, cols].astype(F32)
        qe = qf * jnp.exp(bc)
        kinv = (kf * jnp.exp(-bc)).astype(BF16)
        klast = (kf * jnp.exp(bt - bc)).astype(BF16)
        decay = jnp.exp(bt)
        for a in range(2):
            h = 2 * p + a
            hc = slice(h * LANES, (h + 1) * LANES)
            qa = jnp.where(half == a, qe, 0.0).astype(BF16)
            v = v_ref[:, hc]
            attn = jnp.where(causal, _qk(qa, kinv), 0.0).astype(BF16)
            o_intra = jnp.dot(attn, v, preferred_element_type=F32)
            outs = []
            st = st_ref[h]
            for cc in range(nch):
                sl = slice(cc * ch, (cc + 1) * ch)
                outs.append(_qk(qa[sl], st.astype(BF16)))
                upd = lax.dot_general(v[sl], klast[sl], (((0,), (0,)), ((), ())), preferred_element_type=F32)
                st = st * decay[cc * ch:cc * ch + 1, :] + upd
            st_ref[h] = st
            o = o_intra + jnp.concatenate(outs, axis=0)
            rg = r_ref[:, hc].astype(F32)
            o_ref[:, hc] = (_rms(o, gn) * (rg * jax.nn.sigmoid(rg))).astype(o_ref.dtype)


def _split3(x):
    hi = x.astype(BF16)
    r1 = x - hi.astype(F32)
    mid = r1.astype(BF16)
    lo = (r1 - mid.astype(F32)).astype(BF16)
    return hi, mid, lo


def _gla(proj, w2, gb, gn, lb):
    b, s, _ = proj.shape

    def spec(name, width):
        return pl.BlockSpec((None, lb, width), lambda i, c, n=name, w=width: (i, c, SEC[n][0] // w))

    return pl.pallas_call(
        _gla_kernel,
        grid=(b, s // lb),
        in_specs=[spec("b_q", 256), spec("b_k", 256), spec("b_v", 512), spec("b_r", 512), spec("b_lr", LANES),
                  pl.BlockSpec((LANES, 256), lambda i, c: (0, 0)),
                  pl.BlockSpec((1, 256), lambda i, c: (0, 0)),
                  pl.BlockSpec((1, LANES), lambda i, c: (0, 0))],
        out_specs=pl.BlockSpec((None, lb, 512), lambda i, c: (i, c, 0)),
        out_shape=jax.ShapeDtypeStruct((b, s, 512), BF16),
        scratch_shapes=[pltpu.VMEM((GLA_HEADS, LANES, LANES), F32)],
        compiler_params=_cparams(("parallel", "arbitrary")),
        name="gla",
    )(proj, proj, proj, proj, proj, w2, gb, gn)


def _kmean_kernel(a_ref, k_ref, o_ref):
    o_ref[...] = jnp.dot(a_ref[...], k_ref[...], preferred_element_type=F32).astype(o_ref.dtype)


def _moba_kmean(proj, avg):
    b, s, _ = proj.shape
    return pl.pallas_call(
        _kmean_kernel,
        grid=(b,),
        in_specs=[pl.BlockSpec((LANES, s), lambda i: (0, 0)),
                  pl.BlockSpec((None, s, 512), lambda i: (i, 0, SEC["c_k"][0] // 512))],
        out_specs=pl.BlockSpec((None, LANES, 512), lambda i: (i, 0, 0)),
        out_shape=jax.ShapeDtypeStruct((b, LANES, 512), BF16),
        compiler_params=_cparams(("parallel",)),
        name="moba_kmean",
    )(avg, proj)


def _moba_kernel(q_ref, k_ref, v_ref, km_ref, dp_ref, dd_ref, o_ref, lhs_ref, m_ref, acc_ref,
                 s0_ref, s1_ref, p0_ref, p1_ref, a0_ref, a1_ref):
    tq = ATT_TILE
    nh = MOBA_HEADS
    npair = nh // 2
    rows = nh * tq
    c = pl.program_id(1)

    lane = lax.broadcasted_iota(jnp.int32, (tq, LANES), 1)
    half = lane // MOBA_HEAD_DIM
    for h in range(nh):
        qb = q_ref[:, (h // 2) * LANES:(h // 2 + 1) * LANES]
        lhs_ref[h * tq:(h + 1) * tq, 0:LANES] = jnp.where(half == h % 2, qb, jnp.zeros_like(qb))

    nblk = k_ref.shape[0] // MOBA_BLOCK
    score = jnp.concatenate(
        [_qk(km_ref[0:nblk, p * LANES:(p + 1) * LANES], lhs_ref[2 * p * tq:2 * (p + 1) * tq, 0:LANES])
         for p in range(npair)], axis=1)
    blk = lax.broadcasted_iota(jnp.int32, (nblk, rows), 0)
    chosen = _topk_mask_rows(jnp.where(blk < c, score, NEG_INF), MOBA_TOPK) & (blk < c)
    chosen = jnp.concatenate([jnp.where(chosen, 1.0, 0.0), jnp.zeros((LANES - nblk, rows), F32)], axis=0)
    past = jnp.transpose(chosen) > 0.5
    lane2 = lax.broadcasted_iota(jnp.int32, (rows, LANES), 1)
    lhs_ref[:, LANES:2 * LANES] = jnp.where(past & (lane2 < c - 1), 0.0, NEG_INF).astype(BF16)

    def write_near_mask():
        lhs_ref[:, LANES:2 * LANES] = jnp.where(past | (lane2 == c), 0.0, NEG_INF).astype(BF16)

    o = _pipelined_attention(c, MOBA_BLOCK, npair, lhs_ref, write_near_mask, k_ref, v_ref, dp_ref, dd_ref,
                             (s0_ref, s1_ref), (p0_ref, p1_ref), (a0_ref, a1_ref), m_ref, acc_ref)
    for p in range(npair):
        o0 = o[2 * p * tq:(2 * p + 1) * tq]
        o1 = o[(2 * p + 1) * tq:(2 * p + 2) * tq]
        o_ref[:, p * LANES:(p + 1) * LANES] = jnp.where(half == 0, o0, o1).astype(o_ref.dtype)


def _moba_attention(proj, kmean, dprev, ddiag):
    b, s, _ = proj.shape
    tq = ATT_TILE
    rows = MOBA_HEADS * tq

    def col(name):
        return SEC[name][0] // 512

    return pl.pallas_call(
        _moba_kernel,
        grid=(b, s // tq),
        in_specs=[pl.BlockSpec((None, tq, 512), lambda i, c: (i, c, col("c_q"))),
                  _const_spec((None, s, 512), lambda i, c: (i, 0, col("c_k"))),
                  _const_spec((None, s, 512), lambda i, c: (i, 0, col("c_v"))),
                  _const_spec((None, LANES, 512), lambda i, c: (i, 0, 0)),
                  _const_spec((rows, tq), lambda i, c: (0, 0)),
                  _const_spec((rows, tq), lambda i, c: (0, 0))],
        out_specs=pl.BlockSpec((None, tq, 512), lambda i, c: (i, c, 0)),
        out_shape=jax.ShapeDtypeStruct((b, s, 512), BF16),
        scratch_shapes=[pltpu.VMEM((rows, 2 * LANES), BF16),
                        pltpu.VMEM((rows, LANES), F32), pltpu.VMEM((rows, 2 * LANES), F32),
                        pltpu.VMEM((rows, FAR_TILE), F32), pltpu.VMEM((rows, FAR_TILE), F32),
                        pltpu.VMEM((rows, FAR_TILE), BF16), pltpu.VMEM((rows, FAR_TILE), BF16),
                        pltpu.VMEM((rows, LANES), F32), pltpu.VMEM((rows, LANES), F32)],
        compiler_params=_cparams(("parallel", "arbitrary")),
        name="moba_attention",
    )(proj, proj, proj, kmean, dprev, ddiag)


def _merge_kernel(oa_ref, ob_ref, oc_ref, g0_ref, g1_ref, g2_ref, x_ref, wa_ref, wb_ref, wc_ref, wo_ref, nw_ref,
                  o_ref):
    def branch(o, w, g):
        return jax.nn.sigmoid(g[...].astype(F32)) * jnp.dot(o[...], w[...], preferred_element_type=F32)

    merged = branch(oa_ref, wa_ref, g0_ref) + branch(ob_ref, wb_ref, g1_ref) + branch(oc_ref, wc_ref, g2_ref)
    y = jnp.dot(merged.astype(BF16), wo_ref[...], preferred_element_type=F32)
    o_ref[...] = x_ref[...] + _rms(y, nw_ref[...])


def _merge(oa, ob, oc, proj, x, wa, wb, wc, wo, nw, tm):
    t, d = x.shape
    w = oa.shape[1]
    row = lambda width, j=0: pl.BlockSpec((tm, width), lambda i, j=j: (i, j))
    const = lambda shape: pl.BlockSpec(shape, lambda i: (0, 0))
    return pl.pallas_call(
        _merge_kernel,
        grid=(t // tm,),
        in_specs=[row(w), row(w), row(w), row(d, 0), row(d, 1), row(d, 2), row(d),
                  const((w, d)), const((w, d)), const((w, d)), const((d, d)), const((1, d))],
        out_specs=row(d),
        out_shape=jax.ShapeDtypeStruct((t, d), F32),
        compiler_params=_cparams(("parallel",)),
        name="merge_out",
    )(oa, ob, oc, proj, proj, proj, x, wa, wb, wc, wo, nw.reshape(1, d))


def _ffn_kernel(x_ref, npre_ref, wg_ref, wu_ref, wo_ref, npost_ref, o_ref, h_ref, acc_ref):
    j = pl.program_id(1)

    @pl.when(j == 0)
    def _():
        h_ref[...] = _rms(x_ref[...], npre_ref[...]).astype(BF16)
        acc_ref[...] = jnp.zeros(acc_ref.shape, F32)

    h = h_ref[...]
    gate = jnp.dot(h, wg_ref[...], preferred_element_type=F32)
    up = jnp.dot(h, wu_ref[...], preferred_element_type=F32)
    act = (gate * jax.nn.sigmoid(gate) * up).astype(BF16)
    acc_ref[...] += jnp.dot(act, wo_ref[...], preferred_element_type=F32)

    @pl.when(j == pl.num_programs(1) - 1)
    def _():
        o_ref[...] = x_ref[...] + _rms(acc_ref[...], npost_ref[...])


def _ffn(x, npre, wg, wu, wo, npost, tm, th):
    t, d = x.shape
    hid = wg.shape[1]
    return pl.pallas_call(
        _ffn_kernel,
        grid=(t // tm, hid // th),
        in_specs=[pl.BlockSpec((tm, d), lambda i, j: (i, 0)),
                  pl.BlockSpec((1, d), lambda i, j: (0, 0)),
                  pl.BlockSpec((d, th), lambda i, j: (0, j)),
                  pl.BlockSpec((d, th), lambda i, j: (0, j)),
                  pl.BlockSpec((th, d), lambda i, j: (j, 0)),
                  pl.BlockSpec((1, d), lambda i, j: (0, 0))],
        out_specs=pl.BlockSpec((tm, d), lambda i, j: (i, 0)),
        out_shape=jax.ShapeDtypeStruct((t, d), F32),
        scratch_shapes=[pltpu.VMEM((tm, d), BF16), pltpu.VMEM((tm, d), F32)],
        compiler_params=_cparams(("parallel", "arbitrary")),
        name="ffn",
    )(x, npre.reshape(1, d), wg, wu, wo, npost.reshape(1, d))


def _t5_bucket_table(dist):
    n = np.maximum(dist, 0)
    max_exact = NUM_BUCKETS // 2
    log_ratio = np.log(np.maximum(n, 1).astype(np.float32) / max_exact) / math.log(T5_MAX_DISTANCE / max_exact)
    large = max_exact + (log_ratio * (NUM_BUCKETS - max_exact)).astype(np.int32)
    return np.where(n < max_exact, n, np.minimum(large, NUM_BUCKETS - 1)).astype(np.int32)


def _bias_lookup(rel, dist):
    onehot = (jnp.asarray(_t5_bucket_table(dist))[..., None] == jnp.arange(NUM_BUCKETS)).astype(F32)
    return jnp.dot(onehot, (rel - rel[NUM_BUCKETS - 1]) * LOG2E, precision=lax.Precision.HIGHEST)


def _near_bias(rel, t):
    qi = np.arange(t)[:, None]
    ki = np.arange(t)[None, :]
    prev = _bias_lookup(rel, qi + t - ki)
    diag = jnp.where((qi >= ki)[..., None], _bias_lookup(rel, qi - ki), NEG_INF)
    flat = lambda b: b.transpose(2, 0, 1).reshape(-1, t)
    return flat(prev), flat(diag)


def _cmp_bias(rel, t):
    w = t // CMP_STRIDE
    qi = np.arange(t)[:, None]
    j = np.arange(LANES)[None, :]
    d = qi - CMP_STRIDE * (j - w) - (CMP_LEN - 1)
    b = jnp.where(((j < 2 * w) & (d >= 0))[..., None], _bias_lookup(rel, d),
                  jnp.asarray(np.where(j <= 2 * w, NEG_INF, 0.0) * np.ones_like(d), F32)[..., None])
    return b.transpose(2, 0, 1).reshape(-1, LANES).astype(BF16)


def _overlap(s, nc):
    n_cmp = (s - CMP_LEN) // CMP_STRIDE + 1
    cmp_end = np.arange(nc) * CMP_STRIDE + CMP_LEN - 1
    cmp_start = cmp_end - (CMP_LEN - 1)
    sb_start = np.arange(LANES) * SEL_BLOCK
    ov = (cmp_start[:, None] < sb_start[None, :] + SEL_BLOCK) & (cmp_end[:, None] >= sb_start[None, :])
    ov = ov & (np.arange(nc)[:, None] < n_cmp) & (np.arange(LANES)[None, :] < s // SEL_BLOCK)
    return jnp.asarray(ov, BF16)


def _pair_diag(w):
    z = jnp.zeros_like(w)
    return jnp.concatenate([jnp.concatenate([w, z], axis=-1), jnp.concatenate([z, w], axis=-1)], axis=-2)


def kernel(x, rel_bias, norm_mix_pre, norm_mix_post, norm_ffn_pre, norm_ffn_post, w_in, nsa_pe_k, nsa_pe_v, nsa_cmp_k_w1, nsa_cmp_k_w2, nsa_cmp_v_w1, nsa_cmp_v_w2, gla_gate_w2, gla_gate_b, gla_norm, w_branch_a, w_branch_b, w_branch_c, w_out, w_ffn_in, w_ffn_out):
    b, s, d = x.shape
    depth = w_in.shape[0]
    t = b * s
    dk = NSA_HEAD_DIM
    tq = ATT_TILE
    assert d == D_MODEL and w_in.shape[2] == D_IN
    assert WINDOW == 2 * tq and MOBA_BLOCK == tq and s % tq == 0
    assert SEL_TOPK <= s // SEL_BLOCK <= LANES and s // MOBA_BLOCK <= LANES
    nc = s // CMP_STRIDE
    assert nc % LANES == 0

    w_in_p = _layout_w_in(w_in)
    rel_a = rel_bias[:, :NSA_HEADS]
    rel_c = rel_bias[:, NSA_HEADS:]
    dprev_a, ddiag_a = _near_bias(rel_a, tq)
    dprev_c, ddiag_c = _near_bias(rel_c, tq)
    tcmp = _cmp_bias(rel_a, tq)
    overlap = _overlap(s, nc)
    wtab = jnp.asarray(np.where(np.arange(tq)[None, :] > np.arange(tq)[:, None], 0.0, NEG_INF), F32)
    avg = jnp.asarray((np.arange(LANES)[:, None] == np.arange(s)[None, :] // MOBA_BLOCK) / MOBA_BLOCK, BF16)

    def cmp_w1(w1):
        w = _pair_diag(w1.reshape(depth, 2, CMP_STRIDE, dk, dk))
        return w.reshape(depth, 2, CMP_STRIDE * 2 * dk, 2 * dk)

    def cmp_pe(pe):
        p2 = jnp.concatenate([pe, pe], axis=-1).reshape(depth, 2, 1, CMP_STRIDE * 2 * dk)
        return jnp.broadcast_to(p2, (depth, 2, 8, CMP_STRIDE * 2 * dk))

    cw1 = jnp.stack([cmp_w1(nsa_cmp_k_w1), cmp_w1(nsa_cmp_v_w1)], axis=1).astype(BF16)
    cw2 = jnp.stack([_pair_diag(nsa_cmp_k_w2), _pair_diag(nsa_cmp_v_w2)], axis=1).astype(BF16)
    cpe = jnp.stack([cmp_pe(nsa_pe_k), cmp_pe(nsa_pe_v)], axis=1).astype(BF16)

    gw2 = jnp.concatenate([gla_gate_w2, jnp.zeros((depth, LANES - GLA_GATE_RANK, gla_gate_w2.shape[2]), F32)],
                          axis=1).astype(BF16)
    wa = (w_branch_a.reshape(depth, NSA_KV_GROUPS, NSA_REP, dk, d).transpose(0, 2, 1, 3, 4)
          .reshape(depth, NSA_HEADS * dk, d).astype(BF16))
    wb = w_branch_b.astype(BF16)
    wc = w_branch_c.astype(BF16)
    wo = w_out.astype(BF16)
    hid = w_ffn_out.shape[1]
    wg = w_ffn_in[:, :, :hid].astype(BF16)
    wu = w_ffn_in[:, :, hid:].astype(BF16)
    wf = w_ffn_out.astype(BF16)

    tm = min(1024, t)
    tn = NP_COLS // 5
    xf = x.reshape(t, d)
    for layer in range(depth):
        proj = _norm_matmul(xf, norm_mix_pre[layer], w_in_p[layer], tm, tn).reshape(b, s, NP_COLS)
        kc0, vc0 = SEC["a_kc"][0], SEC["a_vc"][0]
        xkv = jnp.stack([proj[:, :, kc0:kc0 + LANES], proj[:, :, vc0:vc0 + LANES]])
        xkv = xkv.reshape(2, b, nc, CMP_STRIDE * LANES)
        cmp = _compress(xkv, cpe[layer], cw1[layer], cw2[layer])
        o_a = _nsa_attention(proj, cmp[0], cmp[1], overlap, tcmp, dprev_a, ddiag_a, wtab)
        o_b = _gla(proj, gw2[layer], gla_gate_b[layer].reshape(1, -1), gla_norm[layer].reshape(1, -1),
                   min(256, s))
        kmean = _moba_kmean(proj, avg)
        o_c = _moba_attention(proj, kmean, dprev_c, ddiag_c)
        xf = _merge(o_a.reshape(t, -1), o_b.reshape(t, -1), o_c.reshape(t, -1), proj.reshape(t, NP_COLS), xf,
                    wa[layer], wb[layer], wc[layer], wo[layer], norm_mix_post[layer], min(512, t))
        xf = _ffn(xf, norm_ffn_pre[layer], wg[layer], wu[layer], wf[layer], norm_ffn_post[layer],
                  min(512, t), hid // 2)
    return xf.reshape(b, s, d)
```

```python
import math

import numpy as np
import jax
import jax.numpy as jnp
from jax import lax
from jax.experimental import pallas as pl
from jax.experimental.pallas import tpu as pltpu

F32 = jnp.float32
BF16 = jnp.bfloat16

NORM_EPS = 1e-6
NEG_INF = -1e30
TOP_BONUS = 1e9
NUM_BUCKETS = 32
T5_MAX_DISTANCE = 128
NSA_HEADS = 8
NSA_KV_GROUPS = 2
NSA_REP = NSA_HEADS // NSA_KV_GROUPS
NSA_HEAD_DIM = 64
CMP_LEN = 32
CMP_STRIDE = 16
SEL_BLOCK = 64
SEL_TOPK = 16
WINDOW = 512
N_NSA_BRANCH = 3
GLA_HEADS = 4
GLA_KEY_DIM = 64
GLA_VAL_DIM = 128
GLA_GATE_RANK = 16
GLA_GATE_TAU = 16.0
GLA_CHUNK = 64
GLA_FACTORISED_RANGE = 80.0
MOBA_HEADS = 8
MOBA_HEAD_DIM = 64
MOBA_BLOCK = 256
MOBA_TOPK = 3
N_BRANCHES = 3
LOG2E = math.log2(math.e)

LANES = 128
ATT_TILE = 256
FAR_TILE = 2 * ATT_TILE
VMEM_LIMIT = 56 * 1024 * 1024

D_MODEL = 1024
SEC = {}
_off = 0
for _name, _w in (("mg", 3 * D_MODEL), ("a_q", 512), ("c_q", 512), ("c_k", 512), ("c_v", 512),
                  ("b_v", 512), ("b_r", 512), ("b_q", 256), ("b_k", 256),
                  ("a_kc", 128), ("a_vc", 128), ("a_ks", 128), ("a_vs", 128), ("a_kw", 128), ("a_vw", 128),
                  ("a_g", 128), ("b_lr", 128)):
    SEC[_name] = (_off, _w)
    _off += _w
NP_COLS = _off

_SRC_NAMES = ("a_q", "a_kc", "a_vc", "a_ks", "a_vs", "a_kw", "a_vw", "a_g", "b_q", "b_k", "b_v", "b_r", "b_lr",
              "c_q", "c_k", "c_v", "mg")
_SRC_SIZES = (512, 128, 128, 128, 128, 128, 128, 24, 256, 256, 512, 512, 16, 512, 512, 512, 3 * D_MODEL)
_SRC_OFF = dict(zip(_SRC_NAMES, np.cumsum((0,) + _SRC_SIZES[:-1]).tolist()))
_SRC_W = dict(zip(_SRC_NAMES, _SRC_SIZES))
D_IN = int(sum(_SRC_SIZES))


def _layout_w_in(w_in):
    depth, d, _ = w_in.shape

    def src(name, lo=0, hi=None):
        hi = _SRC_W[name] if hi is None else hi
        return w_in[:, :, _SRC_OFF[name] + lo:_SRC_OFF[name] + hi]

    parts = []
    for name, (_, width) in SEC.items():
        if name == "a_q":
            for r in range(NSA_REP):
                for g in range(NSA_KV_GROUPS):
                    h = g * NSA_REP + r
                    parts.append(src(name, h * NSA_HEAD_DIM, (h + 1) * NSA_HEAD_DIM) * (NSA_HEAD_DIM ** -0.5 * LOG2E))
        elif name == "c_q":
            parts.append(src(name) * (MOBA_HEAD_DIM ** -0.5 * LOG2E))
        else:
            parts.append(src(name))
            if _SRC_W[name] < width:
                parts.append(jnp.zeros((depth, d, width - _SRC_W[name]), w_in.dtype))
    return jnp.concatenate(parts, axis=2).astype(BF16)


def _cparams(sem):
    return pltpu.CompilerParams(dimension_semantics=sem, vmem_limit_bytes=VMEM_LIMIT)


def _const_spec(shape, index_map):
    return pl.BlockSpec(shape, index_map, pipeline_mode=pl.Buffered(1))


def _rms(y, w):
    return y * lax.rsqrt(jnp.mean(y * y, axis=-1, keepdims=True) + NORM_EPS) * w


def _norm_matmul_kernel(x_ref, nw_ref, w_ref, o_ref, h_ref):
    @pl.when(pl.program_id(1) == 0)
    def _():
        h_ref[...] = _rms(x_ref[...], nw_ref[...]).astype(BF16)

    o_ref[...] = jnp.dot(h_ref[...], w_ref[...], preferred_element_type=F32).astype(o_ref.dtype)


def _norm_matmul(x, nw, w, tm, tn):
    t, d = x.shape
    n = w.shape[1]
    return pl.pallas_call(
        _norm_matmul_kernel,
        grid=(t // tm, n // tn),
        in_specs=[pl.BlockSpec((tm, d), lambda i, j: (i, 0)),
                  pl.BlockSpec((1, d), lambda i, j: (0, 0)),
                  pl.BlockSpec((d, tn), lambda i, j: (0, j))],
        out_specs=pl.BlockSpec((tm, tn), lambda i, j: (i, j)),
        out_shape=jax.ShapeDtypeStruct((t, n), BF16),
        scratch_shapes=[pltpu.VMEM((tm, d), BF16)],
        compiler_params=_cparams(("parallel", "arbitrary")),
        name="norm_proj",
    )(x, nw.reshape(1, d), w)


def _compress_kernel(x_ref, pe_ref, w1_ref, w2_ref, o_ref):
    x = x_ref[...]
    nc = x.shape[0]
    w1t = w1_ref[0]
    w1b = w1_ref[1]
    a = jnp.dot(x, w1t, preferred_element_type=F32)
    b = jnp.dot(x, w1b, preferred_element_type=F32)
    pe = pe_ref[...]
    pe_term = (jnp.dot(pe[0], w1t, preferred_element_type=F32)
               + jnp.dot(pe[1], w1b, preferred_element_type=F32))[0:1]
    pre = a + pltpu.roll(b, nc - 1, 0) + pe_term
    hid = jax.nn.gelu(pre)
    o_ref[...] = jnp.dot(hid.astype(BF16), w2_ref[...], preferred_element_type=F32).astype(o_ref.dtype)


def _compress(xkv, pe, w1, w2):
    _, b, nc, kw = xkv.shape
    return pl.pallas_call(
        _compress_kernel,
        grid=(2, b),
        in_specs=[pl.BlockSpec((None, None, nc, kw), lambda s, i: (s, i, 0, 0)),
                  pl.BlockSpec((None, 2, 8, kw), lambda s, i: (s, 0, 0, 0)),
                  pl.BlockSpec((None, 2, kw, LANES), lambda s, i: (s, 0, 0, 0)),
                  pl.BlockSpec((None, LANES, LANES), lambda s, i: (s, 0, 0))],
        out_specs=pl.BlockSpec((None, None, nc, LANES), lambda s, i: (s, i, 0, 0)),
        out_shape=jax.ShapeDtypeStruct((2, b, nc, LANES), BF16),
        compiler_params=_cparams(("parallel", "parallel")),
        name="nsa_compress",
    )(xkv, pe, w1, w2)


def _with_ones(v):
    return jnp.concatenate([v, jnp.ones(v.shape, v.dtype)], axis=1)


def _softmax_values(s, v1):
    p = jnp.exp2(s - jnp.max(s, axis=-1, keepdims=True)).astype(BF16)
    r = jnp.dot(p, v1, preferred_element_type=F32)
    return r[:, 0:LANES] / r[:, LANES:2 * LANES]


def _qk(lhs, rhs):
    return lax.dot_general(lhs, rhs, (((1,), (1,)), ((), ())), preferred_element_type=F32)


def _lane_onehot(width, lane_idx):
    klane = lax.broadcasted_iota(jnp.int32, (width, LANES), 1)
    return (klane == lane_idx).astype(BF16)


def _pipelined_attention(c, block, groups, lhs_ref, write_near_mask, k_ref, v_ref, dp_ref, dd_ref,
                         s_refs, p_refs, a_refs, m_ref, acc_ref, t_refs=None):
    tq = ATT_TILE
    tk = FAR_TILE
    rows = lhs_ref.shape[0]
    grows = rows // groups
    j_max = k_ref.shape[0] // tk - 1
    per = tk // block
    n_far = (jnp.maximum(c - 1, 0) * tq + tk - 1) // tk
    n_loop = 2 * ((n_far + 1) // 2)
    krow = lax.broadcasted_iota(jnp.int32, (tk, LANES), 0) // block
    klane = lax.broadcasted_iota(jnp.int32, (tk, LANES), 1)

    m_ref[...] = jnp.full(m_ref.shape, NEG_INF, F32)
    acc_ref[...] = jnp.zeros(acc_ref.shape, F32)
    s_refs[1][...] = jnp.full((rows, tk), -jnp.inf, F32)
    if t_refs is not None:
        t_refs[1][...] = jnp.full((rows, LANES), -jnp.inf, F32)
    p_refs[0][...] = jnp.zeros((rows, tk), BF16)
    a_refs[0][...] = jnp.ones((rows, LANES), F32)

    def far_rows(j):
        start = pl.multiple_of(jnp.clip(j, 0, j_max) * tk, tk)
        return lambda ref, g: ref[pl.ds(start, tk), g * LANES:(g + 1) * LANES]

    def near_rows(ref, g):
        cols = slice(g * LANES, (g + 1) * LANES)
        prev = ref[pl.ds(pl.multiple_of(jnp.maximum(c - 1, 0) * tq, tq), tq), cols]
        return jnp.concatenate([prev, ref[pl.ds(pl.multiple_of(c * tq, tq), tq), cols]], axis=0)

    def value_stage(slot, window):
        for g in range(groups):
            gr = slice(g * grows, (g + 1) * grows)
            acc_ref[gr, :] = (jnp.tile(a_refs[slot][gr, :], (1, 2)) * acc_ref[gr, :]
                              + jnp.dot(p_refs[slot][gr, :], _with_ones(window(v_ref, g)),
                                        preferred_element_type=F32))

    def softmax_stage(slot, table=None):
        s = s_refs[slot][...]
        if table is not None:
            s = s + table
        if table is not None or t_refs is None:
            row_max = jnp.max(s, axis=-1, keepdims=True)
        else:
            row_max = t_refs[slot][...]
        m_prev = m_ref[...]
        m_new = jnp.maximum(m_prev, row_max)
        a_refs[slot][...] = jnp.exp2(m_prev - m_new)
        p_refs[slot][...] = jnp.exp2(s - jnp.tile(m_new, (1, tk // LANES))).astype(BF16)
        m_ref[...] = m_new

    def score_stage(slot, window, lanes):
        onehot = (klane == lanes).astype(BF16)
        for g in range(groups):
            gr = slice(g * grows, (g + 1) * grows)
            s = _qk(lhs_ref[gr, :], jnp.concatenate([window(k_ref, g), onehot], axis=1))
            s_refs[slot][gr, :] = s
            if t_refs is not None:
                t_refs[slot][gr, :] = jnp.broadcast_to(jnp.max(s, axis=-1, keepdims=True), (grows, LANES))

    def loop_stage(i, slot):
        value_stage(slot, far_rows(i - 2))
        softmax_stage(1 - slot)
        score_stage(slot, far_rows(i), jnp.where(i < n_far, krow + i * per, LANES - 1))

    def body(ii, carry):
        loop_stage(2 * ii, 0)
        loop_stage(2 * ii + 1, 1)
        return carry

    lax.fori_loop(0, n_loop // 2, body, 0)

    value_stage(0, far_rows(n_loop - 2))
    softmax_stage(1)
    write_near_mask()
    near_lanes = jnp.where((krow >= tq // block) | (c >= 1), krow + (c - 1) * (tq // block), LANES - 1)
    score_stage(0, near_rows, near_lanes)
    value_stage(1, far_rows(n_loop - 1))
    softmax_stage(0, jnp.concatenate([dp_ref[...], dd_ref[...]], axis=1))
    value_stage(0, near_rows)
    return acc_ref[:, 0:LANES] / acc_ref[:, LANES:2 * LANES]


def _topk_mask(vals, k):
    lane = lax.broadcasted_iota(jnp.int32, vals.shape, 1).astype(F32)
    sel = jnp.zeros(vals.shape, jnp.bool_)
    for _ in range(k):
        m = jnp.max(vals, axis=-1, keepdims=True)
        idx = jnp.min(jnp.where(vals == m, lane, float(LANES)), axis=-1, keepdims=True)
        hit = lane == idx
        sel = jnp.logical_or(sel, hit)
        vals = jnp.where(hit, -jnp.inf, vals)
    return sel


def _topk_mask_rows(vals, k):
    row = lax.broadcasted_iota(jnp.int32, vals.shape, 0).astype(F32)
    sel = jnp.zeros(vals.shape, jnp.bool_)
    for _ in range(k):
        m = jnp.max(vals, axis=0, keepdims=True)
        idx = jnp.min(jnp.where(vals == m, row, float(vals.shape[0])), axis=0, keepdims=True)
        hit = row == idx
        sel = jnp.logical_or(sel, hit)
        vals = jnp.where(hit, -jnp.inf, vals)
    return sel


def _nsa_kernel(q_ref, ks_ref, vs_ref, kw_ref, vw_ref, g_ref, kc_ref, vc_ref, ov_ref, tc_ref, dp_ref, dd_ref, wt_ref,
                o_ref, lhs_ref, nm_ref, m_ref, acc_ref, out_ref, s0_ref, s1_ref, p0_ref, p1_ref, a0_ref, a1_ref, t0_ref,
                t1_ref):
    tq = ATT_TILE
    nh = NSA_HEADS
    rows = nh * tq
    c = pl.program_id(1)
    t0 = c * tq
    ncp = kc_ref.shape[0]
    wcols = 2 * (tq // CMP_STRIDE)

    lane = lax.broadcasted_iota(jnp.int32, (tq, LANES), 1)
    rowi = lax.broadcasted_iota(jnp.int32, (tq, LANES), 0)
    half = lane // NSA_HEAD_DIM

    q = q_ref[...]
    for g in range(NSA_KV_GROUPS):
        for r in range(NSA_REP):
            h = g * NSA_REP + r
            qb = q[:, r * LANES:(r + 1) * LANES]
            lhs_ref[h * tq:(h + 1) * tq, 0:LANES] = jnp.where(half == g, qb, jnp.zeros_like(qb))
    qrows = lhs_ref[:, 0:LANES]

    gates = jax.nn.sigmoid(g_ref[...].astype(F32))

    def gate_col(h, br):
        col = h * N_NSA_BRANCH + br
        return gates[:, col:col + 1]

    prev_start = pl.multiple_of(jnp.maximum(c - 1, 0) * tq, tq)
    diag_start = pl.multiple_of(c * tq, tq)

    ii = lax.broadcasted_iota(jnp.int32, (ncp, LANES), 0) - (c * (tq // CMP_STRIDE) - tq // CMP_STRIDE)
    ww = lax.broadcasted_iota(jnp.int32, (ncp, LANES), 1)
    place = (((ww < wcols) & (ii == ww)) | ((ww == wcols) & (ii >= wcols))).astype(BF16)
    s = _qk(jnp.concatenate([qrows, tc_ref[...]], axis=1), jnp.concatenate([kc_ref[...], place], axis=1))
    m = jnp.max(s, axis=-1, keepdims=True)
    p = jnp.exp2(s - m)
    p_hi = p.astype(BF16)
    p_lo = (p - p_hi.astype(F32)).astype(BF16)
    ov = ov_ref[...]
    r1 = jnp.dot(p_hi, jnp.concatenate([_with_ones(vc_ref[...]), ov], axis=1), preferred_element_type=F32)
    r2 = jnp.dot(p_lo, ov, preferred_element_type=F32)
    has_key = m > 0.5 * NEG_INF
    inv_l = jnp.where(has_key, 1.0 / r1[:, LANES:2 * LANES], 0.0)
    o_cmp = r1[:, 0:LANES] * inv_l
    imp_h = (r1[:, 2 * LANES:3 * LANES] + r2) * inv_l
    for h in range(nh):
        sl = slice(h * tq, (h + 1) * tq)
        out_ref[sl, :] = gate_col(h, 0) * o_cmp[sl]

    cur = (rowi + t0) // SEL_BLOCK
    forced = (lane == 0) | (lane == cur) | (lane == cur - 1)
    far_blocks = (c - 1) * (tq // SEL_BLOCK)
    for g in range(NSA_KV_GROUPS):
        imp = imp_h[g * NSA_REP * tq:(g * NSA_REP + 1) * tq]
        for r in range(1, NSA_REP):
            imp = imp + imp_h[(g * NSA_REP + r) * tq:(g * NSA_REP + r + 1) * tq]
        imp = jnp.where(forced, TOP_BONUS, jnp.where(lane <= cur, imp, NEG_INF))
        sel = _topk_mask(imp, SEL_TOPK)
        nm_ref[g] = jnp.where(sel, 0.0, NEG_INF).astype(BF16)
        far_mask = jnp.where(sel & (lane < far_blocks), 0.0, NEG_INF).astype(BF16)
        for r in range(NSA_REP):
            h = g * NSA_REP + r
            lhs_ref[h * tq:(h + 1) * tq, LANES:2 * LANES] = far_mask

    wpat = jnp.where(lax.broadcasted_iota(jnp.int32, (rows, LANES), 1) == 1, NEG_INF, 0.0).astype(BF16)
    back2_start = pl.multiple_of(jnp.maximum(c - 2, 0) * tq, tq)
    kwin = jnp.concatenate([kw_ref[pl.ds(back2_start, tq), :], kw_ref[pl.ds(prev_start, tq), :],
                            kw_ref[pl.ds(diag_start, tq), :]], axis=0)
    vwin = jnp.concatenate([vw_ref[pl.ds(back2_start, tq), :], vw_ref[pl.ds(prev_start, tq), :],
                            vw_ref[pl.ds(diag_start, tq), :]], axis=0)
    hot = jnp.concatenate([_lane_onehot(tq, jnp.where(c >= 2, 0, 1)), _lane_onehot(tq, jnp.where(c >= 1, 0, 1)),
                           _lane_onehot(tq, 0)], axis=0)
    s = _qk(jnp.concatenate([qrows, wpat], axis=1), jnp.concatenate([kwin, hot], axis=1))
    s = s + jnp.concatenate([jnp.tile(wt_ref[...], (nh, 1)), dp_ref[...], dd_ref[...]], axis=1)
    o_win = _softmax_values(s, _with_ones(vwin))
    for h in range(nh):
        sl = slice(h * tq, (h + 1) * tq)
        out_ref[sl, :] = out_ref[sl, :] + gate_col(h, 2) * o_win[sl]

    def write_near_mask():
        for h in range(nh):
            lhs_ref[h * tq:(h + 1) * tq, LANES:2 * LANES] = nm_ref[h // NSA_REP]

    o_sel = _pipelined_attention(c, SEL_BLOCK, 1, lhs_ref, write_near_mask, ks_ref, vs_ref, dp_ref, dd_ref,
                                 (s0_ref, s1_ref), (p0_ref, p1_ref), (a0_ref, a1_ref), m_ref, acc_ref,
                                 (t0_ref, t1_ref))

    for r in range(NSA_REP):
        h0, h1 = r, NSA_REP + r
        o0 = out_ref[h0 * tq:(h0 + 1) * tq, :] + gate_col(h0, 1) * o_sel[h0 * tq:(h0 + 1) * tq]
        o1 = out_ref[h1 * tq:(h1 + 1) * tq, :] + gate_col(h1, 1) * o_sel[h1 * tq:(h1 + 1) * tq]
        o_ref[:, r * LANES:(r + 1) * LANES] = jnp.where(half == 0, o0, o1).astype(o_ref.dtype)


def _nsa_attention(proj, kcmp, vcmp, overlap, tcmp, dprev, ddiag, wtab):
    b, s, _ = proj.shape
    tq = ATT_TILE
    ncp = kcmp.shape[1]
    rows = NSA_HEADS * tq

    def col(name, width):
        return SEC[name][0] // width

    full = lambda name: _const_spec((None, s, LANES), lambda i, c, n=name: (i, 0, col(n, LANES)))
    return pl.pallas_call(
        _nsa_kernel,
        grid=(b, s // tq),
        in_specs=[pl.BlockSpec((None, tq, 512), lambda i, c: (i, c, col("a_q", 512))),
                  full("a_ks"), full("a_vs"), full("a_kw"), full("a_vw"),
                  pl.BlockSpec((None, tq, LANES), lambda i, c: (i, c, col("a_g", LANES))),
                  _const_spec((None, ncp, LANES), lambda i, c: (i, 0, 0)),
                  _const_spec((None, ncp, LANES), lambda i, c: (i, 0, 0)),
                  _const_spec((ncp, LANES), lambda i, c: (0, 0)),
                  _const_spec((rows, LANES), lambda i, c: (0, 0)),
                  _const_spec((rows, tq), lambda i, c: (0, 0)),
                  _const_spec((rows, tq), lambda i, c: (0, 0)),
                  _const_spec((tq, tq), lambda i, c: (0, 0))],
        out_specs=pl.BlockSpec((None, tq, 512), lambda i, c: (i, c, 0)),
        out_shape=jax.ShapeDtypeStruct((b, s, 512), BF16),
        scratch_shapes=[pltpu.VMEM((rows, 2 * LANES), BF16),
                        pltpu.VMEM((NSA_KV_GROUPS, tq, LANES), BF16),
                        pltpu.VMEM((rows, LANES), F32), pltpu.VMEM((rows, 2 * LANES), F32),
                        pltpu.VMEM((rows, LANES), F32),
                        pltpu.VMEM((rows, FAR_TILE), F32), pltpu.VMEM((rows, FAR_TILE), F32),
                        pltpu.VMEM((rows, FAR_TILE), BF16), pltpu.VMEM((rows, FAR_TILE), BF16),
                        pltpu.VMEM((rows, LANES), F32), pltpu.VMEM((rows, LANES), F32),
                        pltpu.VMEM((rows, LANES), F32), pltpu.VMEM((rows, LANES), F32)],
        compiler_params=_cparams(("parallel", "arbitrary")),
        name="nsa_attention",
    )(proj, proj, proj, proj, proj, proj, kcmp, vcmp, overlap, tcmp, dprev, ddiag, wtab)


def _gla_kernel(q_ref, k_ref, v_ref, r_ref, lr_ref, w2_ref, gb_ref, gn_ref, o_ref, st_ref, oi_ref, bc_ref):
    ch = GLA_CHUNK
    lb = q_ref.shape[0]
    nch = lb // ch

    @pl.when(pl.program_id(1) == 0)
    def _():
        st_ref[...] = jnp.zeros(st_ref.shape, F32)

    x = jnp.dot(lr_ref[...], w2_ref[...], preferred_element_type=F32) + gb_ref[...]
    log_a = (jnp.minimum(x, 0.0) - jnp.log1p(jnp.exp(-jnp.abs(x)))) / GLA_GATE_TAU

    ti = lax.broadcasted_iota(jnp.int32, (lb, lb), 0)
    tj = lax.broadcasted_iota(jnp.int32, (lb, lb), 1)
    same_chunk = (ti // ch) == (tj // ch)
    causal = same_chunk & (tj <= ti)
    g_hi, g_mid, g_lo = _split3(log_a)

    def chunk_sums(w):
        return (jnp.dot(w, g_hi, preferred_element_type=F32) + jnp.dot(w, g_mid, preferred_element_type=F32)
                + jnp.dot(w, g_lo, preferred_element_type=F32))

    bcum = chunk_sums(causal.astype(BF16))
    btot = chunk_sums(same_chunk.astype(BF16))
    half = lax.broadcasted_iota(jnp.int32, (lb, LANES), 1) // GLA_KEY_DIM
    scale = GLA_KEY_DIM ** -0.5

    for p in range(GLA_HEADS // 2):
        cols = slice(p * LANES, (p + 1) * LANES)
        bc = bcum[:, cols]
        qe = q_ref[:, cols].astype(F32) * scale * jnp.exp(bc)
        kinv = (k_ref[:, cols].astype(F32) * jnp.exp(-bc)).astype(BF16)
        for a in range(2):
            h = 2 * p + a
            hc = slice(h * LANES, (h + 1) * LANES)
            qa = jnp.where(half == a, qe, 0.0).astype(BF16)
            attn = jnp.where(causal, _qk(qa, kinv), 0.0).astype(BF16)
            oi_ref[:, hc] = jnp.dot(attn, v_ref[:, hc], preferred_element_type=F32)

    @pl.when(jnp.min(bcum) < -GLA_FACTORISED_RANGE)
    def _():
        bc_ref[...] = bcum
        rowc = lax.broadcasted_iota(jnp.int32, (ch, LANES), 0)
        lanec = lax.broadcasted_iota(jnp.int32, (ch, LANES), 1)

        def chunk_step(cc, carry):
            sl = pl.ds(pl.multiple_of(cc * ch, ch), ch)
            for p in range(GLA_HEADS // 2):
                cols = slice(p * LANES, (p + 1) * LANES)
                qc = q_ref[sl, cols].astype(F32) * scale
                kc = k_ref[sl, cols].astype(F32)
                bcc = bc_ref[sl, cols]
                v0 = v_ref[sl, 2 * p * LANES:(2 * p + 1) * LANES].astype(F32)
                v1 = v_ref[sl, (2 * p + 1) * LANES:(2 * p + 2) * LANES].astype(F32)
                acc0 = jnp.zeros((ch, LANES), F32)
                acc1 = jnp.zeros((ch, LANES), F32)
                for j in range(ch):
                    w = jnp.where(rowc >= j, jnp.exp(jnp.minimum(bcc - bcc[j:j + 1], 0.0)), 0.0)
                    t = qc * (kc[j:j + 1] * w)
                    a0 = jnp.sum(jnp.where(lanec < GLA_KEY_DIM, t, 0.0), axis=-1, keepdims=True)
                    a1 = jnp.sum(jnp.where(lanec >= GLA_KEY_DIM, t, 0.0), axis=-1, keepdims=True)
                    acc0 = acc0 + a0 * jnp.broadcast_to(v0[j:j + 1], (ch, LANES))
                    acc1 = acc1 + a1 * jnp.broadcast_to(v1[j:j + 1], (ch, LANES))
                oi_ref[sl, 2 * p * LANES:(2 * p + 1) * LANES] = acc0
                oi_ref[sl, (2 * p + 1) * LANES:(2 * p + 2) * LANES] = acc1
            return carry

        lax.fori_loop(0, nch, chunk_step, 0)

    gn = gn_ref[...]
    for p in range(GLA_HEADS // 2):
        cols = slice(p * LANES, (p + 1) * LANES)
        bc = bcum[:, cols]
        bt = btot[:, cols]
        kf = k_ref[:, cols].astype(F32)
        qe = q_ref[:, cols].astype(F32) * scale * jnp.exp(bc)
        klast = (kf * jnp.exp(bt - bc)).astype(BF16)
        decay = jnp.exp(bt)
        for a in range(2):
            h = 2 * p + a
            hc = slice(h * LANES, (h + 1) * LANES)
            qa = jnp.where(half == a, qe, 0.0).astype(BF16)
            v = v_ref[:, hc]
            outs = []
            st = st_ref[h]
            for cc in range(nch):
                sl = slice(cc * ch, (cc + 1) * ch)
                outs.append(_qk(qa[sl], st.astype(BF16)))
                upd = lax.dot_general(v[sl], klast[sl], (((0,), (0,)), ((), ())), preferred_element_type=F32)
                st = st * decay[cc * ch:cc * ch + 1, :] + upd
            st_ref[h] = st
            o = oi_ref[:, hc] + jnp.concatenate(outs, axis=0)
            rg = r_ref[:, hc].astype(F32)
            o_ref[:, hc] = (_rms(o, gn) * (rg * jax.nn.sigmoid(rg))).astype(o_ref.dtype)


def _split3(x):
    hi = x.astype(BF16)
    r1 = x - hi.astype(F32)
    mid = r1.astype(BF16)
    lo = (r1 - mid.astype(F32)).astype(BF16)
    return hi, mid, lo


def _gla(proj, w2, gb, gn, lb):
    b, s, _ = proj.shape

    def spec(name, width):
        return pl.BlockSpec((None, lb, width), lambda i, c, n=name, w=width: (i, c, SEC[n][0] // w))

    return pl.pallas_call(
        _gla_kernel,
        grid=(b, s // lb),
        in_specs=[spec("b_q", 256), spec("b_k", 256), spec("b_v", 512), spec("b_r", 512), spec("b_lr", LANES),
                  pl.BlockSpec((LANES, 256), lambda i, c: (0, 0)),
                  pl.BlockSpec((1, 256), lambda i, c: (0, 0)),
                  pl.BlockSpec((1, LANES), lambda i, c: (0, 0))],
        out_specs=pl.BlockSpec((None, lb, 512), lambda i, c: (i, c, 0)),
        out_shape=jax.ShapeDtypeStruct((b, s, 512), BF16),
        scratch_shapes=[pltpu.VMEM((GLA_HEADS, LANES, LANES), F32), pltpu.VMEM((lb, 512), F32),
                        pltpu.VMEM((lb, 256), F32)],
        compiler_params=_cparams(("parallel", "arbitrary")),
        name="gla",
    )(proj, proj, proj, proj, proj, w2, gb, gn)


def _kmean_kernel(a_ref, k_ref, o_ref):
    o_ref[...] = jnp.dot(a_ref[...], k_ref[...], preferred_element_type=F32).astype(o_ref.dtype)


def _moba_kmean(proj, avg):
    b, s, _ = proj.shape
    return pl.pallas_call(
        _kmean_kernel,
        grid=(b,),
        in_specs=[pl.BlockSpec((LANES, s), lambda i: (0, 0)),
                  pl.BlockSpec((None, s, 512), lambda i: (i, 0, SEC["c_k"][0] // 512))],
        out_specs=pl.BlockSpec((None, LANES, 512), lambda i: (i, 0, 0)),
        out_shape=jax.ShapeDtypeStruct((b, LANES, 512), BF16),
        compiler_params=_cparams(("parallel",)),
        name="moba_kmean",
    )(avg, proj)


def _moba_kernel(q_ref, k_ref, v_ref, km_ref, dp_ref, dd_ref, o_ref, lhs_ref, m_ref, acc_ref,
                 s0_ref, s1_ref, p0_ref, p1_ref, a0_ref, a1_ref):
    tq = ATT_TILE
    nh = MOBA_HEADS
    npair = nh // 2
    rows = nh * tq
    c = pl.program_id(1)

    lane = lax.broadcasted_iota(jnp.int32, (tq, LANES), 1)
    half = lane // MOBA_HEAD_DIM
    for h in range(nh):
        qb = q_ref[:, (h // 2) * LANES:(h // 2 + 1) * LANES]
        lhs_ref[h * tq:(h + 1) * tq, 0:LANES] = jnp.where(half == h % 2, qb, jnp.zeros_like(qb))

    nblk = k_ref.shape[0] // MOBA_BLOCK
    score = jnp.concatenate(
        [_qk(km_ref[0:nblk, p * LANES:(p + 1) * LANES], lhs_ref[2 * p * tq:2 * (p + 1) * tq, 0:LANES])
         for p in range(npair)], axis=1)
    blk = lax.broadcasted_iota(jnp.int32, (nblk, rows), 0)
    chosen = _topk_mask_rows(jnp.where(blk < c, score, NEG_INF), MOBA_TOPK) & (blk < c)
    chosen = jnp.concatenate([jnp.where(chosen, 1.0, 0.0), jnp.zeros((LANES - nblk, rows), F32)], axis=0)
    past = jnp.transpose(chosen) > 0.5
    lane2 = lax.broadcasted_iota(jnp.int32, (rows, LANES), 1)
    lhs_ref[:, LANES:2 * LANES] = jnp.where(past & (lane2 < c - 1), 0.0, NEG_INF).astype(BF16)

    def write_near_mask():
        lhs_ref[:, LANES:2 * LANES] = jnp.where(past | (lane2 == c), 0.0, NEG_INF).astype(BF16)

    o = _pipelined_attention(c, MOBA_BLOCK, npair, lhs_ref, write_near_mask, k_ref, v_ref, dp_ref, dd_ref,
                             (s0_ref, s1_ref), (p0_ref, p1_ref), (a0_ref, a1_ref), m_ref, acc_ref)
    for p in range(npair):
        o0 = o[2 * p * tq:(2 * p + 1) * tq]
        o1 = o[(2 * p + 1) * tq:(2 * p + 2) * tq]
        o_ref[:, p * LANES:(p + 1) * LANES] = jnp.where(half == 0, o0, o1).astype(o_ref.dtype)


def _moba_attention(proj, kmean, dprev, ddiag):
    b, s, _ = proj.shape
    tq = ATT_TILE
    rows = MOBA_HEADS * tq

    def col(name):
        return SEC[name][0] // 512

    return pl.pallas_call(
        _moba_kernel,
        grid=(b, s // tq),
        in_specs=[pl.BlockSpec((None, tq, 512), lambda i, c: (i, c, col("c_q"))),
                  _const_spec((None, s, 512), lambda i, c: (i, 0, col("c_k"))),
                  _const_spec((None, s, 512), lambda i, c: (i, 0, col("c_v"))),
                  _const_spec((None, LANES, 512), lambda i, c: (i, 0, 0)),
                  _const_spec((rows, tq), lambda i, c: (0, 0)),
                  _const_spec((rows, tq), lambda i, c: (0, 0))],
        out_specs=pl.BlockSpec((None, tq, 512), lambda i, c: (i, c, 0)),
        out_shape=jax.ShapeDtypeStruct((b, s, 512), BF16),
        scratch_shapes=[pltpu.VMEM((rows, 2 * LANES), BF16),
                        pltpu.VMEM((rows, LANES), F32), pltpu.VMEM((rows, 2 * LANES), F32),
                        pltpu.VMEM((rows, FAR_TILE), F32), pltpu.VMEM((rows, FAR_TILE), F32),
                        pltpu.VMEM((rows, FAR_TILE), BF16), pltpu.VMEM((rows, FAR_TILE), BF16),
                        pltpu.VMEM((rows, LANES), F32), pltpu.VMEM((rows, LANES), F32)],
        compiler_params=_cparams(("parallel", "arbitrary")),
        name="moba_attention",
    )(proj, proj, proj, kmean, dprev, ddiag)


def _merge_kernel(oa_ref, ob_ref, oc_ref, g0_ref, g1_ref, g2_ref, x_ref, wa_ref, wb_ref, wc_ref, wo_ref, nw_ref,
                  o_ref):
    def branch(o, w, g):
        return jax.nn.sigmoid(g[...].astype(F32)) * jnp.dot(o[...], w[...], preferred_element_type=F32)

    merged = branch(oa_ref, wa_ref, g0_ref) + branch(ob_ref, wb_ref, g1_ref) + branch(oc_ref, wc_ref, g2_ref)
    y = jnp.dot(merged.astype(BF16), wo_ref[...], preferred_element_type=F32)
    o_ref[...] = x_ref[...] + _rms(y, nw_ref[...])


def _merge(oa, ob, oc, proj, x, wa, wb, wc, wo, nw, tm):
    t, d = x.shape
    w = oa.shape[1]
    row = lambda width, j=0: pl.BlockSpec((tm, width), lambda i, j=j: (i, j))
    const = lambda shape: pl.BlockSpec(shape, lambda i: (0, 0))
    return pl.pallas_call(
        _merge_kernel,
        grid=(t // tm,),
        in_specs=[row(w), row(w), row(w), row(d, 0), row(d, 1), row(d, 2), row(d),
                  const((w, d)), const((w, d)), const((w, d)), const((d, d)), const((1, d))],
        out_specs=row(d),
        out_shape=jax.ShapeDtypeStruct((t, d), F32),
        compiler_params=_cparams(("parallel",)),
        name="merge_out",
    )(oa, ob, oc, proj, proj, proj, x, wa, wb, wc, wo, nw.reshape(1, d))


def _ffn_kernel(x_ref, npre_ref, wg_ref, wu_ref, wo_ref, npost_ref, o_ref, h_ref, acc_ref):
    j = pl.program_id(1)

    @pl.when(j == 0)
    def _():
        h_ref[...] = _rms(x_ref[...], npre_ref[...]).astype(BF16)
        acc_ref[...] = jnp.zeros(acc_ref.shape, F32)

    h = h_ref[...]
    gate = jnp.dot(h, wg_ref[...], preferred_element_type=F32)
    up = jnp.dot(h, wu_ref[...], preferred_element_type=F32)
    act = (gate * jax.nn.sigmoid(gate) * up).astype(BF16)
    acc_ref[...] += jnp.dot(act, wo_ref[...], preferred_element_type=F32)

    @pl.when(j == pl.num_programs(1) - 1)
    def _():
        o_ref[...] = x_ref[...] + _rms(acc_ref[...], npost_ref[...])


def _ffn(x, npre, wg, wu, wo, npost, tm, th):
    t, d = x.shape
    hid = wg.shape[1]
    return pl.pallas_call(
        _ffn_kernel,
        grid=(t // tm, hid // th),
        in_specs=[pl.BlockSpec((tm, d), lambda i, j: (i, 0)),
                  pl.BlockSpec((1, d), lambda i, j: (0, 0)),
                  pl.BlockSpec((d, th), lambda i, j: (0, j)),
                  pl.BlockSpec((d, th), lambda i, j: (0, j)),
                  pl.BlockSpec((th, d), lambda i, j: (j, 0)),
                  pl.BlockSpec((1, d), lambda i, j: (0, 0))],
        out_specs=pl.BlockSpec((tm, d), lambda i, j: (i, 0)),
        out_shape=jax.ShapeDtypeStruct((t, d), F32),
        scratch_shapes=[pltpu.VMEM((tm, d), BF16), pltpu.VMEM((tm, d), F32)],
        compiler_params=_cparams(("parallel", "arbitrary")),
        name="ffn",
    )(x, npre.reshape(1, d), wg, wu, wo, npost.reshape(1, d))


def _t5_bucket_table(dist):
    n = np.maximum(dist, 0)
    max_exact = NUM_BUCKETS // 2
    log_ratio = np.log(np.maximum(n, 1).astype(np.float32) / max_exact) / math.log(T5_MAX_DISTANCE / max_exact)
    large = max_exact + (log_ratio * (NUM_BUCKETS - max_exact)).astype(np.int32)
    return np.where(n < max_exact, n, np.minimum(large, NUM_BUCKETS - 1)).astype(np.int32)


def _bias_lookup(rel, dist):
    onehot = (jnp.asarray(_t5_bucket_table(dist))[..., None] == jnp.arange(NUM_BUCKETS)).astype(F32)
    return jnp.dot(onehot, (rel - rel[NUM_BUCKETS - 1]) * LOG2E, precision=lax.Precision.HIGHEST)


def _near_bias(rel, t):
    qi = np.arange(t)[:, None]
    ki = np.arange(t)[None, :]
    prev = _bias_lookup(rel, qi + t - ki)
    diag = jnp.where((qi >= ki)[..., None], _bias_lookup(rel, qi - ki), NEG_INF)
    flat = lambda b: b.transpose(2, 0, 1).reshape(-1, t)
    return flat(prev), flat(diag)


def _cmp_bias(rel, t):
    w = t // CMP_STRIDE
    qi = np.arange(t)[:, None]
    j = np.arange(LANES)[None, :]
    d = qi - CMP_STRIDE * (j - w) - (CMP_LEN - 1)
    b = jnp.where(((j < 2 * w) & (d >= 0))[..., None], _bias_lookup(rel, d),
                  jnp.asarray(np.where(j <= 2 * w, NEG_INF, 0.0) * np.ones_like(d), F32)[..., None])
    return b.transpose(2, 0, 1).reshape(-1, LANES).astype(BF16)


def _overlap(s, nc):
    n_cmp = (s - CMP_LEN) // CMP_STRIDE + 1
    cmp_end = np.arange(nc) * CMP_STRIDE + CMP_LEN - 1
    cmp_start = cmp_end - (CMP_LEN - 1)
    sb_start = np.arange(LANES) * SEL_BLOCK
    ov = (cmp_start[:, None] < sb_start[None, :] + SEL_BLOCK) & (cmp_end[:, None] >= sb_start[None, :])
    ov = ov & (np.arange(nc)[:, None] < n_cmp) & (np.arange(LANES)[None, :] < s // SEL_BLOCK)
    return jnp.asarray(ov, BF16)


def _pair_diag(w):
    z = jnp.zeros_like(w)
    return jnp.concatenate([jnp.concatenate([w, z], axis=-1), jnp.concatenate([z, w], axis=-1)], axis=-2)


def kernel(x, rel_bias, norm_mix_pre, norm_mix_post, norm_ffn_pre, norm_ffn_post, w_in, nsa_pe_k, nsa_pe_v, nsa_cmp_k_w1, nsa_cmp_k_w2, nsa_cmp_v_w1, nsa_cmp_v_w2, gla_gate_w2, gla_gate_b, gla_norm, w_branch_a, w_branch_b, w_branch_c, w_out, w_ffn_in, w_ffn_out):
    b, s, d = x.shape
    depth = w_in.shape[0]
    t = b * s
    dk = NSA_HEAD_DIM
    tq = ATT_TILE
    assert d == D_MODEL and w_in.shape[2] == D_IN
    assert WINDOW == 2 * tq and MOBA_BLOCK == tq and s % tq == 0
    assert SEL_TOPK <= s // SEL_BLOCK <= LANES and s // MOBA_BLOCK <= LANES
    nc = s // CMP_STRIDE
    assert nc % LANES == 0

    w_in_p = _layout_w_in(w_in)
    rel_a = rel_bias[:, :NSA_HEADS]
    rel_c = rel_bias[:, NSA_HEADS:]
    dprev_a, ddiag_a = _near_bias(rel_a, tq)
    dprev_c, ddiag_c = _near_bias(rel_c, tq)
    tcmp = _cmp_bias(rel_a, tq)
    overlap = _overlap(s, nc)
    wtab = jnp.asarray(np.where(np.arange(tq)[None, :] > np.arange(tq)[:, None], 0.0, NEG_INF), F32)
    avg = jnp.asarray((np.arange(LANES)[:, None] == np.arange(s)[None, :] // MOBA_BLOCK) / MOBA_BLOCK, BF16)

    def cmp_w1(w1):
        w = _pair_diag(w1.reshape(depth, 2, CMP_STRIDE, dk, dk))
        return w.reshape(depth, 2, CMP_STRIDE * 2 * dk, 2 * dk)

    def cmp_pe(pe):
        p2 = jnp.concatenate([pe, pe], axis=-1).reshape(depth, 2, 1, CMP_STRIDE * 2 * dk)
        return jnp.broadcast_to(p2, (depth, 2, 8, CMP_STRIDE * 2 * dk))

    cw1 = jnp.stack([cmp_w1(nsa_cmp_k_w1), cmp_w1(nsa_cmp_v_w1)], axis=1).astype(BF16)
    cw2 = jnp.stack([_pair_diag(nsa_cmp_k_w2), _pair_diag(nsa_cmp_v_w2)], axis=1).astype(BF16)
    cpe = jnp.stack([cmp_pe(nsa_pe_k), cmp_pe(nsa_pe_v)], axis=1).astype(BF16)

    gw2 = jnp.concatenate([gla_gate_w2, jnp.zeros((depth, LANES - GLA_GATE_RANK, gla_gate_w2.shape[2]), F32)],
                          axis=1).astype(BF16)
    wa = (w_branch_a.reshape(depth, NSA_KV_GROUPS, NSA_REP, dk, d).transpose(0, 2, 1, 3, 4)
          .reshape(depth, NSA_HEADS * dk, d).astype(BF16))
    wb = w_branch_b.astype(BF16)
    wc = w_branch_c.astype(BF16)
    wo = w_out.astype(BF16)
    hid = w_ffn_out.shape[1]
    wg = w_ffn_in[:, :, :hid].astype(BF16)
    wu = w_ffn_in[:, :, hid:].astype(BF16)
    wf = w_ffn_out.astype(BF16)

    tm = min(1024, t)
    tn = NP_COLS // 5
    xf = x.reshape(t, d)
    for layer in range(depth):
        proj = _norm_matmul(xf, norm_mix_pre[layer], w_in_p[layer], tm, tn).reshape(b, s, NP_COLS)
        kc0, vc0 = SEC["a_kc"][0], SEC["a_vc"][0]
        xkv = jnp.stack([proj[:, :, kc0:kc0 + LANES], proj[:, :, vc0:vc0 + LANES]])
        xkv = xkv.reshape(2, b, nc, CMP_STRIDE * LANES)
        cmp = _compress(xkv, cpe[layer], cw1[layer], cw2[layer])
        o_a = _nsa_attention(proj, cmp[0], cmp[1], overlap, tcmp, dprev_a, ddiag_a, wtab)
        o_b = _gla(proj, gw2[layer], gla_gate_b[layer].reshape(1, -1), gla_norm[layer].reshape(1, -1),
                   min(256, s))
        kmean = _moba_kmean(proj, avg)
        o_c = _moba_attention(proj, kmean, dprev_c, ddiag_c)
        xf = _merge(o_a.reshape(t, -1), o_b.reshape(t, -1), o_c.reshape(t, -1), proj.reshape(t, NP_COLS), xf,
                    wa[layer], wb[layer], wc[layer], wo[layer], norm_mix_post[layer], min(512, t))
        xf = _ffn(xf, norm_ffn_pre[layer], wg[layer], wu[layer], wf[layer], norm_ffn_post[layer],
                  min(512, t), hid // 2)
    return xf.reshape(b, s, d)
```

```python
import math

import numpy as np
import jax
import jax.numpy as jnp
from jax import lax
from jax.experimental import pallas as pl
from jax.experimental.pallas import tpu as pltpu

F32 = jnp.float32
BF16 = jnp.bfloat16

NORM_EPS = 1e-6
NEG_INF = -1e30
TOP_BONUS = 1e9
NUM_BUCKETS = 32
T5_MAX_DISTANCE = 128
NSA_HEADS = 8
NSA_KV_GROUPS = 2
NSA_REP = NSA_HEADS // NSA_KV_GROUPS
NSA_HEAD_DIM = 64
CMP_LEN = 32
CMP_STRIDE = 16
SEL_BLOCK = 64
SEL_TOPK = 16
WINDOW = 512
N_NSA_BRANCH = 3
GLA_HEADS = 4
GLA_KEY_DIM = 64
GLA_VAL_DIM = 128
GLA_GATE_RANK = 16
GLA_GATE_TAU = 16.0
GLA_CHUNK = 64
GLA_FACTORISED_RANGE = 80.0
MOBA_HEADS = 8
MOBA_HEAD_DIM = 64
MOBA_BLOCK = 256
MOBA_TOPK = 3
N_BRANCHES = 3
LOG2E = math.log2(math.e)

LANES = 128
ATT_TILE = 256
FAR_TILE = 2 * ATT_TILE
FFN_CHUNK = 512
VMEM_LIMIT = 56 * 1024 * 1024

D_MODEL = 1024
SEC = {}
_off = 0
for _name, _w in (("mg", 3 * D_MODEL), ("a_q", 512), ("c_q", 512), ("c_k", 512), ("c_v", 512),
                  ("b_v", 512), ("b_r", 512), ("b_q", 256), ("b_k", 256),
                  ("a_kc", 128), ("a_vc", 128), ("a_ks", 128), ("a_vs", 128), ("a_kw", 128), ("a_vw", 128),
                  ("a_g", 128), ("b_lr", 128)):
    SEC[_name] = (_off, _w)
    _off += _w
NP_COLS = _off

_SRC_NAMES = ("a_q", "a_kc", "a_vc", "a_ks", "a_vs", "a_kw", "a_vw", "a_g", "b_q", "b_k", "b_v", "b_r", "b_lr",
              "c_q", "c_k", "c_v", "mg")
_SRC_SIZES = (512, 128, 128, 128, 128, 128, 128, 24, 256, 256, 512, 512, 16, 512, 512, 512, 3 * D_MODEL)
_SRC_OFF = dict(zip(_SRC_NAMES, np.cumsum((0,) + _SRC_SIZES[:-1]).tolist()))
_SRC_W = dict(zip(_SRC_NAMES, _SRC_SIZES))
D_IN = int(sum(_SRC_SIZES))


def _layout_w_in(w_in):
    depth, d, _ = w_in.shape

    def src(name, lo=0, hi=None):
        hi = _SRC_W[name] if hi is None else hi
        return w_in[:, :, _SRC_OFF[name] + lo:_SRC_OFF[name] + hi]

    parts = []
    for name, (_, width) in SEC.items():
        if name == "a_q":
            for r in range(NSA_REP):
                for g in range(NSA_KV_GROUPS):
                    h = g * NSA_REP + r
                    parts.append(src(name, h * NSA_HEAD_DIM, (h + 1) * NSA_HEAD_DIM) * (NSA_HEAD_DIM ** -0.5 * LOG2E))
        elif name == "c_q":
            parts.append(src(name) * (MOBA_HEAD_DIM ** -0.5 * LOG2E))
        else:
            parts.append(src(name))
            if _SRC_W[name] < width:
                parts.append(jnp.zeros((depth, d, width - _SRC_W[name]), w_in.dtype))
    return jnp.concatenate(parts, axis=2).astype(BF16)


def _cparams(sem):
    return pltpu.CompilerParams(dimension_semantics=sem, vmem_limit_bytes=VMEM_LIMIT)


def _const_spec(shape, index_map):
    return pl.BlockSpec(shape, index_map, pipeline_mode=pl.Buffered(1))


def _rms(y, w):
    return y * lax.rsqrt(jnp.mean(y * y, axis=-1, keepdims=True) + NORM_EPS) * w


def _norm_matmul_kernel(x_ref, nw_ref, w_ref, o_ref, h_ref):
    @pl.when(pl.program_id(1) == 0)
    def _():
        h_ref[...] = _rms(x_ref[...], nw_ref[...]).astype(BF16)

    o_ref[...] = jnp.dot(h_ref[...], w_ref[...], preferred_element_type=F32).astype(o_ref.dtype)


def _norm_matmul(x, nw, w, tm, tn):
    t, d = x.shape
    n = w.shape[1]
    return pl.pallas_call(
        _norm_matmul_kernel,
        grid=(t // tm, n // tn),
        in_specs=[pl.BlockSpec((tm, d), lambda i, j: (i, 0)),
                  pl.BlockSpec((1, d), lambda i, j: (0, 0)),
                  pl.BlockSpec((d, tn), lambda i, j: (0, j))],
        out_specs=pl.BlockSpec((tm, tn), lambda i, j: (i, j)),
        out_shape=jax.ShapeDtypeStruct((t, n), BF16),
        scratch_shapes=[pltpu.VMEM((tm, d), BF16)],
        compiler_params=_cparams(("parallel", "arbitrary")),
        name="norm_proj",
    )(x, nw.reshape(1, d), w)


def _compress_kernel(x_ref, pe_ref, w1_ref, w2_ref, o_ref):
    x = x_ref[...]
    nc = x.shape[0]
    w1t = w1_ref[0]
    w1b = w1_ref[1]
    a = jnp.dot(x, w1t, preferred_element_type=F32)
    b = jnp.dot(x, w1b, preferred_element_type=F32)
    pe = pe_ref[...]
    pe_term = (jnp.dot(pe[0], w1t, preferred_element_type=F32)
               + jnp.dot(pe[1], w1b, preferred_element_type=F32))[0:1]
    pre = a + pltpu.roll(b, nc - 1, 0) + pe_term
    hid = jax.nn.gelu(pre)
    o_ref[...] = jnp.dot(hid.astype(BF16), w2_ref[...], preferred_element_type=F32).astype(o_ref.dtype)


def _compress(xkv, pe, w1, w2):
    _, b, nc, kw = xkv.shape
    return pl.pallas_call(
        _compress_kernel,
        grid=(2, b),
        in_specs=[pl.BlockSpec((None, None, nc, kw), lambda s, i: (s, i, 0, 0)),
                  pl.BlockSpec((None, 2, 8, kw), lambda s, i: (s, 0, 0, 0)),
                  pl.BlockSpec((None, 2, kw, LANES), lambda s, i: (s, 0, 0, 0)),
                  pl.BlockSpec((None, LANES, LANES), lambda s, i: (s, 0, 0))],
        out_specs=pl.BlockSpec((None, None, nc, LANES), lambda s, i: (s, i, 0, 0)),
        out_shape=jax.ShapeDtypeStruct((2, b, nc, LANES), BF16),
        compiler_params=_cparams(("parallel", "parallel")),
        name="nsa_compress",
    )(xkv, pe, w1, w2)


def _with_ones(v):
    return jnp.concatenate([v, jnp.ones(v.shape, v.dtype)], axis=1)


def _softmax_values(s, v1):
    p = jnp.exp2(s - jnp.max(s, axis=-1, keepdims=True)).astype(BF16)
    r = jnp.dot(p, v1, preferred_element_type=F32)
    return r[:, 0:LANES] / r[:, LANES:2 * LANES]


def _qk(lhs, rhs):
    return lax.dot_general(lhs, rhs, (((1,), (1,)), ((), ())), preferred_element_type=F32)


def _lane_onehot(width, lane_idx):
    klane = lax.broadcasted_iota(jnp.int32, (width, LANES), 1)
    return (klane == lane_idx).astype(BF16)


def _pipelined_attention(c, block, groups, lhs_ref, write_near_mask, k_ref, v_ref, dp_ref, dd_ref,
                         s_refs, p_refs, a_refs, m_ref, acc_ref, t_refs=None):
    tq = ATT_TILE
    tk = FAR_TILE
    rows = lhs_ref.shape[0]
    grows = rows // groups
    j_max = k_ref.shape[0] // tk - 1
    per = tk // block
    n_far = (jnp.maximum(c - 1, 0) * tq + tk - 1) // tk
    n_loop = 2 * ((n_far + 1) // 2)
    krow = lax.broadcasted_iota(jnp.int32, (tk, LANES), 0) // block
    klane = lax.broadcasted_iota(jnp.int32, (tk, LANES), 1)

    m_ref[...] = jnp.full(m_ref.shape, NEG_INF, F32)
    acc_ref[...] = jnp.zeros(acc_ref.shape, F32)
    s_refs[1][...] = jnp.full((rows, tk), -jnp.inf, F32)
    if t_refs is not None:
        t_refs[1][...] = jnp.full((rows, LANES), -jnp.inf, F32)
    p_refs[0][...] = jnp.zeros((rows, tk), BF16)
    a_refs[0][...] = jnp.ones((rows, LANES), F32)

    def far_rows(j):
        start = pl.multiple_of(jnp.clip(j, 0, j_max) * tk, tk)
        return lambda ref, g: ref[pl.ds(start, tk), g * LANES:(g + 1) * LANES]

    def near_rows(ref, g):
        cols = slice(g * LANES, (g + 1) * LANES)
        prev = ref[pl.ds(pl.multiple_of(jnp.maximum(c - 1, 0) * tq, tq), tq), cols]
        return jnp.concatenate([prev, ref[pl.ds(pl.multiple_of(c * tq, tq), tq), cols]], axis=0)

    def value_stage(slot, window):
        for g in range(groups):
            gr = slice(g * grows, (g + 1) * grows)
            acc_ref[gr, :] = (jnp.tile(a_refs[slot][gr, :], (1, 2)) * acc_ref[gr, :]
                              + jnp.dot(p_refs[slot][gr, :], _with_ones(window(v_ref, g)),
                                        preferred_element_type=F32))

    def softmax_stage(slot, table=None):
        s = s_refs[slot][...]
        if table is not None:
            s = s + table
        if table is not None or t_refs is None:
            row_max = jnp.max(s, axis=-1, keepdims=True)
        else:
            row_max = t_refs[slot][...]
        m_prev = m_ref[...]
        m_new = jnp.maximum(m_prev, row_max)
        a_refs[slot][...] = jnp.exp2(m_prev - m_new)
        p_refs[slot][...] = jnp.exp2(s - jnp.tile(m_new, (1, tk // LANES))).astype(BF16)
        m_ref[...] = m_new

    def score_stage(slot, window, lanes):
        onehot = (klane == lanes).astype(BF16)
        for g in range(groups):
            gr = slice(g * grows, (g + 1) * grows)
            s = _qk(lhs_ref[gr, :], jnp.concatenate([window(k_ref, g), onehot], axis=1))
            s_refs[slot][gr, :] = s
            if t_refs is not None:
                t_refs[slot][gr, :] = jnp.broadcast_to(jnp.max(s, axis=-1, keepdims=True), (grows, LANES))

    def loop_stage(i, slot):
        value_stage(slot, far_rows(i - 2))
        softmax_stage(1 - slot)
        score_stage(slot, far_rows(i), jnp.where(i < n_far, krow + i * per, LANES - 1))

    def body(ii, carry):
        loop_stage(2 * ii, 0)
        loop_stage(2 * ii + 1, 1)
        return carry

    lax.fori_loop(0, n_loop // 2, body, 0)

    value_stage(0, far_rows(n_loop - 2))
    softmax_stage(1)
    write_near_mask()
    near_lanes = jnp.where((krow >= tq // block) | (c >= 1), krow + (c - 1) * (tq // block), LANES - 1)
    score_stage(0, near_rows, near_lanes)
    value_stage(1, far_rows(n_loop - 1))
    softmax_stage(0, jnp.concatenate([dp_ref[...], dd_ref[...]], axis=1))
    value_stage(0, near_rows)
    return acc_ref[:, 0:LANES] / acc_ref[:, LANES:2 * LANES]


def _topk_mask(vals, k):
    lane = lax.broadcasted_iota(jnp.int32, vals.shape, 1).astype(F32)
    sel = jnp.zeros(vals.shape, jnp.bool_)
    for _ in range(k):
        m = jnp.max(vals, axis=-1, keepdims=True)
        idx = jnp.min(jnp.where(vals == m, lane, float(LANES)), axis=-1, keepdims=True)
        hit = lane == idx
        sel = jnp.logical_or(sel, hit)
        vals = jnp.where(hit, -jnp.inf, vals)
    return sel


def _topk_mask_rows(vals, k):
    row = lax.broadcasted_iota(jnp.int32, vals.shape, 0).astype(F32)
    sel = jnp.zeros(vals.shape, jnp.bool_)
    for _ in range(k):
        m = jnp.max(vals, axis=0, keepdims=True)
        idx = jnp.min(jnp.where(vals == m, row, float(vals.shape[0])), axis=0, keepdims=True)
        hit = row == idx
        sel = jnp.logical_or(sel, hit)
        vals = jnp.where(hit, -jnp.inf, vals)
    return sel


def _nsa_kernel(q_ref, ks_ref, vs_ref, kw_ref, vw_ref, g_ref, kc_ref, vc_ref, ov_ref, tc_ref, dp_ref, dd_ref, wt_ref,
                o_ref, lhs_ref, nm_ref, m_ref, acc_ref, out_ref, s0_ref, s1_ref, p0_ref, p1_ref, a0_ref, a1_ref, t0_ref,
                t1_ref):
    tq = ATT_TILE
    nh = NSA_HEADS
    rows = nh * tq
    c = pl.program_id(1)
    t0 = c * tq
    ncp = kc_ref.shape[0]
    wcols = 2 * (tq // CMP_STRIDE)

    lane = lax.broadcasted_iota(jnp.int32, (tq, LANES), 1)
    rowi = lax.broadcasted_iota(jnp.int32, (tq, LANES), 0)
    half = lane // NSA_HEAD_DIM

    q = q_ref[...]
    for g in range(NSA_KV_GROUPS):
        for r in range(NSA_REP):
            h = g * NSA_REP + r
            qb = q[:, r * LANES:(r + 1) * LANES]
            lhs_ref[h * tq:(h + 1) * tq, 0:LANES] = jnp.where(half == g, qb, jnp.zeros_like(qb))
    qrows = lhs_ref[:, 0:LANES]

    gates = jax.nn.sigmoid(g_ref[...].astype(F32))

    def gate_col(h, br):
        col = h * N_NSA_BRANCH + br
        return gates[:, col:col + 1]

    prev_start = pl.multiple_of(jnp.maximum(c - 1, 0) * tq, tq)
    diag_start = pl.multiple_of(c * tq, tq)

    ii = lax.broadcasted_iota(jnp.int32, (ncp, LANES), 0) - (c * (tq // CMP_STRIDE) - tq // CMP_STRIDE)
    ww = lax.broadcasted_iota(jnp.int32, (ncp, LANES), 1)
    place = (((ww < wcols) & (ii == ww)) | ((ww == wcols) & (ii >= wcols))).astype(BF16)
    s = _qk(jnp.concatenate([qrows, tc_ref[...]], axis=1), jnp.concatenate([kc_ref[...], place], axis=1))
    m = jnp.max(s, axis=-1, keepdims=True)
    p = jnp.exp2(s - m)
    r1 = jnp.dot(p.astype(BF16), jnp.concatenate([_with_ones(vc_ref[...]), ov_ref[...]], axis=1),
                 preferred_element_type=F32)
    has_key = m > 0.5 * NEG_INF
    inv_l = jnp.where(has_key, 1.0 / r1[:, LANES:2 * LANES], 0.0)
    o_cmp = r1[:, 0:LANES] * inv_l
    imp_h = r1[:, 2 * LANES:3 * LANES] * inv_l
    for h in range(nh):
        sl = slice(h * tq, (h + 1) * tq)
        out_ref[sl, :] = gate_col(h, 0) * o_cmp[sl]

    cur = (rowi + t0) // SEL_BLOCK
    forced = (lane == 0) | (lane == cur) | (lane == cur - 1)
    far_blocks = (c - 1) * (tq // SEL_BLOCK)
    for g in range(NSA_KV_GROUPS):
        imp = imp_h[g * NSA_REP * tq:(g * NSA_REP + 1) * tq]
        for r in range(1, NSA_REP):
            imp = imp + imp_h[(g * NSA_REP + r) * tq:(g * NSA_REP + r + 1) * tq]
        imp = jnp.where(forced, TOP_BONUS, jnp.where(lane <= cur, imp, NEG_INF))
        sel = _topk_mask(imp, SEL_TOPK)
        nm_ref[g] = jnp.where(sel, 0.0, NEG_INF).astype(BF16)
        far_mask = jnp.where(sel & (lane < far_blocks), 0.0, NEG_INF).astype(BF16)
        for r in range(NSA_REP):
            h = g * NSA_REP + r
            lhs_ref[h * tq:(h + 1) * tq, LANES:2 * LANES] = far_mask

    wpat = jnp.where(lax.broadcasted_iota(jnp.int32, (rows, LANES), 1) == 1, NEG_INF, 0.0).astype(BF16)
    back2_start = pl.multiple_of(jnp.maximum(c - 2, 0) * tq, tq)
    kwin = jnp.concatenate([kw_ref[pl.ds(back2_start, tq), :], kw_ref[pl.ds(prev_start, tq), :],
                            kw_ref[pl.ds(diag_start, tq), :]], axis=0)
    vwin = jnp.concatenate([vw_ref[pl.ds(back2_start, tq), :], vw_ref[pl.ds(prev_start, tq), :],
                            vw_ref[pl.ds(diag_start, tq), :]], axis=0)
    hot = jnp.concatenate([_lane_onehot(tq, jnp.where(c >= 2, 0, 1)), _lane_onehot(tq, jnp.where(c >= 1, 0, 1)),
                           _lane_onehot(tq, 0)], axis=0)
    s = _qk(jnp.concatenate([qrows, wpat], axis=1), jnp.concatenate([kwin, hot], axis=1))
    s = s + jnp.concatenate([jnp.tile(wt_ref[...], (nh, 1)), dp_ref[...], dd_ref[...]], axis=1)
    o_win = _softmax_values(s, _with_ones(vwin))
    for h in range(nh):
        sl = slice(h * tq, (h + 1) * tq)
        out_ref[sl, :] = out_ref[sl, :] + gate_col(h, 2) * o_win[sl]

    def write_near_mask():
        for h in range(nh):
            lhs_ref[h * tq:(h + 1) * tq, LANES:2 * LANES] = nm_ref[h // NSA_REP]

    o_sel = _pipelined_attention(c, SEL_BLOCK, 1, lhs_ref, write_near_mask, ks_ref, vs_ref, dp_ref, dd_ref,
                                 (s0_ref, s1_ref), (p0_ref, p1_ref), (a0_ref, a1_ref), m_ref, acc_ref,
                                 (t0_ref, t1_ref))

    for r in range(NSA_REP):
        h0, h1 = r, NSA_REP + r
        o0 = out_ref[h0 * tq:(h0 + 1) * tq, :] + gate_col(h0, 1) * o_sel[h0 * tq:(h0 + 1) * tq]
        o1 = out_ref[h1 * tq:(h1 + 1) * tq, :] + gate_col(h1, 1) * o_sel[h1 * tq:(h1 + 1) * tq]
        o_ref[:, r * LANES:(r + 1) * LANES] = jnp.where(half == 0, o0, o1).astype(o_ref.dtype)


def _nsa_attention(proj, kcmp, vcmp, overlap, tcmp, dprev, ddiag, wtab):
    b, s, _ = proj.shape
    tq = ATT_TILE
    ncp = kcmp.shape[1]
    rows = NSA_HEADS * tq

    def col(name, width):
        return SEC[name][0] // width

    full = lambda name: _const_spec((None, s, LANES), lambda i, c, n=name: (i, 0, col(n, LANES)))
    return pl.pallas_call(
        _nsa_kernel,
        grid=(b, s // tq),
        in_specs=[pl.BlockSpec((None, tq, 512), lambda i, c: (i, c, col("a_q", 512))),
                  full("a_ks"), full("a_vs"), full("a_kw"), full("a_vw"),
                  pl.BlockSpec((None, tq, LANES), lambda i, c: (i, c, col("a_g", LANES))),
                  _const_spec((None, ncp, LANES), lambda i, c: (i, 0, 0)),
                  _const_spec((None, ncp, LANES), lambda i, c: (i, 0, 0)),
                  _const_spec((ncp, LANES), lambda i, c: (0, 0)),
                  _const_spec((rows, LANES), lambda i, c: (0, 0)),
                  _const_spec((rows, tq), lambda i, c: (0, 0)),
                  _const_spec((rows, tq), lambda i, c: (0, 0)),
                  _const_spec((tq, tq), lambda i, c: (0, 0))],
        out_specs=pl.BlockSpec((None, tq, 512), lambda i, c: (i, c, 0)),
        out_shape=jax.ShapeDtypeStruct((b, s, 512), BF16),
        scratch_shapes=[pltpu.VMEM((rows, 2 * LANES), BF16),
                        pltpu.VMEM((NSA_KV_GROUPS, tq, LANES), BF16),
                        pltpu.VMEM((rows, LANES), F32), pltpu.VMEM((rows, 2 * LANES), F32),
                        pltpu.VMEM((rows, LANES), F32),
                        pltpu.VMEM((rows, FAR_TILE), F32), pltpu.VMEM((rows, FAR_TILE), F32),
                        pltpu.VMEM((rows, FAR_TILE), BF16), pltpu.VMEM((rows, FAR_TILE), BF16),
                        pltpu.VMEM((rows, LANES), F32), pltpu.VMEM((rows, LANES), F32),
                        pltpu.VMEM((rows, LANES), F32), pltpu.VMEM((rows, LANES), F32)],
        compiler_params=_cparams(("parallel", "arbitrary")),
        name="nsa_attention",
    )(proj, proj, proj, proj, proj, proj, kcmp, vcmp, overlap, tcmp, dprev, ddiag, wtab)


def _gla_kernel(q_ref, k_ref, v_ref, r_ref, lr_ref, w2_ref, gb_ref, gn_ref, o_ref, st_ref, oi_ref, bc_ref):
    ch = GLA_CHUNK
    lb = q_ref.shape[0]
    nch = lb // ch

    @pl.when(pl.program_id(1) == 0)
    def _():
        st_ref[...] = jnp.zeros(st_ref.shape, F32)

    x = jnp.dot(lr_ref[...], w2_ref[...], preferred_element_type=F32) + gb_ref[...]
    log_a = (jnp.minimum(x, 0.0) - jnp.log1p(jnp.exp(-jnp.abs(x)))) / GLA_GATE_TAU

    ti = lax.broadcasted_iota(jnp.int32, (lb, lb), 0)
    tj = lax.broadcasted_iota(jnp.int32, (lb, lb), 1)
    same_chunk = (ti // ch) == (tj // ch)
    causal = same_chunk & (tj <= ti)
    g_hi, g_mid, g_lo = _split3(log_a)

    def chunk_sums(w):
        return (jnp.dot(w, g_hi, preferred_element_type=F32) + jnp.dot(w, g_mid, preferred_element_type=F32)
                + jnp.dot(w, g_lo, preferred_element_type=F32))

    bcum = chunk_sums(causal.astype(BF16))
    btot = chunk_sums(same_chunk.astype(BF16))
    half = lax.broadcasted_iota(jnp.int32, (lb, LANES), 1) // GLA_KEY_DIM
    scale = GLA_KEY_DIM ** -0.5
    gn = gn_ref[...]

    def finish(o, hc):
        rg = r_ref[:, hc].astype(F32)
        o_ref[:, hc] = (_rms(o, gn) * (rg * jax.nn.sigmoid(rg))).astype(o_ref.dtype)

    for p in range(GLA_HEADS // 2):
        cols = slice(p * LANES, (p + 1) * LANES)
        bc = bcum[:, cols]
        bt = btot[:, cols]
        kf = k_ref[:, cols].astype(F32)
        qe = q_ref[:, cols].astype(F32) * scale * jnp.exp(bc)
        kinv = (kf * jnp.exp(-bc)).astype(BF16)
        klast = (kf * jnp.exp(bt - bc)).astype(BF16)
        decay = jnp.exp(bt)
        for a in range(2):
            h = 2 * p + a
            hc = slice(h * LANES, (h + 1) * LANES)
            qa = jnp.where(half == a, qe, 0.0).astype(BF16)
            v = v_ref[:, hc]
            attn = jnp.where(causal, _qk(qa, kinv), 0.0).astype(BF16)
            o_intra = jnp.dot(attn, v, preferred_element_type=F32)
            outs = []
            st = st_ref[h]
            for cc in range(nch):
                sl = slice(cc * ch, (cc + 1) * ch)
                outs.append(_qk(qa[sl], st.astype(BF16)))
                upd = lax.dot_general(v[sl], klast[sl], (((0,), (0,)), ((), ())), preferred_element_type=F32)
                st = st * decay[cc * ch:cc * ch + 1, :] + upd
            st_ref[h] = st
            o_inter = jnp.concatenate(outs, axis=0)
            oi_ref[:, hc] = o_inter
            finish(o_inter + o_intra, hc)

    @pl.when(jnp.min(btot) < -GLA_FACTORISED_RANGE)
    def _():
        bc_ref[...] = bcum
        rowc = lax.broadcasted_iota(jnp.int32, (ch, LANES), 0)
        lanec = lax.broadcasted_iota(jnp.int32, (ch, LANES), 1)

        def chunk_step(cc, carry):
            sl = pl.ds(pl.multiple_of(cc * ch, ch), ch)
            for p in range(GLA_HEADS // 2):
                cols = slice(p * LANES, (p + 1) * LANES)
                qc = q_ref[sl, cols].astype(F32) * scale
                kc = k_ref[sl, cols].astype(F32)
                bcc = bc_ref[sl, cols]
                v0 = v_ref[sl, 2 * p * LANES:(2 * p + 1) * LANES].astype(F32)
                v1 = v_ref[sl, (2 * p + 1) * LANES:(2 * p + 2) * LANES].astype(F32)
                acc0 = jnp.zeros((ch, LANES), F32)
                acc1 = jnp.zeros((ch, LANES), F32)
                for j in range(ch):
                    w = jnp.where(rowc >= j, jnp.exp(jnp.minimum(bcc - bcc[j:j + 1], 0.0)), 0.0)
                    t = qc * (kc[j:j + 1] * w)
                    a0 = jnp.sum(jnp.where(lanec < GLA_KEY_DIM, t, 0.0), axis=-1, keepdims=True)
                    a1 = jnp.sum(jnp.where(lanec >= GLA_KEY_DIM, t, 0.0), axis=-1, keepdims=True)
                    acc0 = acc0 + a0 * jnp.broadcast_to(v0[j:j + 1], (ch, LANES))
                    acc1 = acc1 + a1 * jnp.broadcast_to(v1[j:j + 1], (ch, LANES))
                oi_ref[sl, 2 * p * LANES:(2 * p + 1) * LANES] += acc0
                oi_ref[sl, (2 * p + 1) * LANES:(2 * p + 2) * LANES] += acc1
            return carry

        lax.fori_loop(0, nch, chunk_step, 0)
        for h in range(GLA_HEADS):
            hc = slice(h * LANES, (h + 1) * LANES)
            finish(oi_ref[:, hc], hc)


def _split3(x):
    hi = x.astype(BF16)
    r1 = x - hi.astype(F32)
    mid = r1.astype(BF16)
    lo = (r1 - mid.astype(F32)).astype(BF16)
    return hi, mid, lo


def _gla(proj, w2, gb, gn, lb):
    b, s, _ = proj.shape

    def spec(name, width):
        return pl.BlockSpec((None, lb, width), lambda i, c, n=name, w=width: (i, c, SEC[n][0] // w))

    return pl.pallas_call(
        _gla_kernel,
        grid=(b, s // lb),
        in_specs=[spec("b_q", 256), spec("b_k", 256), spec("b_v", 512), spec("b_r", 512), spec("b_lr", LANES),
                  pl.BlockSpec((LANES, 256), lambda i, c: (0, 0)),
                  pl.BlockSpec((1, 256), lambda i, c: (0, 0)),
                  pl.BlockSpec((1, LANES), lambda i, c: (0, 0))],
        out_specs=pl.BlockSpec((None, lb, 512), lambda i, c: (i, c, 0)),
        out_shape=jax.ShapeDtypeStruct((b, s, 512), BF16),
        scratch_shapes=[pltpu.VMEM((GLA_HEADS, LANES, LANES), F32), pltpu.VMEM((lb, 512), F32),
                        pltpu.VMEM((lb, 256), F32)],
        compiler_params=_cparams(("parallel", "arbitrary")),
        name="gla",
    )(proj, proj, proj, proj, proj, w2, gb, gn)


def _kmean_kernel(a_ref, k_ref, o_ref):
    o_ref[...] = jnp.dot(a_ref[...], k_ref[...], preferred_element_type=F32).astype(o_ref.dtype)


def _moba_kmean(proj, avg):
    b, s, _ = proj.shape
    return pl.pallas_call(
        _kmean_kernel,
        grid=(b,),
        in_specs=[pl.BlockSpec((LANES, s), lambda i: (0, 0)),
                  pl.BlockSpec((None, s, 512), lambda i: (i, 0, SEC["c_k"][0] // 512))],
        out_specs=pl.BlockSpec((None, LANES, 512), lambda i: (i, 0, 0)),
        out_shape=jax.ShapeDtypeStruct((b, LANES, 512), BF16),
        compiler_params=_cparams(("parallel",)),
        name="moba_kmean",
    )(avg, proj)


def _moba_kernel(q_ref, k_ref, v_ref, km_ref, dp_ref, dd_ref, o_ref, lhs_ref, m_ref, acc_ref,
                 s0_ref, s1_ref, p0_ref, p1_ref, a0_ref, a1_ref):
    tq = ATT_TILE
    nh = MOBA_HEADS
    npair = nh // 2
    rows = nh * tq
    c = pl.program_id(1)

    lane = lax.broadcasted_iota(jnp.int32, (tq, LANES), 1)
    half = lane // MOBA_HEAD_DIM
    for h in range(nh):
        qb = q_ref[:, (h // 2) * LANES:(h // 2 + 1) * LANES]
        lhs_ref[h * tq:(h + 1) * tq, 0:LANES] = jnp.where(half == h % 2, qb, jnp.zeros_like(qb))

    nblk = k_ref.shape[0] // MOBA_BLOCK
    score = jnp.concatenate(
        [_qk(km_ref[0:nblk, p * LANES:(p + 1) * LANES], lhs_ref[2 * p * tq:2 * (p + 1) * tq, 0:LANES])
         for p in range(npair)], axis=1)
    blk = lax.broadcasted_iota(jnp.int32, (nblk, rows), 0)
    chosen = _topk_mask_rows(jnp.where(blk < c, score, NEG_INF), MOBA_TOPK) & (blk < c)
    chosen = jnp.concatenate([jnp.where(chosen, 1.0, 0.0), jnp.zeros((LANES - nblk, rows), F32)], axis=0)
    past = jnp.transpose(chosen) > 0.5
    lane2 = lax.broadcasted_iota(jnp.int32, (rows, LANES), 1)
    lhs_ref[:, LANES:2 * LANES] = jnp.where(past & (lane2 < c - 1), 0.0, NEG_INF).astype(BF16)

    def write_near_mask():
        lhs_ref[:, LANES:2 * LANES] = jnp.where(past | (lane2 == c), 0.0, NEG_INF).astype(BF16)

    o = _pipelined_attention(c, MOBA_BLOCK, npair, lhs_ref, write_near_mask, k_ref, v_ref, dp_ref, dd_ref,
                             (s0_ref, s1_ref), (p0_ref, p1_ref), (a0_ref, a1_ref), m_ref, acc_ref)
    for p in range(npair):
        o0 = o[2 * p * tq:(2 * p + 1) * tq]
        o1 = o[(2 * p + 1) * tq:(2 * p + 2) * tq]
        o_ref[:, p * LANES:(p + 1) * LANES] = jnp.where(half == 0, o0, o1).astype(o_ref.dtype)


def _moba_attention(proj, kmean, dprev, ddiag):
    b, s, _ = proj.shape
    tq = ATT_TILE
    rows = MOBA_HEADS * tq

    def col(name):
        return SEC[name][0] // 512

    return pl.pallas_call(
        _moba_kernel,
        grid=(b, s // tq),
        in_specs=[pl.BlockSpec((None, tq, 512), lambda i, c: (i, c, col("c_q"))),
                  _const_spec((None, s, 512), lambda i, c: (i, 0, col("c_k"))),
                  _const_spec((None, s, 512), lambda i, c: (i, 0, col("c_v"))),
                  _const_spec((None, LANES, 512), lambda i, c: (i, 0, 0)),
                  _const_spec((rows, tq), lambda i, c: (0, 0)),
                  _const_spec((rows, tq), lambda i, c: (0, 0))],
        out_specs=pl.BlockSpec((None, tq, 512), lambda i, c: (i, c, 0)),
        out_shape=jax.ShapeDtypeStruct((b, s, 512), BF16),
        scratch_shapes=[pltpu.VMEM((rows, 2 * LANES), BF16),
                        pltpu.VMEM((rows, LANES), F32), pltpu.VMEM((rows, 2 * LANES), F32),
                        pltpu.VMEM((rows, FAR_TILE), F32), pltpu.VMEM((rows, FAR_TILE), F32),
                        pltpu.VMEM((rows, FAR_TILE), BF16), pltpu.VMEM((rows, FAR_TILE), BF16),
                        pltpu.VMEM((rows, LANES), F32), pltpu.VMEM((rows, LANES), F32)],
        compiler_params=_cparams(("parallel", "arbitrary")),
        name="moba_attention",
    )(proj, proj, proj, kmean, dprev, ddiag)


def _merge_kernel(oa_ref, ob_ref, oc_ref, g0_ref, g1_ref, g2_ref, x_ref, wa_ref, wb_ref, wc_ref, wo_ref, nw_ref,
                  o_ref):
    def branch(o, w, g):
        return jax.nn.sigmoid(g[...].astype(F32)) * jnp.dot(o[...], w[...], preferred_element_type=F32)

    merged = branch(oa_ref, wa_ref, g0_ref) + branch(ob_ref, wb_ref, g1_ref) + branch(oc_ref, wc_ref, g2_ref)
    y = jnp.dot(merged.astype(BF16), wo_ref[...], preferred_element_type=F32)
    o_ref[...] = x_ref[...] + _rms(y, nw_ref[...])


def _merge(oa, ob, oc, proj, x, wa, wb, wc, wo, nw, tm):
    t, d = x.shape
    w = oa.shape[1]
    row = lambda width, j=0: pl.BlockSpec((tm, width), lambda i, j=j: (i, j))
    const = lambda shape: pl.BlockSpec(shape, lambda i: (0, 0))
    return pl.pallas_call(
        _merge_kernel,
        grid=(t // tm,),
        in_specs=[row(w), row(w), row(w), row(d, 0), row(d, 1), row(d, 2), row(d),
                  const((w, d)), const((w, d)), const((w, d)), const((d, d)), const((1, d))],
        out_specs=row(d),
        out_shape=jax.ShapeDtypeStruct((t, d), F32),
        compiler_params=_cparams(("parallel",)),
        name="merge_out",
    )(oa, ob, oc, proj, proj, proj, x, wa, wb, wc, wo, nw.reshape(1, d))


def _ffn_kernel(x_ref, npre_ref, wg_ref, wu_ref, wo_ref, npost_ref, o_ref):
    h = _rms(x_ref[...], npre_ref[...]).astype(BF16)
    hid = wg_ref.shape[1]
    acc = jnp.zeros(o_ref.shape, F32)
    for lo in range(0, hid, FFN_CHUNK):
        hi = min(lo + FFN_CHUNK, hid)
        gate = jnp.dot(h, wg_ref[:, lo:hi], preferred_element_type=F32)
        up = jnp.dot(h, wu_ref[:, lo:hi], preferred_element_type=F32)
        act = (gate * jax.nn.sigmoid(gate) * up).astype(BF16)
        acc = acc + jnp.dot(act, wo_ref[lo:hi, :], preferred_element_type=F32)
    o_ref[...] = x_ref[...] + _rms(acc, npost_ref[...])


def _ffn(x, npre, wg, wu, wo, npost, tm):
    t, d = x.shape
    hid = wg.shape[1]
    return pl.pallas_call(
        _ffn_kernel,
        grid=(t // tm,),
        in_specs=[pl.BlockSpec((tm, d), lambda i: (i, 0)),
                  pl.BlockSpec((1, d), lambda i: (0, 0)),
                  _const_spec((d, hid), lambda i: (0, 0)),
                  _const_spec((d, hid), lambda i: (0, 0)),
                  _const_spec((hid, d), lambda i: (0, 0)),
                  pl.BlockSpec((1, d), lambda i: (0, 0))],
        out_specs=pl.BlockSpec((tm, d), lambda i: (i, 0)),
        out_shape=jax.ShapeDtypeStruct((t, d), F32),
        compiler_params=_cparams(("parallel",)),
        name="ffn",
    )(x, npre.reshape(1, d), wg, wu, wo, npost.reshape(1, d))


def _t5_bucket_table(dist):
    n = np.maximum(dist, 0)
    max_exact = NUM_BUCKETS // 2
    log_ratio = np.log(np.maximum(n, 1).astype(np.float32) / max_exact) / math.log(T5_MAX_DISTANCE / max_exact)
    large = max_exact + (log_ratio * (NUM_BUCKETS - max_exact)).astype(np.int32)
    return np.where(n < max_exact, n, np.minimum(large, NUM_BUCKETS - 1)).astype(np.int32)


def _bias_lookup(rel, dist):
    onehot = (jnp.asarray(_t5_bucket_table(dist))[..., None] == jnp.arange(NUM_BUCKETS)).astype(F32)
    return jnp.dot(onehot, (rel - rel[NUM_BUCKETS - 1]) * LOG2E, precision=lax.Precision.HIGHEST)


def _near_bias(rel, t):
    qi = np.arange(t)[:, None]
    ki = np.arange(t)[None, :]
    prev = _bias_lookup(rel, qi + t - ki)
    diag = jnp.where((qi >= ki)[..., None], _bias_lookup(rel, qi - ki), NEG_INF)
    flat = lambda b: b.transpose(2, 0, 1).reshape(-1, t)
    return flat(prev), flat(diag)


def _cmp_bias(rel, t):
    w = t // CMP_STRIDE
    qi = np.arange(t)[:, None]
    j = np.arange(LANES)[None, :]
    d = qi - CMP_STRIDE * (j - w) - (CMP_LEN - 1)
    b = jnp.where(((j < 2 * w) & (d >= 0))[..., None], _bias_lookup(rel, d),
                  jnp.asarray(np.where(j <= 2 * w, NEG_INF, 0.0) * np.ones_like(d), F32)[..., None])
    return b.transpose(2, 0, 1).reshape(-1, LANES).astype(BF16)


def _overlap(s, nc):
    n_cmp = (s - CMP_LEN) // CMP_STRIDE + 1
    cmp_end = np.arange(nc) * CMP_STRIDE + CMP_LEN - 1
    cmp_start = cmp_end - (CMP_LEN - 1)
    sb_start = np.arange(LANES) * SEL_BLOCK
    ov = (cmp_start[:, None] < sb_start[None, :] + SEL_BLOCK) & (cmp_end[:, None] >= sb_start[None, :])
    ov = ov & (np.arange(nc)[:, None] < n_cmp) & (np.arange(LANES)[None, :] < s // SEL_BLOCK)
    return jnp.asarray(ov, BF16)


def _pair_diag(w):
    z = jnp.zeros_like(w)
    return jnp.concatenate([jnp.concatenate([w, z], axis=-1), jnp.concatenate([z, w], axis=-1)], axis=-2)


def kernel(x, rel_bias, norm_mix_pre, norm_mix_post, norm_ffn_pre, norm_ffn_post, w_in, nsa_pe_k, nsa_pe_v, nsa_cmp_k_w1, nsa_cmp_k_w2, nsa_cmp_v_w1, nsa_cmp_v_w2, gla_gate_w2, gla_gate_b, gla_norm, w_branch_a, w_branch_b, w_branch_c, w_out, w_ffn_in, w_ffn_out):
    b, s, d = x.shape
    depth = w_in.shape[0]
    t = b * s
    dk = NSA_HEAD_DIM
    tq = ATT_TILE
    assert d == D_MODEL and w_in.shape[2] == D_IN
    assert WINDOW == 2 * tq and MOBA_BLOCK == tq and s % tq == 0
    assert SEL_TOPK <= s // SEL_BLOCK <= LANES and s // MOBA_BLOCK <= LANES
    nc = s // CMP_STRIDE
    assert nc % LANES == 0

    w_in_p = _layout_w_in(w_in)
    rel_a = rel_bias[:, :NSA_HEADS]
    rel_c = rel_bias[:, NSA_HEADS:]
    dprev_a, ddiag_a = _near_bias(rel_a, tq)
    dprev_c, ddiag_c = _near_bias(rel_c, tq)
    tcmp = _cmp_bias(rel_a, tq)
    overlap = _overlap(s, nc)
    wtab = jnp.asarray(np.where(np.arange(tq)[None, :] > np.arange(tq)[:, None], 0.0, NEG_INF), F32)
    avg = jnp.asarray((np.arange(LANES)[:, None] == np.arange(s)[None, :] // MOBA_BLOCK) / MOBA_BLOCK, BF16)

    def cmp_w1(w1):
        w = _pair_diag(w1.reshape(depth, 2, CMP_STRIDE, dk, dk))
        return w.reshape(depth, 2, CMP_STRIDE * 2 * dk, 2 * dk)

    def cmp_pe(pe):
        p2 = jnp.concatenate([pe, pe], axis=-1).reshape(depth, 2, 1, CMP_STRIDE * 2 * dk)
        return jnp.broadcast_to(p2, (depth, 2, 8, CMP_STRIDE * 2 * dk))

    cw1 = jnp.stack([cmp_w1(nsa_cmp_k_w1), cmp_w1(nsa_cmp_v_w1)], axis=1).astype(BF16)
    cw2 = jnp.stack([_pair_diag(nsa_cmp_k_w2), _pair_diag(nsa_cmp_v_w2)], axis=1).astype(BF16)
    cpe = jnp.stack([cmp_pe(nsa_pe_k), cmp_pe(nsa_pe_v)], axis=1).astype(BF16)

    gw2 = jnp.concatenate([gla_gate_w2, jnp.zeros((depth, LANES - GLA_GATE_RANK, gla_gate_w2.shape[2]), F32)],
                          axis=1).astype(BF16)
    wa = (w_branch_a.reshape(depth, NSA_KV_GROUPS, NSA_REP, dk, d).transpose(0, 2, 1, 3, 4)
          .reshape(depth, NSA_HEADS * dk, d).astype(BF16))
    wb = w_branch_b.astype(BF16)
    wc = w_branch_c.astype(BF16)
    wo = w_out.astype(BF16)
    hid = w_ffn_out.shape[1]
    wg = w_ffn_in[:, :, :hid].astype(BF16)
    wu = w_ffn_in[:, :, hid:].astype(BF16)
    wf = w_ffn_out.astype(BF16)

    tm = min(1024, t)
    tn = NP_COLS // 5
    xf = x.reshape(t, d)
    for layer in range(depth):
        proj = _norm_matmul(xf, norm_mix_pre[layer], w_in_p[layer], tm, tn).reshape(b, s, NP_COLS)
        kc0, vc0 = SEC["a_kc"][0], SEC["a_vc"][0]
        xkv = jnp.stack([proj[:, :, kc0:kc0 + LANES], proj[:, :, vc0:vc0 + LANES]])
        xkv = xkv.reshape(2, b, nc, CMP_STRIDE * LANES)
        cmp = _compress(xkv, cpe[layer], cw1[layer], cw2[layer])
        o_a = _nsa_attention(proj, cmp[0], cmp[1], overlap, tcmp, dprev_a, ddiag_a, wtab)
        o_b = _gla(proj, gw2[layer], gla_gate_b[layer].reshape(1, -1), gla_norm[layer].reshape(1, -1),
                   min(256, s))
        kmean = _moba_kmean(proj, avg)
        o_c = _moba_attention(proj, kmean, dprev_c, ddiag_c)
        xf = _merge(o_a.reshape(t, -1), o_b.reshape(t, -1), o_c.reshape(t, -1), proj.reshape(t, NP_COLS), xf,
                    wa[layer], wb[layer], wc[layer], wo[layer], norm_mix_post[layer], min(512, t))
        xf = _ffn(xf, norm_ffn_pre[layer], wg[layer], wu[layer], wf[layer], norm_ffn_post[layer], min(512, t))
    return xf.reshape(b, s, d)
```

```python
import math

import numpy as np
import jax
import jax.numpy as jnp
from jax import lax
from jax.experimental import pallas as pl
from jax.experimental.pallas import tpu as pltpu

F32 = jnp.float32
BF16 = jnp.bfloat16

NORM_EPS = 1e-6
NEG_INF = -1e30
TOP_BONUS = 1e9
NUM_BUCKETS = 32
T5_MAX_DISTANCE = 128
NSA_HEADS = 8
NSA_KV_GROUPS = 2
NSA_REP = NSA_HEADS // NSA_KV_GROUPS
NSA_HEAD_DIM = 64
CMP_LEN = 32
CMP_STRIDE = 16
SEL_BLOCK = 64
SEL_TOPK = 16
WINDOW = 512
N_NSA_BRANCH = 3
GLA_HEADS = 4
GLA_KEY_DIM = 64
GLA_VAL_DIM = 128
GLA_GATE_RANK = 16
GLA_GATE_TAU = 16.0
GLA_CHUNK = 64
GLA_FACTORISED_RANGE = 80.0
MOBA_HEADS = 8
MOBA_HEAD_DIM = 64
MOBA_BLOCK = 256
MOBA_TOPK = 3
N_BRANCHES = 3
LOG2E = math.log2(math.e)

LANES = 128
ATT_TILE = 256
FAR_TILE = 2 * ATT_TILE
FFN_CHUNK = 512
PROJ_CHUNK = 1536
VMEM_LIMIT = 56 * 1024 * 1024

D_MODEL = 1024
SEC = {}
_off = 0
for _name, _w in (("mg", 3 * D_MODEL), ("a_q", 512), ("c_q", 512), ("c_k", 512), ("c_v", 512),
                  ("b_v", 512), ("b_r", 512), ("b_q", 256), ("b_k", 256),
                  ("a_kc", 128), ("a_vc", 128), ("a_ks", 128), ("a_vs", 128), ("a_kw", 128), ("a_vw", 128),
                  ("a_g", 128), ("b_lr", 128)):
    SEC[_name] = (_off, _w)
    _off += _w
NP_COLS = _off

_SRC_NAMES = ("a_q", "a_kc", "a_vc", "a_ks", "a_vs", "a_kw", "a_vw", "a_g", "b_q", "b_k", "b_v", "b_r", "b_lr",
              "c_q", "c_k", "c_v", "mg")
_SRC_SIZES = (512, 128, 128, 128, 128, 128, 128, 24, 256, 256, 512, 512, 16, 512, 512, 512, 3 * D_MODEL)
_SRC_OFF = dict(zip(_SRC_NAMES, np.cumsum((0,) + _SRC_SIZES[:-1]).tolist()))
_SRC_W = dict(zip(_SRC_NAMES, _SRC_SIZES))
D_IN = int(sum(_SRC_SIZES))


def _layout_w_in(w_in):
    depth, d, _ = w_in.shape

    def src(name, lo=0, hi=None):
        hi = _SRC_W[name] if hi is None else hi
        return w_in[:, :, _SRC_OFF[name] + lo:_SRC_OFF[name] + hi]

    parts = []
    for name, (_, width) in SEC.items():
        if name == "a_q":
            for r in range(NSA_REP):
                for g in range(NSA_KV_GROUPS):
                    h = g * NSA_REP + r
                    parts.append(src(name, h * NSA_HEAD_DIM, (h + 1) * NSA_HEAD_DIM) * (NSA_HEAD_DIM ** -0.5 * LOG2E))
        elif name == "c_q":
            parts.append(src(name) * (MOBA_HEAD_DIM ** -0.5 * LOG2E))
        else:
            parts.append(src(name))
            if _SRC_W[name] < width:
                parts.append(jnp.zeros((depth, d, width - _SRC_W[name]), w_in.dtype))
    return jnp.concatenate(parts, axis=2).astype(BF16)


def _cparams(sem):
    return pltpu.CompilerParams(dimension_semantics=sem, vmem_limit_bytes=VMEM_LIMIT)


def _const_spec(shape, index_map):
    return pl.BlockSpec(shape, index_map, pipeline_mode=pl.Buffered(1))


def _rms(y, w):
    return y * lax.rsqrt(jnp.mean(y * y, axis=-1, keepdims=True) + NORM_EPS) * w


def _norm_matmul_kernel(x_ref, nw_ref, w_ref, o_ref):
    h = _rms(x_ref[...], nw_ref[...]).astype(BF16)
    for lo in range(0, w_ref.shape[1], PROJ_CHUNK):
        hi = lo + PROJ_CHUNK
        o_ref[:, lo:hi] = jnp.dot(h, w_ref[:, lo:hi], preferred_element_type=F32).astype(o_ref.dtype)


def _norm_matmul(x, nw, w, tm):
    t, d = x.shape
    n = w.shape[1]
    assert n % PROJ_CHUNK == 0
    return pl.pallas_call(
        _norm_matmul_kernel,
        grid=(t // tm,),
        in_specs=[pl.BlockSpec((tm, d), lambda i: (i, 0)),
                  pl.BlockSpec((1, d), lambda i: (0, 0)),
                  _const_spec((d, n), lambda i: (0, 0))],
        out_specs=pl.BlockSpec((tm, n), lambda i: (i, 0)),
        out_shape=jax.ShapeDtypeStruct((t, n), BF16),
        compiler_params=_cparams(("parallel",)),
        name="norm_proj",
    )(x, nw.reshape(1, d), w)


def _compress_kernel(x_ref, pe_ref, w1_ref, w2_ref, o_ref):
    x = x_ref[...]
    nc = x.shape[0]
    w1t = w1_ref[0]
    w1b = w1_ref[1]
    a = jnp.dot(x, w1t, preferred_element_type=F32)
    b = jnp.dot(x, w1b, preferred_element_type=F32)
    pe = pe_ref[...]
    pe_term = (jnp.dot(pe[0], w1t, preferred_element_type=F32)
               + jnp.dot(pe[1], w1b, preferred_element_type=F32))[0:1]
    pre = a + pltpu.roll(b, nc - 1, 0) + pe_term
    hid = jax.nn.gelu(pre)
    o_ref[...] = jnp.dot(hid.astype(BF16), w2_ref[...], preferred_element_type=F32).astype(o_ref.dtype)


def _compress(xkv, pe, w1, w2):
    _, b, nc, kw = xkv.shape
    return pl.pallas_call(
        _compress_kernel,
        grid=(2, b),
        in_specs=[pl.BlockSpec((None, None, nc, kw), lambda s, i: (s, i, 0, 0)),
                  pl.BlockSpec((None, 2, 8, kw), lambda s, i: (s, 0, 0, 0)),
                  pl.BlockSpec((None, 2, kw, LANES), lambda s, i: (s, 0, 0, 0)),
                  pl.BlockSpec((None, LANES, LANES), lambda s, i: (s, 0, 0))],
        out_specs=pl.BlockSpec((None, None, nc, LANES), lambda s, i: (s, i, 0, 0)),
        out_shape=jax.ShapeDtypeStruct((2, b, nc, LANES), BF16),
        compiler_params=_cparams(("parallel", "parallel")),
        name="nsa_compress",
    )(xkv, pe, w1, w2)


def _with_ones(v):
    return jnp.concatenate([v, jnp.ones(v.shape, v.dtype)], axis=1)


def _softmax_values(s, v1):
    p = jnp.exp2(s - jnp.max(s, axis=-1, keepdims=True)).astype(BF16)
    r = jnp.dot(p, v1, preferred_element_type=F32)
    return r[:, 0:LANES] / r[:, LANES:2 * LANES]


def _qk(lhs, rhs):
    return lax.dot_general(lhs, rhs, (((1,), (1,)), ((), ())), preferred_element_type=F32)


def _lane_onehot(width, lane_idx):
    klane = lax.broadcasted_iota(jnp.int32, (width, LANES), 1)
    return (klane == lane_idx).astype(BF16)


def _pipelined_attention(c, block, groups, lhs_ref, write_near_mask, k_ref, v_ref, dp_ref, dd_ref,
                         s_refs, p_refs, a_refs, m_ref, acc_ref, t_refs=None):
    tq = ATT_TILE
    tk = FAR_TILE
    rows = lhs_ref.shape[0]
    grows = rows // groups
    j_max = k_ref.shape[0] // tk - 1
    per = tk // block
    n_far = (jnp.maximum(c - 1, 0) * tq + tk - 1) // tk
    n_loop = 2 * ((n_far + 1) // 2)
    krow = lax.broadcasted_iota(jnp.int32, (tk, LANES), 0) // block
    klane = lax.broadcasted_iota(jnp.int32, (tk, LANES), 1)

    m_ref[...] = jnp.full(m_ref.shape, NEG_INF, F32)
    acc_ref[...] = jnp.zeros(acc_ref.shape, F32)
    s_refs[1][...] = jnp.full((rows, tk), -jnp.inf, F32)
    if t_refs is not None:
        t_refs[1][...] = jnp.full((rows, LANES), -jnp.inf, F32)
    p_refs[0][...] = jnp.zeros((rows, tk), BF16)
    a_refs[0][...] = jnp.ones((rows, LANES), F32)

    def far_rows(j):
        start = pl.multiple_of(jnp.clip(j, 0, j_max) * tk, tk)
        return lambda ref, g: ref[pl.ds(start, tk), g * LANES:(g + 1) * LANES]

    def near_rows(ref, g):
        cols = slice(g * LANES, (g + 1) * LANES)
        prev = ref[pl.ds(pl.multiple_of(jnp.maximum(c - 1, 0) * tq, tq), tq), cols]
        return jnp.concatenate([prev, ref[pl.ds(pl.multiple_of(c * tq, tq), tq), cols]], axis=0)

    def value_stage(slot, window):
        for g in range(groups):
            gr = slice(g * grows, (g + 1) * grows)
            acc_ref[gr, :] = (jnp.tile(a_refs[slot][gr, :], (1, 2)) * acc_ref[gr, :]
                              + jnp.dot(p_refs[slot][gr, :], _with_ones(window(v_ref, g)),
                                        preferred_element_type=F32))

    def softmax_stage(slot, table=None):
        s = s_refs[slot][...]
        if table is not None:
            s = s + table
        if table is not None or t_refs is None:
            row_max = jnp.max(s, axis=-1, keepdims=True)
        else:
            row_max = t_refs[slot][...]
        m_prev = m_ref[...]
        m_new = jnp.maximum(m_prev, row_max)
        a_refs[slot][...] = jnp.exp2(m_prev - m_new)
        p_refs[slot][...] = jnp.exp2(s - jnp.tile(m_new, (1, tk // LANES))).astype(BF16)
        m_ref[...] = m_new

    def score_stage(slot, window, lanes):
        onehot = (klane == lanes).astype(BF16)
        for g in range(groups):
            gr = slice(g * grows, (g + 1) * grows)
            s = _qk(lhs_ref[gr, :], jnp.concatenate([window(k_ref, g), onehot], axis=1))
            s_refs[slot][gr, :] = s
            if t_refs is not None:
                t_refs[slot][gr, :] = jnp.broadcast_to(jnp.max(s, axis=-1, keepdims=True), (grows, LANES))

    def loop_stage(i, slot):
        value_stage(slot, far_rows(i - 2))
        softmax_stage(1 - slot)
        score_stage(slot, far_rows(i), jnp.where(i < n_far, krow + i * per, LANES - 1))

    def body(ii, carry):
        loop_stage(2 * ii, 0)
        loop_stage(2 * ii + 1, 1)
        return carry

    lax.fori_loop(0, n_loop // 2, body, 0)

    value_stage(0, far_rows(n_loop - 2))
    softmax_stage(1)
    write_near_mask()
    near_lanes = jnp.where((krow >= tq // block) | (c >= 1), krow + (c - 1) * (tq // block), LANES - 1)
    score_stage(0, near_rows, near_lanes)
    value_stage(1, far_rows(n_loop - 1))
    softmax_stage(0, jnp.concatenate([dp_ref[...], dd_ref[...]], axis=1))
    value_stage(0, near_rows)
    return acc_ref[:, 0:LANES] / acc_ref[:, LANES:2 * LANES]


def _topk_mask(vals, k):
    lane = lax.broadcasted_iota(jnp.int32, vals.shape, 1).astype(F32)
    sel = jnp.zeros(vals.shape, jnp.bool_)
    for _ in range(k):
        m = jnp.max(vals, axis=-1, keepdims=True)
        idx = jnp.min(jnp.where(vals == m, lane, float(LANES)), axis=-1, keepdims=True)
        hit = lane == idx
        sel = jnp.logical_or(sel, hit)
        vals = jnp.where(hit, -jnp.inf, vals)
    return sel


def _topk_mask_rows(vals, k):
    row = lax.broadcasted_iota(jnp.int32, vals.shape, 0).astype(F32)
    sel = jnp.zeros(vals.shape, jnp.bool_)
    for _ in range(k):
        m = jnp.max(vals, axis=0, keepdims=True)
        idx = jnp.min(jnp.where(vals == m, row, float(vals.shape[0])), axis=0, keepdims=True)
        hit = row == idx
        sel = jnp.logical_or(sel, hit)
        vals = jnp.where(hit, -jnp.inf, vals)
    return sel


def _nsa_kernel(q_ref, ks_ref, vs_ref, kw_ref, vw_ref, g_ref, kc_ref, vc_ref, ov_ref, tc_ref, dp_ref, dd_ref, wt_ref,
                o_ref, lhs_ref, nm_ref, m_ref, acc_ref, out_ref, s0_ref, s1_ref, p0_ref, p1_ref, a0_ref, a1_ref, t0_ref,
                t1_ref):
    tq = ATT_TILE
    nh = NSA_HEADS
    rows = nh * tq
    c = pl.program_id(1)
    t0 = c * tq
    ncp = kc_ref.shape[0]
    wcols = 2 * (tq // CMP_STRIDE)

    lane = lax.broadcasted_iota(jnp.int32, (tq, LANES), 1)
    rowi = lax.broadcasted_iota(jnp.int32, (tq, LANES), 0)
    half = lane // NSA_HEAD_DIM

    q = q_ref[...]
    for g in range(NSA_KV_GROUPS):
        for r in range(NSA_REP):
            h = g * NSA_REP + r
            qb = q[:, r * LANES:(r + 1) * LANES]
            lhs_ref[h * tq:(h + 1) * tq, 0:LANES] = jnp.where(half == g, qb, jnp.zeros_like(qb))
    qrows = lhs_ref[:, 0:LANES]

    gates = jax.nn.sigmoid(g_ref[...].astype(F32))

    def gate_col(h, br):
        col = h * N_NSA_BRANCH + br
        return gates[:, col:col + 1]

    prev_start = pl.multiple_of(jnp.maximum(c - 1, 0) * tq, tq)
    diag_start = pl.multiple_of(c * tq, tq)

    ii = lax.broadcasted_iota(jnp.int32, (ncp, LANES), 0) - (c * (tq // CMP_STRIDE) - tq // CMP_STRIDE)
    ww = lax.broadcasted_iota(jnp.int32, (ncp, LANES), 1)
    place = (((ww < wcols) & (ii == ww)) | ((ww == wcols) & (ii >= wcols))).astype(BF16)
    s = _qk(jnp.concatenate([qrows, tc_ref[...]], axis=1), jnp.concatenate([kc_ref[...], place], axis=1))
    m = jnp.max(s, axis=-1, keepdims=True)
    p = jnp.exp2(s - m)
    r1 = jnp.dot(p.astype(BF16), jnp.concatenate([_with_ones(vc_ref[...]), ov_ref[...]], axis=1),
                 preferred_element_type=F32)
    has_key = m > 0.5 * NEG_INF
    inv_l = jnp.where(has_key, 1.0 / r1[:, LANES:2 * LANES], 0.0)
    o_cmp = r1[:, 0:LANES] * inv_l
    imp_h = r1[:, 2 * LANES:3 * LANES] * inv_l
    for h in range(nh):
        sl = slice(h * tq, (h + 1) * tq)
        out_ref[sl, :] = gate_col(h, 0) * o_cmp[sl]

    cur = (rowi + t0) // SEL_BLOCK
    forced = (lane == 0) | (lane == cur) | (lane == cur - 1)
    far_blocks = (c - 1) * (tq // SEL_BLOCK)
    for g in range(NSA_KV_GROUPS):
        imp = imp_h[g * NSA_REP * tq:(g * NSA_REP + 1) * tq]
        for r in range(1, NSA_REP):
            imp = imp + imp_h[(g * NSA_REP + r) * tq:(g * NSA_REP + r + 1) * tq]
        imp = jnp.where(forced, TOP_BONUS, jnp.where(lane <= cur, imp, NEG_INF))
        sel = _topk_mask(imp, SEL_TOPK)
        nm_ref[g] = jnp.where(sel, 0.0, NEG_INF).astype(BF16)
        far_mask = jnp.where(sel & (lane < far_blocks), 0.0, NEG_INF).astype(BF16)
        for r in range(NSA_REP):
            h = g * NSA_REP + r
            lhs_ref[h * tq:(h + 1) * tq, LANES:2 * LANES] = far_mask

    wpat = jnp.where(lax.broadcasted_iota(jnp.int32, (rows, LANES), 1) == 1, NEG_INF, 0.0).astype(BF16)
    back2_start = pl.multiple_of(jnp.maximum(c - 2, 0) * tq, tq)
    kwin = jnp.concatenate([kw_ref[pl.ds(back2_start, tq), :], kw_ref[pl.ds(prev_start, tq), :],
                            kw_ref[pl.ds(diag_start, tq), :]], axis=0)
    vwin = jnp.concatenate([vw_ref[pl.ds(back2_start, tq), :], vw_ref[pl.ds(prev_start, tq), :],
                            vw_ref[pl.ds(diag_start, tq), :]], axis=0)
    hot = jnp.concatenate([_lane_onehot(tq, jnp.where(c >= 2, 0, 1)), _lane_onehot(tq, jnp.where(c >= 1, 0, 1)),
                           _lane_onehot(tq, 0)], axis=0)
    s = _qk(jnp.concatenate([qrows, wpat], axis=1), jnp.concatenate([kwin, hot], axis=1))
    s = s + jnp.concatenate([jnp.tile(wt_ref[...], (nh, 1)), dp_ref[...], dd_ref[...]], axis=1)
    o_win = _softmax_values(s, _with_ones(vwin))
    for h in range(nh):
        sl = slice(h * tq, (h + 1) * tq)
        out_ref[sl, :] = out_ref[sl, :] + gate_col(h, 2) * o_win[sl]

    def write_near_mask():
        for h in range(nh):
            lhs_ref[h * tq:(h + 1) * tq, LANES:2 * LANES] = nm_ref[h // NSA_REP]

    o_sel = _pipelined_attention(c, SEL_BLOCK, 1, lhs_ref, write_near_mask, ks_ref, vs_ref, dp_ref, dd_ref,
                                 (s0_ref, s1_ref), (p0_ref, p1_ref), (a0_ref, a1_ref), m_ref, acc_ref,
                                 (t0_ref, t1_ref))

    for r in range(NSA_REP):
        h0, h1 = r, NSA_REP + r
        o0 = out_ref[h0 * tq:(h0 + 1) * tq, :] + gate_col(h0, 1) * o_sel[h0 * tq:(h0 + 1) * tq]
        o1 = out_ref[h1 * tq:(h1 + 1) * tq, :] + gate_col(h1, 1) * o_sel[h1 * tq:(h1 + 1) * tq]
        o_ref[:, r * LANES:(r + 1) * LANES] = jnp.where(half == 0, o0, o1).astype(o_ref.dtype)


def _nsa_attention(proj, kcmp, vcmp, overlap, tcmp, dprev, ddiag, wtab):
    b, s, _ = proj.shape
    tq = ATT_TILE
    ncp = kcmp.shape[1]
    rows = NSA_HEADS * tq

    def col(name, width):
        return SEC[name][0] // width

    full = lambda name: _const_spec((None, s, LANES), lambda i, c, n=name: (i, 0, col(n, LANES)))
    return pl.pallas_call(
        _nsa_kernel,
        grid=(b, s // tq),
        in_specs=[pl.BlockSpec((None, tq, 512), lambda i, c: (i, c, col("a_q", 512))),
                  full("a_ks"), full("a_vs"), full("a_kw"), full("a_vw"),
                  pl.BlockSpec((None, tq, LANES), lambda i, c: (i, c, col("a_g", LANES))),
                  _const_spec((None, ncp, LANES), lambda i, c: (i, 0, 0)),
                  _const_spec((None, ncp, LANES), lambda i, c: (i, 0, 0)),
                  _const_spec((ncp, LANES), lambda i, c: (0, 0)),
                  _const_spec((rows, LANES), lambda i, c: (0, 0)),
                  _const_spec((rows, tq), lambda i, c: (0, 0)),
                  _const_spec((rows, tq), lambda i, c: (0, 0)),
                  _const_spec((tq, tq), lambda i, c: (0, 0))],
        out_specs=pl.BlockSpec((None, tq, 512), lambda i, c: (i, c, 0)),
        out_shape=jax.ShapeDtypeStruct((b, s, 512), BF16),
        scratch_shapes=[pltpu.VMEM((rows, 2 * LANES), BF16),
                        pltpu.VMEM((NSA_KV_GROUPS, tq, LANES), BF16),
                        pltpu.VMEM((rows, LANES), F32), pltpu.VMEM((rows, 2 * LANES), F32),
                        pltpu.VMEM((rows, LANES), F32),
                        pltpu.VMEM((rows, FAR_TILE), F32), pltpu.VMEM((rows, FAR_TILE), F32),
                        pltpu.VMEM((rows, FAR_TILE), BF16), pltpu.VMEM((rows, FAR_TILE), BF16),
                        pltpu.VMEM((rows, LANES), F32), pltpu.VMEM((rows, LANES), F32),
                        pltpu.VMEM((rows, LANES), F32), pltpu.VMEM((rows, LANES), F32)],
        compiler_params=_cparams(("parallel", "arbitrary")),
        name="nsa_attention",
    )(proj, proj, proj, proj, proj, proj, kcmp, vcmp, overlap, tcmp, dprev, ddiag, wtab)


def _gla_kernel(q_ref, k_ref, v_ref, r_ref, lr_ref, w2_ref, gb_ref, gn_ref, o_ref, st_ref, oi_ref, bc_ref):
    ch = GLA_CHUNK
    lb = q_ref.shape[0]
    nch = lb // ch

    @pl.when(pl.program_id(1) == 0)
    def _():
        st_ref[...] = jnp.zeros(st_ref.shape, F32)

    x = jnp.dot(lr_ref[...], w2_ref[...], preferred_element_type=F32) + gb_ref[...]
    log_a = (jnp.minimum(x, 0.0) - jnp.log1p(jnp.exp(-jnp.abs(x)))) / GLA_GATE_TAU

    ti = lax.broadcasted_iota(jnp.int32, (lb, lb), 0)
    tj = lax.broadcasted_iota(jnp.int32, (lb, lb), 1)
    same_chunk = (ti // ch) == (tj // ch)
    causal = same_chunk & (tj <= ti)
    g_hi, g_mid, g_lo = _split3(log_a)

    def chunk_sums(w):
        return (jnp.dot(w, g_hi, preferred_element_type=F32) + jnp.dot(w, g_mid, preferred_element_type=F32)
                + jnp.dot(w, g_lo, preferred_element_type=F32))

    bcum = chunk_sums(causal.astype(BF16))
    btot = chunk_sums(same_chunk.astype(BF16))
    half = lax.broadcasted_iota(jnp.int32, (lb, LANES), 1) // GLA_KEY_DIM
    scale = GLA_KEY_DIM ** -0.5
    gn = gn_ref[...]

    def finish(o, hc):
        rg = r_ref[:, hc].astype(F32)
        o_ref[:, hc] = (_rms(o, gn) * (rg * jax.nn.sigmoid(rg))).astype(o_ref.dtype)

    for p in range(GLA_HEADS // 2):
        cols = slice(p * LANES, (p + 1) * LANES)
        bc = bcum[:, cols]
        bt = btot[:, cols]
        kf = k_ref[:, cols].astype(F32)
        qe = q_ref[:, cols].astype(F32) * scale * jnp.exp(bc)
        kinv = (kf * jnp.exp(-bc)).astype(BF16)
        klast = (kf * jnp.exp(bt - bc)).astype(BF16)
        decay = jnp.exp(bt)
        for a in range(2):
            h = 2 * p + a
            hc = slice(h * LANES, (h + 1) * LANES)
            qa = jnp.where(half == a, qe, 0.0).astype(BF16)
            v = v_ref[:, hc]
            attn = jnp.where(causal, _qk(qa, kinv), 0.0).astype(BF16)
            o_intra = jnp.dot(attn, v, preferred_element_type=F32)
            outs = []
            st = st_ref[h]
            for cc in range(nch):
                sl = slice(cc * ch, (cc + 1) * ch)
                outs.append(_qk(qa[sl], st.astype(BF16)))
                upd = lax.dot_general(v[sl], klast[sl], (((0,), (0,)), ((), ())), preferred_element_type=F32)
                st = st * decay[cc * ch:cc * ch + 1, :] + upd
            st_ref[h] = st
            o_inter = jnp.concatenate(outs, axis=0)
            oi_ref[:, hc] = o_inter
            finish(o_inter + o_intra, hc)

    @pl.when(jnp.min(btot) < -GLA_FACTORISED_RANGE)
    def _():
        bc_ref[...] = bcum
        rowc = lax.broadcasted_iota(jnp.int32, (ch, LANES), 0)
        lanec = lax.broadcasted_iota(jnp.int32, (ch, LANES), 1)

        def chunk_step(cc, carry):
            sl = pl.ds(pl.multiple_of(cc * ch, ch), ch)
            for p in range(GLA_HEADS // 2):
                cols = slice(p * LANES, (p + 1) * LANES)
                qc = q_ref[sl, cols].astype(F32) * scale
                kc = k_ref[sl, cols].astype(F32)
                bcc = bc_ref[sl, cols]
                v0 = v_ref[sl, 2 * p * LANES:(2 * p + 1) * LANES].astype(F32)
                v1 = v_ref[sl, (2 * p + 1) * LANES:(2 * p + 2) * LANES].astype(F32)
                acc0 = jnp.zeros((ch, LANES), F32)
                acc1 = jnp.zeros((ch, LANES), F32)
                for j in range(ch):
                    w = jnp.where(rowc >= j, jnp.exp(jnp.minimum(bcc - bcc[j:j + 1], 0.0)), 0.0)
                    t = qc * (kc[j:j + 1] * w)
                    a0 = jnp.sum(jnp.where(lanec < GLA_KEY_DIM, t, 0.0), axis=-1, keepdims=True)
                    a1 = jnp.sum(jnp.where(lanec >= GLA_KEY_DIM, t, 0.0), axis=-1, keepdims=True)
                    acc0 = acc0 + a0 * jnp.broadcast_to(v0[j:j + 1], (ch, LANES))
                    acc1 = acc1 + a1 * jnp.broadcast_to(v1[j:j + 1], (ch, LANES))
                oi_ref[sl, 2 * p * LANES:(2 * p + 1) * LANES] += acc0
                oi_ref[sl, (2 * p + 1) * LANES:(2 * p + 2) * LANES] += acc1
            return carry

        lax.fori_loop(0, nch, chunk_step, 0)
        for h in range(GLA_HEADS):
            hc = slice(h * LANES, (h + 1) * LANES)
            finish(oi_ref[:, hc], hc)


def _split3(x):
    hi = x.astype(BF16)
    r1 = x - hi.astype(F32)
    mid = r1.astype(BF16)
    lo = (r1 - mid.astype(F32)).astype(BF16)
    return hi, mid, lo


def _gla(proj, w2, gb, gn, lb):
    b, s, _ = proj.shape

    def spec(name, width):
        return pl.BlockSpec((None, lb, width), lambda i, c, n=name, w=width: (i, c, SEC[n][0] // w))

    return pl.pallas_call(
        _gla_kernel,
        grid=(b, s // lb),
        in_specs=[spec("b_q", 256), spec("b_k", 256), spec("b_v", 512), spec("b_r", 512), spec("b_lr", LANES),
                  pl.BlockSpec((LANES, 256), lambda i, c: (0, 0)),
                  pl.BlockSpec((1, 256), lambda i, c: (0, 0)),
                  pl.BlockSpec((1, LANES), lambda i, c: (0, 0))],
        out_specs=pl.BlockSpec((None, lb, 512), lambda i, c: (i, c, 0)),
        out_shape=jax.ShapeDtypeStruct((b, s, 512), BF16),
        scratch_shapes=[pltpu.VMEM((GLA_HEADS, LANES, LANES), F32), pltpu.VMEM((lb, 512), F32),
                        pltpu.VMEM((lb, 256), F32)],
        compiler_params=_cparams(("parallel", "arbitrary")),
        name="gla",
    )(proj, proj, proj, proj, proj, w2, gb, gn)


def _kmean_kernel(a_ref, k_ref, o_ref):
    o_ref[...] = jnp.dot(a_ref[...], k_ref[...], preferred_element_type=F32).astype(o_ref.dtype)


def _moba_kmean(proj, avg):
    b, s, _ = proj.shape
    return pl.pallas_call(
        _kmean_kernel,
        grid=(b,),
        in_specs=[pl.BlockSpec((LANES, s), lambda i: (0, 0)),
                  pl.BlockSpec((None, s, 512), lambda i: (i, 0, SEC["c_k"][0] // 512))],
        out_specs=pl.BlockSpec((None, LANES, 512), lambda i: (i, 0, 0)),
        out_shape=jax.ShapeDtypeStruct((b, LANES, 512), BF16),
        compiler_params=_cparams(("parallel",)),
        name="moba_kmean",
    )(avg, proj)


def _moba_kernel(q_ref, k_ref, v_ref, km_ref, dp_ref, dd_ref, o_ref, lhs_ref, m_ref, acc_ref,
                 s0_ref, s1_ref, p0_ref, p1_ref, a0_ref, a1_ref):
    tq = ATT_TILE
    nh = MOBA_HEADS
    npair = nh // 2
    rows = nh * tq
    c = pl.program_id(1)

    lane = lax.broadcasted_iota(jnp.int32, (tq, LANES), 1)
    half = lane // MOBA_HEAD_DIM
    for h in range(nh):
        qb = q_ref[:, (h // 2) * LANES:(h // 2 + 1) * LANES]
        lhs_ref[h * tq:(h + 1) * tq, 0:LANES] = jnp.where(half == h % 2, qb, jnp.zeros_like(qb))

    nblk = k_ref.shape[0] // MOBA_BLOCK
    score = jnp.concatenate(
        [_qk(km_ref[0:nblk, p * LANES:(p + 1) * LANES], lhs_ref[2 * p * tq:2 * (p + 1) * tq, 0:LANES])
         for p in range(npair)], axis=1)
    blk = lax.broadcasted_iota(jnp.int32, (nblk, rows), 0)
    chosen = _topk_mask_rows(jnp.where(blk < c, score, NEG_INF), MOBA_TOPK) & (blk < c)
    chosen = jnp.concatenate([jnp.where(chosen, 1.0, 0.0), jnp.zeros((LANES - nblk, rows), F32)], axis=0)
    past = jnp.transpose(chosen) > 0.5
    lane2 = lax.broadcasted_iota(jnp.int32, (rows, LANES), 1)
    lhs_ref[:, LANES:2 * LANES] = jnp.where(past & (lane2 < c - 1), 0.0, NEG_INF).astype(BF16)

    def write_near_mask():
        lhs_ref[:, LANES:2 * LANES] = jnp.where(past | (lane2 == c), 0.0, NEG_INF).astype(BF16)

    o = _pipelined_attention(c, MOBA_BLOCK, npair, lhs_ref, write_near_mask, k_ref, v_ref, dp_ref, dd_ref,
                             (s0_ref, s1_ref), (p0_ref, p1_ref), (a0_ref, a1_ref), m_ref, acc_ref)
    for p in range(npair):
        o0 = o[2 * p * tq:(2 * p + 1) * tq]
        o1 = o[(2 * p + 1) * tq:(2 * p + 2) * tq]
        o_ref[:, p * LANES:(p + 1) * LANES] = jnp.where(half == 0, o0, o1).astype(o_ref.dtype)


def _moba_attention(proj, kmean, dprev, ddiag):
    b, s, _ = proj.shape
    tq = ATT_TILE
    rows = MOBA_HEADS * tq

    def col(name):
        return SEC[name][0] // 512

    return pl.pallas_call(
        _moba_kernel,
        grid=(b, s // tq),
        in_specs=[pl.BlockSpec((None, tq, 512), lambda i, c: (i, c, col("c_q"))),
                  _const_spec((None, s, 512), lambda i, c: (i, 0, col("c_k"))),
                  _const_spec((None, s, 512), lambda i, c: (i, 0, col("c_v"))),
                  _const_spec((None, LANES, 512), lambda i, c: (i, 0, 0)),
                  _const_spec((rows, tq), lambda i, c: (0, 0)),
                  _const_spec((rows, tq), lambda i, c: (0, 0))],
        out_specs=pl.BlockSpec((None, tq, 512), lambda i, c: (i, c, 0)),
        out_shape=jax.ShapeDtypeStruct((b, s, 512), BF16),
        scratch_shapes=[pltpu.VMEM((rows, 2 * LANES), BF16),
                        pltpu.VMEM((rows, LANES), F32), pltpu.VMEM((rows, 2 * LANES), F32),
                        pltpu.VMEM((rows, FAR_TILE), F32), pltpu.VMEM((rows, FAR_TILE), F32),
                        pltpu.VMEM((rows, FAR_TILE), BF16), pltpu.VMEM((rows, FAR_TILE), BF16),
                        pltpu.VMEM((rows, LANES), F32), pltpu.VMEM((rows, LANES), F32)],
        compiler_params=_cparams(("parallel", "arbitrary")),
        name="moba_attention",
    )(proj, proj, proj, kmean, dprev, ddiag)


def _merge_kernel(oa_ref, ob_ref, oc_ref, g0_ref, g1_ref, g2_ref, x_ref, wa_ref, wb_ref, wc_ref, wo_ref, nw_ref,
                  o_ref):
    def branch(o, w, g):
        return jax.nn.sigmoid(g[...].astype(F32)) * jnp.dot(o[...], w[...], preferred_element_type=F32)

    merged = branch(oa_ref, wa_ref, g0_ref) + branch(ob_ref, wb_ref, g1_ref) + branch(oc_ref, wc_ref, g2_ref)
    y = jnp.dot(merged.astype(BF16), wo_ref[...], preferred_element_type=F32)
    o_ref[...] = x_ref[...] + _rms(y, nw_ref[...])


def _merge(oa, ob, oc, proj, x, wa, wb, wc, wo, nw, tm):
    t, d = x.shape
    w = oa.shape[1]
    row = lambda width, j=0: pl.BlockSpec((tm, width), lambda i, j=j: (i, j))
    const = lambda shape: pl.BlockSpec(shape, lambda i: (0, 0))
    return pl.pallas_call(
        _merge_kernel,
        grid=(t // tm,),
        in_specs=[row(w), row(w), row(w), row(d, 0), row(d, 1), row(d, 2), row(d),
                  const((w, d)), const((w, d)), const((w, d)), const((d, d)), const((1, d))],
        out_specs=row(d),
        out_shape=jax.ShapeDtypeStruct((t, d), F32),
        compiler_params=_cparams(("parallel",)),
        name="merge_out",
    )(oa, ob, oc, proj, proj, proj, x, wa, wb, wc, wo, nw.reshape(1, d))


def _ffn_kernel(x_ref, npre_ref, wi_ref, wo_ref, npost_ref, o_ref):
    h = _rms(x_ref[...], npre_ref[...]).astype(BF16)
    hid = wo_ref.shape[0]
    acc = jnp.zeros(o_ref.shape, F32)
    for lo in range(0, hid, FFN_CHUNK):
        hi = min(lo + FFN_CHUNK, hid)
        gate = jnp.dot(h, wi_ref[:, lo:hi], preferred_element_type=F32)
        up = jnp.dot(h, wi_ref[:, hid + lo:hid + hi], preferred_element_type=F32)
        act = (gate * jax.nn.sigmoid(gate) * up).astype(BF16)
        acc = acc + jnp.dot(act, wo_ref[lo:hi, :], preferred_element_type=F32)
    o_ref[...] = x_ref[...] + _rms(acc, npost_ref[...])


def _ffn(x, npre, wi, wo, npost, tm):
    t, d = x.shape
    hid = wo.shape[0]
    assert hid % LANES == 0
    return pl.pallas_call(
        _ffn_kernel,
        grid=(t // tm,),
        in_specs=[pl.BlockSpec((tm, d), lambda i: (i, 0)),
                  pl.BlockSpec((1, d), lambda i: (0, 0)),
                  _const_spec((d, 2 * hid), lambda i: (0, 0)),
                  _const_spec((hid, d), lambda i: (0, 0)),
                  pl.BlockSpec((1, d), lambda i: (0, 0))],
        out_specs=pl.BlockSpec((tm, d), lambda i: (i, 0)),
        out_shape=jax.ShapeDtypeStruct((t, d), F32),
        compiler_params=_cparams(("parallel",)),
        name="ffn",
    )(x, npre.reshape(1, d), wi, wo, npost.reshape(1, d))


def _t5_bucket_table(dist):
    n = np.maximum(dist, 0)
    max_exact = NUM_BUCKETS // 2
    log_ratio = np.log(np.maximum(n, 1).astype(np.float32) / max_exact) / math.log(T5_MAX_DISTANCE / max_exact)
    large = max_exact + (log_ratio * (NUM_BUCKETS - max_exact)).astype(np.int32)
    return np.where(n < max_exact, n, np.minimum(large, NUM_BUCKETS - 1)).astype(np.int32)


def _bias_lookup(rel, dist):
    onehot = (jnp.asarray(_t5_bucket_table(dist))[..., None] == jnp.arange(NUM_BUCKETS)).astype(F32)
    return jnp.dot(onehot, (rel - rel[NUM_BUCKETS - 1]) * LOG2E, precision=lax.Precision.HIGHEST)


def _near_bias(rel, t):
    qi = np.arange(t)[:, None]
    ki = np.arange(t)[None, :]
    prev = _bias_lookup(rel, qi + t - ki)
    diag = jnp.where((qi >= ki)[..., None], _bias_lookup(rel, qi - ki), NEG_INF)
    flat = lambda b: b.transpose(2, 0, 1).reshape(-1, t)
    return flat(prev), flat(diag)


def _cmp_bias(rel, t):
    w = t // CMP_STRIDE
    qi = np.arange(t)[:, None]
    j = np.arange(LANES)[None, :]
    d = qi - CMP_STRIDE * (j - w) - (CMP_LEN - 1)
    b = jnp.where(((j < 2 * w) & (d >= 0))[..., None], _bias_lookup(rel, d),
                  jnp.asarray(np.where(j <= 2 * w, NEG_INF, 0.0) * np.ones_like(d), F32)[..., None])
    return b.transpose(2, 0, 1).reshape(-1, LANES).astype(BF16)


def _overlap(s, nc):
    n_cmp = (s - CMP_LEN) // CMP_STRIDE + 1
    cmp_end = np.arange(nc) * CMP_STRIDE + CMP_LEN - 1
    cmp_start = cmp_end - (CMP_LEN - 1)
    sb_start = np.arange(LANES) * SEL_BLOCK
    ov = (cmp_start[:, None] < sb_start[None, :] + SEL_BLOCK) & (cmp_end[:, None] >= sb_start[None, :])
    ov = ov & (np.arange(nc)[:, None] < n_cmp) & (np.arange(LANES)[None, :] < s // SEL_BLOCK)
    return jnp.asarray(ov, BF16)


def _pair_diag(w):
    z = jnp.zeros_like(w)
    return jnp.concatenate([jnp.concatenate([w, z], axis=-1), jnp.concatenate([z, w], axis=-1)], axis=-2)


def kernel(x, rel_bias, norm_mix_pre, norm_mix_post, norm_ffn_pre, norm_ffn_post, w_in, nsa_pe_k, nsa_pe_v, nsa_cmp_k_w1, nsa_cmp_k_w2, nsa_cmp_v_w1, nsa_cmp_v_w2, gla_gate_w2, gla_gate_b, gla_norm, w_branch_a, w_branch_b, w_branch_c, w_out, w_ffn_in, w_ffn_out):
    b, s, d = x.shape
    depth = w_in.shape[0]
    t = b * s
    dk = NSA_HEAD_DIM
    tq = ATT_TILE
    assert d == D_MODEL and w_in.shape[2] == D_IN
    assert WINDOW == 2 * tq and MOBA_BLOCK == tq and s % tq == 0
    assert SEL_TOPK <= s // SEL_BLOCK <= LANES and s // MOBA_BLOCK <= LANES
    nc = s // CMP_STRIDE
    assert nc % LANES == 0

    w_in_p = _layout_w_in(w_in)
    rel_a = rel_bias[:, :NSA_HEADS]
    rel_c = rel_bias[:, NSA_HEADS:]
    dprev_a, ddiag_a = _near_bias(rel_a, tq)
    dprev_c, ddiag_c = _near_bias(rel_c, tq)
    tcmp = _cmp_bias(rel_a, tq)
    overlap = _overlap(s, nc)
    wtab = jnp.asarray(np.where(np.arange(tq)[None, :] > np.arange(tq)[:, None], 0.0, NEG_INF), F32)
    avg = jnp.asarray((np.arange(LANES)[:, None] == np.arange(s)[None, :] // MOBA_BLOCK) / MOBA_BLOCK, BF16)

    def cmp_w1(w1):
        w = _pair_diag(w1.reshape(depth, 2, CMP_STRIDE, dk, dk))
        return w.reshape(depth, 2, CMP_STRIDE * 2 * dk, 2 * dk)

    def cmp_pe(pe):
        p2 = jnp.concatenate([pe, pe], axis=-1).reshape(depth, 2, 1, CMP_STRIDE * 2 * dk)
        return jnp.broadcast_to(p2, (depth, 2, 8, CMP_STRIDE * 2 * dk))

    cw1 = jnp.stack([cmp_w1(nsa_cmp_k_w1), cmp_w1(nsa_cmp_v_w1)], axis=1).astype(BF16)
    cw2 = jnp.stack([_pair_diag(nsa_cmp_k_w2), _pair_diag(nsa_cmp_v_w2)], axis=1).astype(BF16)
    cpe = jnp.stack([cmp_pe(nsa_pe_k), cmp_pe(nsa_pe_v)], axis=1).astype(BF16)

    gw2 = jnp.concatenate([gla_gate_w2, jnp.zeros((depth, LANES - GLA_GATE_RANK, gla_gate_w2.shape[2]), F32)],
                          axis=1).astype(BF16)
    wa = (w_branch_a.reshape(depth, NSA_KV_GROUPS, NSA_REP, dk, d).transpose(0, 2, 1, 3, 4)
          .reshape(depth, NSA_HEADS * dk, d).astype(BF16))
    wb = w_branch_b.astype(BF16)
    wc = w_branch_c.astype(BF16)
    wo = w_out.astype(BF16)
    wi = w_ffn_in.astype(BF16)
    wf = w_ffn_out.astype(BF16)

    xf = x.reshape(t, d)
    for layer in range(depth):
        proj = _norm_matmul(xf, norm_mix_pre[layer], w_in_p[layer], min(512, t)).reshape(b, s, NP_COLS)
        kc0, vc0 = SEC["a_kc"][0], SEC["a_vc"][0]
        xkv = jnp.stack([proj[:, :, kc0:kc0 + LANES], proj[:, :, vc0:vc0 + LANES]])
        xkv = xkv.reshape(2, b, nc, CMP_STRIDE * LANES)
        cmp = _compress(xkv, cpe[layer], cw1[layer], cw2[layer])
        o_a = _nsa_attention(proj, cmp[0], cmp[1], overlap, tcmp, dprev_a, ddiag_a, wtab)
        o_b = _gla(proj, gw2[layer], gla_gate_b[layer].reshape(1, -1), gla_norm[layer].reshape(1, -1),
                   min(256, s))
        kmean = _moba_kmean(proj, avg)
        o_c = _moba_attention(proj, kmean, dprev_c, ddiag_c)
        xf = _merge(o_a.reshape(t, -1), o_b.reshape(t, -1), o_c.reshape(t, -1), proj.reshape(t, NP_COLS), xf,
                    wa[layer], wb[layer], wc[layer], wo[layer], norm_mix_post[layer], min(512, t))
        xf = _ffn(xf, norm_ffn_pre[layer], wi[layer], wf[layer], norm_ffn_post[layer], min(512, t))
    return xf.reshape(b, s, d)
```

```python
import math

import numpy as np
import jax
import jax.numpy as jnp
from jax import lax
from jax.experimental import pallas as pl
from jax.experimental.pallas import tpu as pltpu

F32 = jnp.float32
BF16 = jnp.bfloat16

NORM_EPS = 1e-6
NEG_INF = -1e30
TOP_BONUS = 1e9
NUM_BUCKETS = 32
T5_MAX_DISTANCE = 128
NSA_HEADS = 8
NSA_KV_GROUPS = 2
NSA_REP = NSA_HEADS // NSA_KV_GROUPS
NSA_HEAD_DIM = 64
CMP_LEN = 32
CMP_STRIDE = 16
SEL_BLOCK = 64
SEL_TOPK = 16
WINDOW = 512
N_NSA_BRANCH = 3
GLA_HEADS = 4
GLA_KEY_DIM = 64
GLA_VAL_DIM = 128
GLA_GATE_RANK = 16
GLA_GATE_TAU = 16.0
GLA_CHUNK = 64
GLA_FACTORISED_RANGE = 80.0
MOBA_HEADS = 8
MOBA_HEAD_DIM = 64
MOBA_BLOCK = 256
MOBA_TOPK = 3
N_BRANCHES = 3
LOG2E = math.log2(math.e)

LANES = 128
ATT_TILE = 256
FAR_TILE = 2 * ATT_TILE
FFN_CHUNK = 512
PROJ_CHUNK = 1536
VMEM_LIMIT = 56 * 1024 * 1024

D_MODEL = 1024
SEC = {}
_off = 0
for _name, _w in (("mg", 3 * D_MODEL), ("a_q", 512), ("c_q", 512), ("c_k", 512), ("c_v", 512),
                  ("b_v", 512), ("b_r", 512), ("b_q", 256), ("b_k", 256),
                  ("a_kc", 128), ("a_vc", 128), ("a_ks", 128), ("a_vs", 128), ("a_kw", 128), ("a_vw", 128),
                  ("a_g", 128), ("b_lr", 128)):
    SEC[_name] = (_off, _w)
    _off += _w
NP_COLS = _off

_SRC_NAMES = ("a_q", "a_kc", "a_vc", "a_ks", "a_vs", "a_kw", "a_vw", "a_g", "b_q", "b_k", "b_v", "b_r", "b_lr",
              "c_q", "c_k", "c_v", "mg")
_SRC_SIZES = (512, 128, 128, 128, 128, 128, 128, 24, 256, 256, 512, 512, 16, 512, 512, 512, 3 * D_MODEL)
_SRC_OFF = dict(zip(_SRC_NAMES, np.cumsum((0,) + _SRC_SIZES[:-1]).tolist()))
_SRC_W = dict(zip(_SRC_NAMES, _SRC_SIZES))
D_IN = int(sum(_SRC_SIZES))


def _layout_w_in(w_in):
    depth, d, _ = w_in.shape

    def src(name, lo=0, hi=None):
        hi = _SRC_W[name] if hi is None else hi
        return w_in[:, :, _SRC_OFF[name] + lo:_SRC_OFF[name] + hi]

    parts = []
    for name, (_, width) in SEC.items():
        if name == "a_q":
            for r in range(NSA_REP):
                for g in range(NSA_KV_GROUPS):
                    h = g * NSA_REP + r
                    parts.append(src(name, h * NSA_HEAD_DIM, (h + 1) * NSA_HEAD_DIM) * (NSA_HEAD_DIM ** -0.5 * LOG2E))
        elif name == "c_q":
            parts.append(src(name) * (MOBA_HEAD_DIM ** -0.5 * LOG2E))
        else:
            parts.append(src(name))
            if _SRC_W[name] < width:
                parts.append(jnp.zeros((depth, d, width - _SRC_W[name]), w_in.dtype))
    return jnp.concatenate(parts, axis=2).astype(BF16)


def _cparams(sem):
    return pltpu.CompilerParams(dimension_semantics=sem, vmem_limit_bytes=VMEM_LIMIT)


def _const_spec(shape, index_map):
    return pl.BlockSpec(shape, index_map, pipeline_mode=pl.Buffered(1))


def _rms(y, w):
    return y * lax.rsqrt(jnp.mean(y * y, axis=-1, keepdims=True) + NORM_EPS) * w


def _norm_matmul_kernel(x_ref, nw_ref, w_ref, o_ref):
    h = _rms(x_ref[...], nw_ref[...]).astype(BF16)
    for lo in range(0, w_ref.shape[1], PROJ_CHUNK):
        hi = lo + PROJ_CHUNK
        o_ref[:, lo:hi] = jnp.dot(h, w_ref[:, lo:hi], preferred_element_type=F32).astype(o_ref.dtype)


def _norm_matmul(x, nw, w, tm):
    t, d = x.shape
    n = w.shape[1]
    assert n % PROJ_CHUNK == 0
    return pl.pallas_call(
        _norm_matmul_kernel,
        grid=(t // tm,),
        in_specs=[pl.BlockSpec((tm, d), lambda i: (i, 0)),
                  pl.BlockSpec((1, d), lambda i: (0, 0)),
                  _const_spec((d, n), lambda i: (0, 0))],
        out_specs=pl.BlockSpec((tm, n), lambda i: (i, 0)),
        out_shape=jax.ShapeDtypeStruct((t, n), BF16),
        compiler_params=_cparams(("parallel",)),
        name="norm_proj",
    )(x, nw.reshape(1, d), w)


def _compress_kernel(x_ref, pe_ref, w1_ref, w2_ref, o_ref):
    x = x_ref[...]
    nc = x.shape[0]
    w1t = w1_ref[0]
    w1b = w1_ref[1]
    a = jnp.dot(x, w1t, preferred_element_type=F32)
    b = jnp.dot(x, w1b, preferred_element_type=F32)
    pe = pe_ref[...]
    pe_term = (jnp.dot(pe[0], w1t, preferred_element_type=F32)
               + jnp.dot(pe[1], w1b, preferred_element_type=F32))[0:1]
    pre = a + pltpu.roll(b, nc - 1, 0) + pe_term
    hid = jax.nn.gelu(pre)
    o_ref[...] = jnp.dot(hid.astype(BF16), w2_ref[...], preferred_element_type=F32).astype(o_ref.dtype)


def _compress(xkv, pe, w1, w2):
    _, b, nc, kw = xkv.shape
    return pl.pallas_call(
        _compress_kernel,
        grid=(2, b),
        in_specs=[pl.BlockSpec((None, None, nc, kw), lambda s, i: (s, i, 0, 0)),
                  pl.BlockSpec((None, 2, 8, kw), lambda s, i: (s, 0, 0, 0)),
                  pl.BlockSpec((None, 2, kw, LANES), lambda s, i: (s, 0, 0, 0)),
                  pl.BlockSpec((None, LANES, LANES), lambda s, i: (s, 0, 0))],
        out_specs=pl.BlockSpec((None, None, nc, LANES), lambda s, i: (s, i, 0, 0)),
        out_shape=jax.ShapeDtypeStruct((2, b, nc, LANES), BF16),
        compiler_params=_cparams(("parallel", "parallel")),
        name="nsa_compress",
    )(xkv, pe, w1, w2)


def _with_ones(v):
    return jnp.concatenate([v, jnp.ones(v.shape, v.dtype)], axis=1)


def _softmax_values(s, v1):
    p = jnp.exp2(s - jnp.max(s, axis=-1, keepdims=True)).astype(BF16)
    r = jnp.dot(p, v1, preferred_element_type=F32)
    return r[:, 0:LANES] / r[:, LANES:2 * LANES]


def _qk(lhs, rhs):
    return lax.dot_general(lhs, rhs, (((1,), (1,)), ((), ())), preferred_element_type=F32)


def _lane_onehot(width, lane_idx):
    klane = lax.broadcasted_iota(jnp.int32, (width, LANES), 1)
    return (klane == lane_idx).astype(BF16)


def _pipelined_attention(c, block, groups, lhs_ref, write_near_mask, k_ref, v_ref, dp_ref, dd_ref,
                         s_refs, p_refs, a_refs, m_ref, acc_ref, t_refs=None):
    tq = ATT_TILE
    tk = FAR_TILE
    rows = lhs_ref.shape[0]
    grows = rows // groups
    j_max = k_ref.shape[0] // tk - 1
    per = tk // block
    n_far = (jnp.maximum(c - 1, 0) * tq + tk - 1) // tk
    n_loop = 2 * ((n_far + 1) // 2)
    krow = lax.broadcasted_iota(jnp.int32, (tk, LANES), 0) // block
    klane = lax.broadcasted_iota(jnp.int32, (tk, LANES), 1)

    m_ref[...] = jnp.full(m_ref.shape, NEG_INF, F32)
    acc_ref[...] = jnp.zeros(acc_ref.shape, F32)

    def far_rows(j):
        start = pl.multiple_of(jnp.clip(j, 0, j_max) * tk, tk)
        return lambda ref, g: ref[pl.ds(start, tk), g * LANES:(g + 1) * LANES]

    def near_rows(ref, g):
        cols = slice(g * LANES, (g + 1) * LANES)
        prev = ref[pl.ds(pl.multiple_of(jnp.maximum(c - 1, 0) * tq, tq), tq), cols]
        return jnp.concatenate([prev, ref[pl.ds(pl.multiple_of(c * tq, tq), tq), cols]], axis=0)

    def value_stage(slot, window):
        for g in range(groups):
            gr = slice(g * grows, (g + 1) * grows)
            acc_ref[gr, :] = (jnp.tile(a_refs[slot][gr, :], (1, 2)) * acc_ref[gr, :]
                              + jnp.dot(p_refs[slot][gr, :], _with_ones(window(v_ref, g)),
                                        preferred_element_type=F32))

    def softmax_stage(slot, table=None):
        s = s_refs[slot][...]
        if table is not None:
            s = s + table
        if table is not None or t_refs is None:
            row_max = jnp.max(s, axis=-1, keepdims=True)
        else:
            row_max = t_refs[slot][...]
        m_prev = m_ref[...]
        m_new = jnp.maximum(m_prev, row_max)
        a_refs[slot][...] = jnp.exp2(m_prev - m_new)
        p_refs[slot][...] = jnp.exp2(s - jnp.tile(m_new, (1, tk // LANES))).astype(BF16)
        m_ref[...] = m_new

    def score_stage(slot, window, lanes):
        onehot = (klane == lanes).astype(BF16)
        for g in range(groups):
            gr = slice(g * grows, (g + 1) * grows)
            s = _qk(lhs_ref[gr, :], jnp.concatenate([window(k_ref, g), onehot], axis=1))
            s_refs[slot][gr, :] = s
            if t_refs is not None:
                t_refs[slot][gr, :] = jnp.broadcast_to(jnp.max(s, axis=-1, keepdims=True), (grows, LANES))

    odd = n_far % 2

    def far_lanes(i):
        return jnp.where((i >= odd) & (i - odd < n_far), krow + (i - odd) * per, LANES - 1)

    def loop_stage(i, slot):
        value_stage(slot, far_rows(i - 2 - odd))
        softmax_stage(1 - slot)
        score_stage(slot, far_rows(i - odd), far_lanes(i))

    def body(ii, carry):
        loop_stage(2 * ii, 0)
        loop_stage(2 * ii + 1, 1)
        return carry

    @pl.when(n_loop == 0)
    def _():
        s_refs[1][...] = jnp.full((rows, tk), -jnp.inf, F32)
        if t_refs is not None:
            t_refs[1][...] = jnp.full((rows, LANES), -jnp.inf, F32)
        p_refs[0][...] = jnp.zeros((rows, tk), BF16)
        a_refs[0][...] = jnp.ones((rows, LANES), F32)

    @pl.when((n_loop > 0) & (odd == 0))
    def _():
        score_stage(0, far_rows(0), far_lanes(0))
        softmax_stage(0)
        score_stage(1, far_rows(1), far_lanes(1))

    @pl.when(odd == 1)
    def _():
        p_refs[0][...] = jnp.zeros((rows, tk), BF16)
        a_refs[0][...] = jnp.ones((rows, LANES), F32)
        score_stage(1, far_rows(0), far_lanes(1))

    lax.fori_loop(1, n_loop // 2, body, 0)

    value_stage(0, far_rows(n_far - 2))
    softmax_stage(1)
    write_near_mask()
    near_lanes = jnp.where((krow >= tq // block) | (c >= 1), krow + (c - 1) * (tq // block), LANES - 1)
    score_stage(0, near_rows, near_lanes)
    value_stage(1, far_rows(n_far - 1))
    softmax_stage(0, jnp.concatenate([dp_ref[...], dd_ref[...]], axis=1))
    value_stage(0, near_rows)
    return acc_ref[:, 0:LANES] / acc_ref[:, LANES:2 * LANES]


def _topk_mask(vals, k):
    lane = lax.broadcasted_iota(jnp.int32, vals.shape, 1).astype(F32)
    sel = jnp.zeros(vals.shape, jnp.bool_)
    for _ in range(k):
        m = jnp.max(vals, axis=-1, keepdims=True)
        idx = jnp.min(jnp.where(vals == m, lane, float(LANES)), axis=-1, keepdims=True)
        hit = lane == idx
        sel = jnp.logical_or(sel, hit)
        vals = jnp.where(hit, -jnp.inf, vals)
    return sel


def _topk_mask_rows(vals, k):
    row = lax.broadcasted_iota(jnp.int32, vals.shape, 0).astype(F32)
    sel = jnp.zeros(vals.shape, jnp.bool_)
    for _ in range(k):
        m = jnp.max(vals, axis=0, keepdims=True)
        idx = jnp.min(jnp.where(vals == m, row, float(vals.shape[0])), axis=0, keepdims=True)
        hit = row == idx
        sel = jnp.logical_or(sel, hit)
        vals = jnp.where(hit, -jnp.inf, vals)
    return sel


def _nsa_kernel(q_ref, ks_ref, vs_ref, kw_ref, vw_ref, g_ref, kc_ref, vc_ref, ov_ref, tc_ref, dp_ref, dd_ref, wt_ref,
                o_ref, lhs_ref, nm_ref, m_ref, acc_ref, out_ref, s0_ref, s1_ref, p0_ref, p1_ref, a0_ref, a1_ref, t0_ref,
                t1_ref):
    tq = ATT_TILE
    nh = NSA_HEADS
    rows = nh * tq
    c = pl.program_id(1)
    t0 = c * tq
    ncp = kc_ref.shape[0]
    wcols = 2 * (tq // CMP_STRIDE)

    lane = lax.broadcasted_iota(jnp.int32, (tq, LANES), 1)
    rowi = lax.broadcasted_iota(jnp.int32, (tq, LANES), 0)
    half = lane // NSA_HEAD_DIM

    q = q_ref[...]
    for g in range(NSA_KV_GROUPS):
        for r in range(NSA_REP):
            h = g * NSA_REP + r
            qb = q[:, r * LANES:(r + 1) * LANES]
            lhs_ref[h * tq:(h + 1) * tq, 0:LANES] = jnp.where(half == g, qb, jnp.zeros_like(qb))
    qrows = lhs_ref[:, 0:LANES]

    gates = jax.nn.sigmoid(g_ref[...].astype(F32))

    def gate_col(h, br):
        col = h * N_NSA_BRANCH + br
        return gates[:, col:col + 1]

    prev_start = pl.multiple_of(jnp.maximum(c - 1, 0) * tq, tq)
    diag_start = pl.multiple_of(c * tq, tq)

    ii = lax.broadcasted_iota(jnp.int32, (ncp, LANES), 0) - (c * (tq // CMP_STRIDE) - tq // CMP_STRIDE)
    ww = lax.broadcasted_iota(jnp.int32, (ncp, LANES), 1)
    place = (((ww < wcols) & (ii == ww)) | ((ww == wcols) & (ii >= wcols))).astype(BF16)
    s = _qk(jnp.concatenate([qrows, tc_ref[...]], axis=1), jnp.concatenate([kc_ref[...], place], axis=1))
    m = jnp.max(s, axis=-1, keepdims=True)
    p = jnp.exp2(s - m)
    r1 = jnp.dot(p.astype(BF16), jnp.concatenate([_with_ones(vc_ref[...]), ov_ref[...]], axis=1),
                 preferred_element_type=F32)
    has_key = m > 0.5 * NEG_INF
    inv_l = jnp.where(has_key, 1.0 / r1[:, LANES:2 * LANES], 0.0)
    o_cmp = r1[:, 0:LANES] * inv_l
    imp_h = r1[:, 2 * LANES:3 * LANES] * inv_l
    for h in range(nh):
        sl = slice(h * tq, (h + 1) * tq)
        out_ref[sl, :] = gate_col(h, 0) * o_cmp[sl]

    cur = (rowi + t0) // SEL_BLOCK
    forced = (lane == 0) | (lane == cur) | (lane == cur - 1)
    far_blocks = (c - 1) * (tq // SEL_BLOCK)
    for g in range(NSA_KV_GROUPS):
        imp = imp_h[g * NSA_REP * tq:(g * NSA_REP + 1) * tq]
        for r in range(1, NSA_REP):
            imp = imp + imp_h[(g * NSA_REP + r) * tq:(g * NSA_REP + r + 1) * tq]
        imp = jnp.where(forced, TOP_BONUS, jnp.where(lane <= cur, imp, NEG_INF))
        sel = _topk_mask(imp, SEL_TOPK)
        nm_ref[g] = jnp.where(sel, 0.0, NEG_INF).astype(BF16)
        far_mask = jnp.where(sel & (lane < far_blocks), 0.0, NEG_INF).astype(BF16)
        for r in range(NSA_REP):
            h = g * NSA_REP + r
            lhs_ref[h * tq:(h + 1) * tq, LANES:2 * LANES] = far_mask

    wpat = jnp.where(lax.broadcasted_iota(jnp.int32, (rows, LANES), 1) == 1, NEG_INF, 0.0).astype(BF16)
    back2_start = pl.multiple_of(jnp.maximum(c - 2, 0) * tq, tq)
    kwin = jnp.concatenate([kw_ref[pl.ds(back2_start, tq), :], kw_ref[pl.ds(prev_start, tq), :],
                            kw_ref[pl.ds(diag_start, tq), :]], axis=0)
    vwin = jnp.concatenate([vw_ref[pl.ds(back2_start, tq), :], vw_ref[pl.ds(prev_start, tq), :],
                            vw_ref[pl.ds(diag_start, tq), :]], axis=0)
    hot = jnp.concatenate([_lane_onehot(tq, jnp.where(c >= 2, 0, 1)), _lane_onehot(tq, jnp.where(c >= 1, 0, 1)),
                           _lane_onehot(tq, 0)], axis=0)
    s = _qk(jnp.concatenate([qrows, wpat], axis=1), jnp.concatenate([kwin, hot], axis=1))
    s = s + jnp.concatenate([jnp.tile(wt_ref[...], (nh, 1)), dp_ref[...], dd_ref[...]], axis=1)
    o_win = _softmax_values(s, _with_ones(vwin))
    for h in range(nh):
        sl = slice(h * tq, (h + 1) * tq)
        out_ref[sl, :] = out_ref[sl, :] + gate_col(h, 2) * o_win[sl]

    def write_near_mask():
        for h in range(nh):
            lhs_ref[h * tq:(h + 1) * tq, LANES:2 * LANES] = nm_ref[h // NSA_REP]

    o_sel = _pipelined_attention(c, SEL_BLOCK, 1, lhs_ref, write_near_mask, ks_ref, vs_ref, dp_ref, dd_ref,
                                 (s0_ref, s1_ref), (p0_ref, p1_ref), (a0_ref, a1_ref), m_ref, acc_ref,
                                 (t0_ref, t1_ref))

    for r in range(NSA_REP):
        h0, h1 = r, NSA_REP + r
        o0 = out_ref[h0 * tq:(h0 + 1) * tq, :] + gate_col(h0, 1) * o_sel[h0 * tq:(h0 + 1) * tq]
        o1 = out_ref[h1 * tq:(h1 + 1) * tq, :] + gate_col(h1, 1) * o_sel[h1 * tq:(h1 + 1) * tq]
        o_ref[:, r * LANES:(r + 1) * LANES] = jnp.where(half == 0, o0, o1).astype(o_ref.dtype)


def _nsa_attention(proj, kcmp, vcmp, overlap, tcmp, dprev, ddiag, wtab):
    b, s, _ = proj.shape
    tq = ATT_TILE
    ncp = kcmp.shape[1]
    rows = NSA_HEADS * tq

    def col(name, width):
        return SEC[name][0] // width

    full = lambda name: _const_spec((None, s, LANES), lambda i, c, n=name: (i, 0, col(n, LANES)))
    return pl.pallas_call(
        _nsa_kernel,
        grid=(b, s // tq),
        in_specs=[pl.BlockSpec((None, tq, 512), lambda i, c: (i, c, col("a_q", 512))),
                  full("a_ks"), full("a_vs"), full("a_kw"), full("a_vw"),
                  pl.BlockSpec((None, tq, LANES), lambda i, c: (i, c, col("a_g", LANES))),
                  _const_spec((None, ncp, LANES), lambda i, c: (i, 0, 0)),
                  _const_spec((None, ncp, LANES), lambda i, c: (i, 0, 0)),
                  _const_spec((ncp, LANES), lambda i, c: (0, 0)),
                  _const_spec((rows, LANES), lambda i, c: (0, 0)),
                  _const_spec((rows, tq), lambda i, c: (0, 0)),
                  _const_spec((rows, tq), lambda i, c: (0, 0)),
                  _const_spec((tq, tq), lambda i, c: (0, 0))],
        out_specs=pl.BlockSpec((None, tq, 512), lambda i, c: (i, c, 0)),
        out_shape=jax.ShapeDtypeStruct((b, s, 512), BF16),
        scratch_shapes=[pltpu.VMEM((rows, 2 * LANES), BF16),
                        pltpu.VMEM((NSA_KV_GROUPS, tq, LANES), BF16),
                        pltpu.VMEM((rows, LANES), F32), pltpu.VMEM((rows, 2 * LANES), F32),
                        pltpu.VMEM((rows, LANES), F32),
                        pltpu.VMEM((rows, FAR_TILE), F32), pltpu.VMEM((rows, FAR_TILE), F32),
                        pltpu.VMEM((rows, FAR_TILE), BF16), pltpu.VMEM((rows, FAR_TILE), BF16),
                        pltpu.VMEM((rows, LANES), F32), pltpu.VMEM((rows, LANES), F32),
                        pltpu.VMEM((rows, LANES), F32), pltpu.VMEM((rows, LANES), F32)],
        compiler_params=_cparams(("parallel", "arbitrary")),
        name="nsa_attention",
    )(proj, proj, proj, proj, proj, proj, kcmp, vcmp, overlap, tcmp, dprev, ddiag, wtab)


def _gla_kernel(q_ref, k_ref, v_ref, r_ref, lr_ref, w2_ref, gb_ref, gn_ref, o_ref, st_ref, oi_ref, bc_ref):
    ch = GLA_CHUNK
    lb = q_ref.shape[0]
    nch = lb // ch

    @pl.when(pl.program_id(1) == 0)
    def _():
        st_ref[...] = jnp.zeros(st_ref.shape, F32)

    x = jnp.dot(lr_ref[...], w2_ref[...], preferred_element_type=F32) + gb_ref[...]
    log_a = (jnp.minimum(x, 0.0) - jnp.log1p(jnp.exp(-jnp.abs(x)))) / GLA_GATE_TAU

    ti = lax.broadcasted_iota(jnp.int32, (lb, lb), 0)
    tj = lax.broadcasted_iota(jnp.int32, (lb, lb), 1)
    same_chunk = (ti // ch) == (tj // ch)
    causal = same_chunk & (tj <= ti)
    g_hi, g_mid, g_lo = _split3(log_a)

    def chunk_sums(w):
        return (jnp.dot(w, g_hi, preferred_element_type=F32) + jnp.dot(w, g_mid, preferred_element_type=F32)
                + jnp.dot(w, g_lo, preferred_element_type=F32))

    bcum = chunk_sums(causal.astype(BF16))
    btot = chunk_sums(same_chunk.astype(BF16))
    half = lax.broadcasted_iota(jnp.int32, (lb, LANES), 1) // GLA_KEY_DIM
    scale = GLA_KEY_DIM ** -0.5
    gn = gn_ref[...]

    def finish(o, hc):
        rg = r_ref[:, hc].astype(F32)
        o_ref[:, hc] = (_rms(o, gn) * (rg * jax.nn.sigmoid(rg))).astype(o_ref.dtype)

    for p in range(GLA_HEADS // 2):
        cols = slice(p * LANES, (p + 1) * LANES)
        bc = bcum[:, cols]
        bt = btot[:, cols]
        kf = k_ref[:, cols].astype(F32)
        qe = q_ref[:, cols].astype(F32) * scale * jnp.exp(bc)
        kinv = (kf * jnp.exp(-bc)).astype(BF16)
        klast = (kf * jnp.exp(bt - bc)).astype(BF16)
        decay = jnp.exp(bt)
        for a in range(2):
            h = 2 * p + a
            hc = slice(h * LANES, (h + 1) * LANES)
            qa = jnp.where(half == a, qe, 0.0).astype(BF16)
            v = v_ref[:, hc]
            attn = jnp.where(causal, _qk(qa, kinv), 0.0).astype(BF16)
            o_intra = jnp.dot(attn, v, preferred_element_type=F32)
            outs = []
            st = st_ref[h]
            for cc in range(nch):
                sl = slice(cc * ch, (cc + 1) * ch)
                outs.append(_qk(qa[sl], st.astype(BF16)))
                upd = lax.dot_general(v[sl], klast[sl], (((0,), (0,)), ((), ())), preferred_element_type=F32)
                st = st * decay[cc * ch:cc * ch + 1, :] + upd
            st_ref[h] = st
            o_inter = jnp.concatenate(outs, axis=0)
            oi_ref[:, hc] = o_inter
            finish(o_inter + o_intra, hc)

    @pl.when(jnp.min(btot) < -GLA_FACTORISED_RANGE)
    def _():
        bc_ref[...] = bcum
        rowc = lax.broadcasted_iota(jnp.int32, (ch, LANES), 0)
        lanec = lax.broadcasted_iota(jnp.int32, (ch, LANES), 1)

        def chunk_step(cc, carry):
            sl = pl.ds(pl.multiple_of(cc * ch, ch), ch)
            for p in range(GLA_HEADS // 2):
                cols = slice(p * LANES, (p + 1) * LANES)
                qc = q_ref[sl, cols].astype(F32) * scale
                kc = k_ref[sl, cols].astype(F32)
                bcc = bc_ref[sl, cols]
                v0 = v_ref[sl, 2 * p * LANES:(2 * p + 1) * LANES].astype(F32)
                v1 = v_ref[sl, (2 * p + 1) * LANES:(2 * p + 2) * LANES].astype(F32)
                acc0 = jnp.zeros((ch, LANES), F32)
                acc1 = jnp.zeros((ch, LANES), F32)
                for j in range(ch):
                    w = jnp.where(rowc >= j, jnp.exp(jnp.minimum(bcc - bcc[j:j + 1], 0.0)), 0.0)
                    t = qc * (kc[j:j + 1] * w)
                    a0 = jnp.sum(jnp.where(lanec < GLA_KEY_DIM, t, 0.0), axis=-1, keepdims=True)
                    a1 = jnp.sum(jnp.where(lanec >= GLA_KEY_DIM, t, 0.0), axis=-1, keepdims=True)
                    acc0 = acc0 + a0 * jnp.broadcast_to(v0[j:j + 1], (ch, LANES))
                    acc1 = acc1 + a1 * jnp.broadcast_to(v1[j:j + 1], (ch, LANES))
                oi_ref[sl, 2 * p * LANES:(2 * p + 1) * LANES] += acc0
                oi_ref[sl, (2 * p + 1) * LANES:(2 * p + 2) * LANES] += acc1
            return carry

        lax.fori_loop(0, nch, chunk_step, 0)
        for h in range(GLA_HEADS):
            hc = slice(h * LANES, (h + 1) * LANES)
            finish(oi_ref[:, hc], hc)


def _split3(x):
    hi = x.astype(BF16)
    r1 = x - hi.astype(F32)
    mid = r1.astype(BF16)
    lo = (r1 - mid.astype(F32)).astype(BF16)
    return hi, mid, lo


def _gla(proj, w2, gb, gn, lb):
    b, s, _ = proj.shape

    def spec(name, width):
        return pl.BlockSpec((None, lb, width), lambda i, c, n=name, w=width: (i, c, SEC[n][0] // w))

    return pl.pallas_call(
        _gla_kernel,
        grid=(b, s // lb),
        in_specs=[spec("b_q", 256), spec("b_k", 256), spec("b_v", 512), spec("b_r", 512), spec("b_lr", LANES),
                  pl.BlockSpec((LANES, 256), lambda i, c: (0, 0)),
                  pl.BlockSpec((1, 256), lambda i, c: (0, 0)),
                  pl.BlockSpec((1, LANES), lambda i, c: (0, 0))],
        out_specs=pl.BlockSpec((None, lb, 512), lambda i, c: (i, c, 0)),
        out_shape=jax.ShapeDtypeStruct((b, s, 512), BF16),
        scratch_shapes=[pltpu.VMEM((GLA_HEADS, LANES, LANES), F32), pltpu.VMEM((lb, 512), F32),
                        pltpu.VMEM((lb, 256), F32)],
        compiler_params=_cparams(("parallel", "arbitrary")),
        name="gla",
    )(proj, proj, proj, proj, proj, w2, gb, gn)


def _kmean_kernel(a_ref, k_ref, o_ref):
    o_ref[...] = jnp.dot(a_ref[...], k_ref[...], preferred_element_type=F32).astype(o_ref.dtype)


def _moba_kmean(proj, avg):
    b, s, _ = proj.shape
    return pl.pallas_call(
        _kmean_kernel,
        grid=(b,),
        in_specs=[pl.BlockSpec((LANES, s), lambda i: (0, 0)),
                  pl.BlockSpec((None, s, 512), lambda i: (i, 0, SEC["c_k"][0] // 512))],
        out_specs=pl.BlockSpec((None, LANES, 512), lambda i: (i, 0, 0)),
        out_shape=jax.ShapeDtypeStruct((b, LANES, 512), BF16),
        compiler_params=_cparams(("parallel",)),
        name="moba_kmean",
    )(avg, proj)


def _moba_kernel(q_ref, k_ref, v_ref, km_ref, dp_ref, dd_ref, o_ref, lhs_ref, m_ref, acc_ref,
                 s0_ref, s1_ref, p0_ref, p1_ref, a0_ref, a1_ref):
    tq = ATT_TILE
    nh = MOBA_HEADS
    npair = nh // 2
    rows = nh * tq
    c = pl.program_id(1)

    lane = lax.broadcasted_iota(jnp.int32, (tq, LANES), 1)
    half = lane // MOBA_HEAD_DIM
    for h in range(nh):
        qb = q_ref[:, (h // 2) * LANES:(h // 2 + 1) * LANES]
        lhs_ref[h * tq:(h + 1) * tq, 0:LANES] = jnp.where(half == h % 2, qb, jnp.zeros_like(qb))

    nblk = k_ref.shape[0] // MOBA_BLOCK
    score = jnp.concatenate(
        [_qk(km_ref[0:nblk, p * LANES:(p + 1) * LANES], lhs_ref[2 * p * tq:2 * (p + 1) * tq, 0:LANES])
         for p in range(npair)], axis=1)
    blk = lax.broadcasted_iota(jnp.int32, (nblk, rows), 0)
    chosen = _topk_mask_rows(jnp.where(blk < c, score, NEG_INF), MOBA_TOPK) & (blk < c)
    chosen = jnp.concatenate([jnp.where(chosen, 1.0, 0.0), jnp.zeros((LANES - nblk, rows), F32)], axis=0)
    past = jnp.transpose(chosen) > 0.5
    lane2 = lax.broadcasted_iota(jnp.int32, (rows, LANES), 1)
    lhs_ref[:, LANES:2 * LANES] = jnp.where(past & (lane2 < c - 1), 0.0, NEG_INF).astype(BF16)

    def write_near_mask():
        lhs_ref[:, LANES:2 * LANES] = jnp.where(past | (lane2 == c), 0.0, NEG_INF).astype(BF16)

    o = _pipelined_attention(c, MOBA_BLOCK, npair, lhs_ref, write_near_mask, k_ref, v_ref, dp_ref, dd_ref,
                             (s0_ref, s1_ref), (p0_ref, p1_ref), (a0_ref, a1_ref), m_ref, acc_ref)
    for p in range(npair):
        o0 = o[2 * p * tq:(2 * p + 1) * tq]
        o1 = o[(2 * p + 1) * tq:(2 * p + 2) * tq]
        o_ref[:, p * LANES:(p + 1) * LANES] = jnp.where(half == 0, o0, o1).astype(o_ref.dtype)


def _moba_attention(proj, kmean, dprev, ddiag):
    b, s, _ = proj.shape
    tq = ATT_TILE
    rows = MOBA_HEADS * tq

    def col(name):
        return SEC[name][0] // 512

    return pl.pallas_call(
        _moba_kernel,
        grid=(b, s // tq),
        in_specs=[pl.BlockSpec((None, tq, 512), lambda i, c: (i, c, col("c_q"))),
                  _const_spec((None, s, 512), lambda i, c: (i, 0, col("c_k"))),
                  _const_spec((None, s, 512), lambda i, c: (i, 0, col("c_v"))),
                  _const_spec((None, LANES, 512), lambda i, c: (i, 0, 0)),
                  _const_spec((rows, tq), lambda i, c: (0, 0)),
                  _const_spec((rows, tq), lambda i, c: (0, 0))],
        out_specs=pl.BlockSpec((None, tq, 512), lambda i, c: (i, c, 0)),
        out_shape=jax.ShapeDtypeStruct((b, s, 512), BF16),
        scratch_shapes=[pltpu.VMEM((rows, 2 * LANES), BF16),
                        pltpu.VMEM((rows, LANES), F32), pltpu.VMEM((rows, 2 * LANES), F32),
                        pltpu.VMEM((rows, FAR_TILE), F32), pltpu.VMEM((rows, FAR_TILE), F32),
                        pltpu.VMEM((rows, FAR_TILE), BF16), pltpu.VMEM((rows, FAR_TILE), BF16),
                        pltpu.VMEM((rows, LANES), F32), pltpu.VMEM((rows, LANES), F32)],
        compiler_params=_cparams(("parallel", "arbitrary")),
        name="moba_attention",
    )(proj, proj, proj, kmean, dprev, ddiag)


def _merge_kernel(oa_ref, ob_ref, oc_ref, g0_ref, g1_ref, g2_ref, x_ref, wa_ref, wb_ref, wc_ref, wo_ref, nw_ref,
                  o_ref):
    def branch(o, w, g):
        return jax.nn.sigmoid(g[...].astype(F32)) * jnp.dot(o[...], w[...], preferred_element_type=F32)

    merged = branch(oa_ref, wa_ref, g0_ref) + branch(ob_ref, wb_ref, g1_ref) + branch(oc_ref, wc_ref, g2_ref)
    y = jnp.dot(merged.astype(BF16), wo_ref[...], preferred_element_type=F32)
    o_ref[...] = x_ref[...] + _rms(y, nw_ref[...])


def _merge(oa, ob, oc, proj, x, wa, wb, wc, wo, nw, tm):
    t, d = x.shape
    w = oa.shape[1]
    row = lambda width, j=0: pl.BlockSpec((tm, width), lambda i, j=j: (i, j))
    const = lambda shape: pl.BlockSpec(shape, lambda i: (0, 0))
    return pl.pallas_call(
        _merge_kernel,
        grid=(t // tm,),
        in_specs=[row(w), row(w), row(w), row(d, 0), row(d, 1), row(d, 2), row(d),
                  const((w, d)), const((w, d)), const((w, d)), const((d, d)), const((1, d))],
        out_specs=row(d),
        out_shape=jax.ShapeDtypeStruct((t, d), F32),
        compiler_params=_cparams(("parallel",)),
        name="merge_out",
    )(oa, ob, oc, proj, proj, proj, x, wa, wb, wc, wo, nw.reshape(1, d))


def _ffn_kernel(x_ref, npre_ref, wi_ref, wo_ref, npost_ref, o_ref):
    h = _rms(x_ref[...], npre_ref[...]).astype(BF16)
    hid = wo_ref.shape[0]
    acc = jnp.zeros(o_ref.shape, F32)
    for lo in range(0, hid, FFN_CHUNK):
        hi = min(lo + FFN_CHUNK, hid)
        gate = jnp.dot(h, wi_ref[:, lo:hi], preferred_element_type=F32)
        up = jnp.dot(h, wi_ref[:, hid + lo:hid + hi], preferred_element_type=F32)
        act = (gate * jax.nn.sigmoid(gate) * up).astype(BF16)
        acc = acc + jnp.dot(act, wo_ref[lo:hi, :], preferred_element_type=F32)
    o_ref[...] = x_ref[...] + _rms(acc, npost_ref[...])


def _ffn(x, npre, wi, wo, npost, tm):
    t, d = x.shape
    hid = wo.shape[0]
    assert hid % LANES == 0
    return pl.pallas_call(
        _ffn_kernel,
        grid=(t // tm,),
        in_specs=[pl.BlockSpec((tm, d), lambda i: (i, 0)),
                  pl.BlockSpec((1, d), lambda i: (0, 0)),
                  _const_spec((d, 2 * hid), lambda i: (0, 0)),
                  _const_spec((hid, d), lambda i: (0, 0)),
                  pl.BlockSpec((1, d), lambda i: (0, 0))],
        out_specs=pl.BlockSpec((tm, d), lambda i: (i, 0)),
        out_shape=jax.ShapeDtypeStruct((t, d), F32),
        compiler_params=_cparams(("parallel",)),
        name="ffn",
    )(x, npre.reshape(1, d), wi, wo, npost.reshape(1, d))


def _t5_bucket_table(dist):
    n = np.maximum(dist, 0)
    max_exact = NUM_BUCKETS // 2
    log_ratio = np.log(np.maximum(n, 1).astype(np.float32) / max_exact) / math.log(T5_MAX_DISTANCE / max_exact)
    large = max_exact + (log_ratio * (NUM_BUCKETS - max_exact)).astype(np.int32)
    return np.where(n < max_exact, n, np.minimum(large, NUM_BUCKETS - 1)).astype(np.int32)


def _bias_lookup(rel, dist):
    onehot = (jnp.asarray(_t5_bucket_table(dist))[..., None] == jnp.arange(NUM_BUCKETS)).astype(F32)
    return jnp.dot(onehot, (rel - rel[NUM_BUCKETS - 1]) * LOG2E, precision=lax.Precision.HIGHEST)


def _near_bias(rel, t):
    qi = np.arange(t)[:, None]
    ki = np.arange(t)[None, :]
    prev = _bias_lookup(rel, qi + t - ki)
    diag = jnp.where((qi >= ki)[..., None], _bias_lookup(rel, qi - ki), NEG_INF)
    flat = lambda b: b.transpose(2, 0, 1).reshape(-1, t)
    return flat(prev), flat(diag)


def _cmp_bias(rel, t):
    w = t // CMP_STRIDE
    qi = np.arange(t)[:, None]
    j = np.arange(LANES)[None, :]
    d = qi - CMP_STRIDE * (j - w) - (CMP_LEN - 1)
    b = jnp.where(((j < 2 * w) & (d >= 0))[..., None], _bias_lookup(rel, d),
                  jnp.asarray(np.where(j <= 2 * w, NEG_INF, 0.0) * np.ones_like(d), F32)[..., None])
    return b.transpose(2, 0, 1).reshape(-1, LANES).astype(BF16)


def _overlap(s, nc):
    n_cmp = (s - CMP_LEN) // CMP_STRIDE + 1
    cmp_end = np.arange(nc) * CMP_STRIDE + CMP_LEN - 1
    cmp_start = cmp_end - (CMP_LEN - 1)
    sb_start = np.arange(LANES) * SEL_BLOCK
    ov = (cmp_start[:, None] < sb_start[None, :] + SEL_BLOCK) & (cmp_end[:, None] >= sb_start[None, :])
    ov = ov & (np.arange(nc)[:, None] < n_cmp) & (np.arange(LANES)[None, :] < s // SEL_BLOCK)
    return jnp.asarray(ov, BF16)


def _pair_diag(w):
    z = jnp.zeros_like(w)
    return jnp.concatenate([jnp.concatenate([w, z], axis=-1), jnp.concatenate([z, w], axis=-1)], axis=-2)


def kernel(x, rel_bias, norm_mix_pre, norm_mix_post, norm_ffn_pre, norm_ffn_post, w_in, nsa_pe_k, nsa_pe_v, nsa_cmp_k_w1, nsa_cmp_k_w2, nsa_cmp_v_w1, nsa_cmp_v_w2, gla_gate_w2, gla_gate_b, gla_norm, w_branch_a, w_branch_b, w_branch_c, w_out, w_ffn_in, w_ffn_out):
    b, s, d = x.shape
    depth = w_in.shape[0]
    t = b * s
    dk = NSA_HEAD_DIM
    tq = ATT_TILE
    assert d == D_MODEL and w_in.shape[2] == D_IN
    assert WINDOW == 2 * tq and MOBA_BLOCK == tq and s % tq == 0
    assert SEL_TOPK <= s // SEL_BLOCK <= LANES and s // MOBA_BLOCK <= LANES
    nc = s // CMP_STRIDE
    assert nc % LANES == 0

    w_in_p = _layout_w_in(w_in)
    rel_a = rel_bias[:, :NSA_HEADS]
    rel_c = rel_bias[:, NSA_HEADS:]
    dprev_a, ddiag_a = _near_bias(rel_a, tq)
    dprev_c, ddiag_c = _near_bias(rel_c, tq)
    tcmp = _cmp_bias(rel_a, tq)
    overlap = _overlap(s, nc)
    wtab = jnp.asarray(np.where(np.arange(tq)[None, :] > np.arange(tq)[:, None], 0.0, NEG_INF), F32)
    avg = jnp.asarray((np.arange(LANES)[:, None] == np.arange(s)[None, :] // MOBA_BLOCK) / MOBA_BLOCK, BF16)

    def cmp_w1(w1):
        w = _pair_diag(w1.reshape(depth, 2, CMP_STRIDE, dk, dk))
        return w.reshape(depth, 2, CMP_STRIDE * 2 * dk, 2 * dk)

    def cmp_pe(pe):
        p2 = jnp.concatenate([pe, pe], axis=-1).reshape(depth, 2, 1, CMP_STRIDE * 2 * dk)
        return jnp.broadcast_to(p2, (depth, 2, 8, CMP_STRIDE * 2 * dk))

    cw1 = jnp.stack([cmp_w1(nsa_cmp_k_w1), cmp_w1(nsa_cmp_v_w1)], axis=1).astype(BF16)
    cw2 = jnp.stack([_pair_diag(nsa_cmp_k_w2), _pair_diag(nsa_cmp_v_w2)], axis=1).astype(BF16)
    cpe = jnp.stack([cmp_pe(nsa_pe_k), cmp_pe(nsa_pe_v)], axis=1).astype(BF16)

    gw2 = jnp.concatenate([gla_gate_w2, jnp.zeros((depth, LANES - GLA_GATE_RANK, gla_gate_w2.shape[2]), F32)],
                          axis=1).astype(BF16)
    wa = (w_branch_a.reshape(depth, NSA_KV_GROUPS, NSA_REP, dk, d).transpose(0, 2, 1, 3, 4)
          .reshape(depth, NSA_HEADS * dk, d).astype(BF16))
    wb = w_branch_b.astype(BF16)
    wc = w_branch_c.astype(BF16)
    wo = w_out.astype(BF16)
    wi = w_ffn_in.astype(BF16)
    wf = w_ffn_out.astype(BF16)

    xf = x.reshape(t, d)
    for layer in range(depth):
        proj = _norm_matmul(xf, norm_mix_pre[layer], w_in_p[layer], min(512, t)).reshape(b, s, NP_COLS)
        kc0, vc0 = SEC["a_kc"][0], SEC["a_vc"][0]
        xkv = jnp.stack([proj[:, :, kc0:kc0 + LANES], proj[:, :, vc0:vc0 + LANES]])
        xkv = xkv.reshape(2, b, nc, CMP_STRIDE * LANES)
        cmp = _compress(xkv, cpe[layer], cw1[layer], cw2[layer])
        o_a = _nsa_attention(proj, cmp[0], cmp[1], overlap, tcmp, dprev_a, ddiag_a, wtab)
        o_b = _gla(proj, gw2[layer], gla_gate_b[layer].reshape(1, -1), gla_norm[layer].reshape(1, -1),
                   min(256, s))
        kmean = _moba_kmean(proj, avg)
        o_c = _moba_attention(proj, kmean, dprev_c, ddiag_c)
        xf = _merge(o_a.reshape(t, -1), o_b.reshape(t, -1), o_c.reshape(t, -1), proj.reshape(t, NP_COLS), xf,
                    wa[layer], wb[layer], wc[layer], wo[layer], norm_mix_post[layer], min(512, t))
        xf = _ffn(xf, norm_ffn_pre[layer], wi[layer], wf[layer], norm_ffn_post[layer], min(512, t))
    return xf.reshape(b, s, d)
```

```python
import math

import numpy as np
import jax
import jax.numpy as jnp
from jax import lax
from jax.experimental import pallas as pl
from jax.experimental.pallas import tpu as pltpu

F32 = jnp.float32
BF16 = jnp.bfloat16

NORM_EPS = 1e-6
NEG_INF = -1e30
TOP_BONUS = 1e9
NUM_BUCKETS = 32
T5_MAX_DISTANCE = 128
NSA_HEADS = 8
NSA_KV_GROUPS = 2
NSA_REP = NSA_HEADS // NSA_KV_GROUPS
NSA_HEAD_DIM = 64
CMP_LEN = 32
CMP_STRIDE = 16
SEL_BLOCK = 64
SEL_TOPK = 16
WINDOW = 512
N_NSA_BRANCH = 3
GLA_HEADS = 4
GLA_KEY_DIM = 64
GLA_GATE_RANK = 16
GLA_GATE_TAU = 16.0
GLA_CHUNK = 64
GLA_FACTORISED_RANGE = 80.0
MOBA_HEADS = 8
MOBA_HEAD_DIM = 64
MOBA_BLOCK = 256
MOBA_TOPK = 3
LOG2E = math.log2(math.e)

LANES = 128
ATT_TILE = 256
FAR_TILE = 2 * ATT_TILE
FFN_CHUNK = 512
PROJ_CHUNK = 1536
VMEM_LIMIT = 56 * 1024 * 1024

D_MODEL = 1024
SEC = {}
_off = 0
for _name, _w in (("mg", 3 * D_MODEL), ("a_q", 512), ("c_q", 512), ("c_k", 512), ("c_v", 512),
                  ("b_v", 512), ("b_r", 512), ("b_q", 256), ("b_k", 256),
                  ("a_kc", 128), ("a_vc", 128), ("a_ks", 128), ("a_vs", 128), ("a_kw", 128), ("a_vw", 128),
                  ("a_g", 128), ("b_lr", 128)):
    SEC[_name] = (_off, _w)
    _off += _w
NP_COLS = _off

_SRC_NAMES = ("a_q", "a_kc", "a_vc", "a_ks", "a_vs", "a_kw", "a_vw", "a_g", "b_q", "b_k", "b_v", "b_r", "b_lr",
              "c_q", "c_k", "c_v", "mg")
_SRC_SIZES = (512, 128, 128, 128, 128, 128, 128, 24, 256, 256, 512, 512, 16, 512, 512, 512, 3 * D_MODEL)
_SRC_OFF = dict(zip(_SRC_NAMES, np.cumsum((0,) + _SRC_SIZES[:-1]).tolist()))
_SRC_W = dict(zip(_SRC_NAMES, _SRC_SIZES))
D_IN = int(sum(_SRC_SIZES))


def _layout_w_in(w_in):
    depth, d, _ = w_in.shape

    def src(name, lo=0, hi=None):
        hi = _SRC_W[name] if hi is None else hi
        return w_in[:, :, _SRC_OFF[name] + lo:_SRC_OFF[name] + hi]

    parts = []
    for name, (_, width) in SEC.items():
        if name == "a_q":
            for r in range(NSA_REP):
                for g in range(NSA_KV_GROUPS):
                    h = g * NSA_REP + r
                    parts.append(src(name, h * NSA_HEAD_DIM, (h + 1) * NSA_HEAD_DIM) * (NSA_HEAD_DIM ** -0.5 * LOG2E))
        elif name == "c_q":
            parts.append(src(name) * (MOBA_HEAD_DIM ** -0.5 * LOG2E))
        else:
            parts.append(src(name))
            if _SRC_W[name] < width:
                parts.append(jnp.zeros((depth, d, width - _SRC_W[name]), w_in.dtype))
    return jnp.concatenate(parts, axis=2).astype(BF16)


def _cparams(sem):
    return pltpu.CompilerParams(dimension_semantics=sem, vmem_limit_bytes=VMEM_LIMIT)


def _const_spec(shape, index_map):
    return pl.BlockSpec(shape, index_map, pipeline_mode=pl.Buffered(1))


def _rms(y, w):
    return y * lax.rsqrt(jnp.mean(y * y, axis=-1, keepdims=True) + NORM_EPS) * w


def _norm_matmul_kernel(x_ref, nw_ref, w_ref, o_ref):
    h = _rms(x_ref[...], nw_ref[...]).astype(BF16)
    for lo in range(0, w_ref.shape[1], PROJ_CHUNK):
        hi = lo + PROJ_CHUNK
        o_ref[:, lo:hi] = jnp.dot(h, w_ref[:, lo:hi], preferred_element_type=F32).astype(o_ref.dtype)


def _norm_matmul(x, nw, w, tm):
    t, d = x.shape
    n = w.shape[1]
    assert n % PROJ_CHUNK == 0
    return pl.pallas_call(
        _norm_matmul_kernel,
        grid=(t // tm,),
        in_specs=[pl.BlockSpec((tm, d), lambda i: (i, 0)),
                  pl.BlockSpec((1, d), lambda i: (0, 0)),
                  _const_spec((d, n), lambda i: (0, 0))],
        out_specs=pl.BlockSpec((tm, n), lambda i: (i, 0)),
        out_shape=jax.ShapeDtypeStruct((t, n), BF16),
        compiler_params=_cparams(("parallel",)),
        name="norm_proj",
    )(x, nw.reshape(1, d), w)


def _compress_kernel(x_ref, pe_ref, w1_ref, w2_ref, o_ref):
    x = x_ref[...]
    nc = x.shape[0]
    w1t = w1_ref[0]
    w1b = w1_ref[1]
    a = jnp.dot(x, w1t, preferred_element_type=F32)
    b = jnp.dot(x, w1b, preferred_element_type=F32)
    pe = pe_ref[...]
    pe_term = (jnp.dot(pe[0], w1t, preferred_element_type=F32)
               + jnp.dot(pe[1], w1b, preferred_element_type=F32))[0:1]
    pre = a + pltpu.roll(b, nc - 1, 0) + pe_term
    hid = jax.nn.gelu(pre)
    o_ref[...] = jnp.dot(hid.astype(BF16), w2_ref[...], preferred_element_type=F32).astype(o_ref.dtype)


def _compress(xkv, pe, w1, w2):
    _, b, nc, kw = xkv.shape
    return pl.pallas_call(
        _compress_kernel,
        grid=(2, b),
        in_specs=[pl.BlockSpec((None, None, nc, kw), lambda s, i: (s, i, 0, 0)),
                  pl.BlockSpec((None, 2, 8, kw), lambda s, i: (s, 0, 0, 0)),
                  pl.BlockSpec((None, 2, kw, LANES), lambda s, i: (s, 0, 0, 0)),
                  pl.BlockSpec((None, LANES, LANES), lambda s, i: (s, 0, 0))],
        out_specs=pl.BlockSpec((None, None, nc, LANES), lambda s, i: (s, i, 0, 0)),
        out_shape=jax.ShapeDtypeStruct((2, b, nc, LANES), BF16),
        compiler_params=_cparams(("parallel", "parallel")),
        name="nsa_compress",
    )(xkv, pe, w1, w2)


def _with_ones(v):
    return jnp.concatenate([v, jnp.ones(v.shape, v.dtype)], axis=1)


def _softmax_values(s, v1):
    p = jnp.exp2(s - jnp.max(s, axis=-1, keepdims=True)).astype(BF16)
    r = jnp.dot(p, v1, preferred_element_type=F32)
    return r[:, 0:LANES] / r[:, LANES:2 * LANES]


def _qk(lhs, rhs):
    return lax.dot_general(lhs, rhs, (((1,), (1,)), ((), ())), preferred_element_type=F32)


def _lane_onehot(width, lane_idx):
    klane = lax.broadcasted_iota(jnp.int32, (width, LANES), 1)
    return (klane == lane_idx).astype(BF16)


def _pipelined_attention(c, block, groups, lhs_ref, write_near_mask, k_ref, v_ref, dp_ref, dd_ref,
                         s_refs, p_refs, a_refs, m_ref, acc_ref, t_refs=None):
    tq = ATT_TILE
    tk = FAR_TILE
    rows = lhs_ref.shape[0]
    grows = rows // groups
    j_max = k_ref.shape[0] // tk - 1
    per = tk // block
    n_far = (jnp.maximum(c - 1, 0) * tq + tk - 1) // tk
    n_loop = 2 * ((n_far + 1) // 2)
    krow = lax.broadcasted_iota(jnp.int32, (tk, LANES), 0) // block
    klane = lax.broadcasted_iota(jnp.int32, (tk, LANES), 1)

    m_ref[...] = jnp.full(m_ref.shape, NEG_INF, F32)
    acc_ref[...] = jnp.zeros(acc_ref.shape, F32)

    def far_rows(j):
        start = pl.multiple_of(jnp.clip(j, 0, j_max) * tk, tk)
        return lambda ref, g: ref[pl.ds(start, tk), g * LANES:(g + 1) * LANES]

    def near_rows(ref, g):
        cols = slice(g * LANES, (g + 1) * LANES)
        prev = ref[pl.ds(pl.multiple_of(jnp.maximum(c - 1, 0) * tq, tq), tq), cols]
        return jnp.concatenate([prev, ref[pl.ds(pl.multiple_of(c * tq, tq), tq), cols]], axis=0)

    def value_stage(slot, window):
        for g in range(groups):
            gr = slice(g * grows, (g + 1) * grows)
            acc_ref[gr, :] = (jnp.tile(a_refs[slot][gr, :], (1, 2)) * acc_ref[gr, :]
                              + jnp.dot(p_refs[slot][gr, :], _with_ones(window(v_ref, g)),
                                        preferred_element_type=F32))

    def softmax_stage(slot, table=None):
        s = s_refs[slot][...]
        if table is not None:
            s = s + table
        if table is not None or t_refs is None:
            row_max = jnp.max(s, axis=-1, keepdims=True)
        else:
            row_max = t_refs[slot][...]
        m_prev = m_ref[...]
        m_new = jnp.maximum(m_prev, row_max)
        a_refs[slot][...] = jnp.exp2(m_prev - m_new)
        p_refs[slot][...] = jnp.exp2(s - jnp.tile(m_new, (1, tk // LANES))).astype(BF16)
        m_ref[...] = m_new

    def score_stage(slot, window, lanes):
        onehot = (klane == lanes).astype(BF16)
        for g in range(groups):
            gr = slice(g * grows, (g + 1) * grows)
            s = _qk(lhs_ref[gr, :], jnp.concatenate([window(k_ref, g), onehot], axis=1))
            s_refs[slot][gr, :] = s
            if t_refs is not None:
                t_refs[slot][gr, :] = jnp.broadcast_to(jnp.max(s, axis=-1, keepdims=True), (grows, LANES))

    odd = n_far % 2

    def far_lanes(i):
        return jnp.where((i >= odd) & (i - odd < n_far), krow + (i - odd) * per, LANES - 1)

    def loop_stage(i, slot):
        value_stage(slot, far_rows(i - 2 - odd))
        softmax_stage(1 - slot)
        score_stage(slot, far_rows(i - odd), far_lanes(i))

    def body(ii, carry):
        loop_stage(2 * ii, 0)
        loop_stage(2 * ii + 1, 1)
        return carry

    @pl.when(n_loop == 0)
    def _():
        s_refs[1][...] = jnp.full((rows, tk), -jnp.inf, F32)
        if t_refs is not None:
            t_refs[1][...] = jnp.full((rows, LANES), -jnp.inf, F32)
        p_refs[0][...] = jnp.zeros((rows, tk), BF16)
        a_refs[0][...] = jnp.ones((rows, LANES), F32)

    @pl.when((n_loop > 0) & (odd == 0))
    def _():
        score_stage(0, far_rows(0), far_lanes(0))
        softmax_stage(0)
        score_stage(1, far_rows(1), far_lanes(1))

    @pl.when(odd == 1)
    def _():
        p_refs[0][...] = jnp.zeros((rows, tk), BF16)
        a_refs[0][...] = jnp.ones((rows, LANES), F32)
        score_stage(1, far_rows(0), far_lanes(1))

    lax.fori_loop(1, n_loop // 2, body, 0)

    value_stage(0, far_rows(n_far - 2))
    softmax_stage(1)
    write_near_mask()
    near_lanes = jnp.where((krow >= tq // block) | (c >= 1), krow + (c - 1) * (tq // block), LANES - 1)
    score_stage(0, near_rows, near_lanes)
    value_stage(1, far_rows(n_far - 1))
    softmax_stage(0, jnp.concatenate([dp_ref[...], dd_ref[...]], axis=1))
    value_stage(0, near_rows)
    return acc_ref[:, 0:LANES] / acc_ref[:, LANES:2 * LANES]


def _topk_mask(vals, k):
    lane = lax.broadcasted_iota(jnp.int32, vals.shape, 1).astype(F32)
    sel = jnp.zeros(vals.shape, jnp.bool_)
    for _ in range(k):
        m = jnp.max(vals, axis=-1, keepdims=True)
        idx = jnp.min(jnp.where(vals == m, lane, float(LANES)), axis=-1, keepdims=True)
        hit = lane == idx
        sel = jnp.logical_or(sel, hit)
        vals = jnp.where(hit, -jnp.inf, vals)
    return sel


def _topk_mask_rows(vals, k):
    row = lax.broadcasted_iota(jnp.int32, vals.shape, 0).astype(F32)
    sel = jnp.zeros(vals.shape, jnp.bool_)
    for _ in range(k):
        m = jnp.max(vals, axis=0, keepdims=True)
        idx = jnp.min(jnp.where(vals == m, row, float(vals.shape[0])), axis=0, keepdims=True)
        hit = row == idx
        sel = jnp.logical_or(sel, hit)
        vals = jnp.where(hit, -jnp.inf, vals)
    return sel


def _nsa_kernel(q_ref, ks_ref, vs_ref, kw_ref, vw_ref, g_ref, kc_ref, vc_ref, ov_ref, tc_ref, dp_ref, dd_ref, wt_ref,
                o_ref, lhs_ref, nm_ref, m_ref, acc_ref, out_ref, s0_ref, s1_ref, p0_ref, p1_ref, a0_ref, a1_ref, t0_ref,
                t1_ref):
    tq = ATT_TILE
    nh = NSA_HEADS
    rows = nh * tq
    c = pl.program_id(1)
    t0 = c * tq
    ncp = kc_ref.shape[0]
    wcols = 2 * (tq // CMP_STRIDE)

    lane = lax.broadcasted_iota(jnp.int32, (tq, LANES), 1)
    rowi = lax.broadcasted_iota(jnp.int32, (tq, LANES), 0)
    half = lane // NSA_HEAD_DIM

    q = q_ref[...]
    for g in range(NSA_KV_GROUPS):
        for r in range(NSA_REP):
            h = g * NSA_REP + r
            qb = q[:, r * LANES:(r + 1) * LANES]
            lhs_ref[h * tq:(h + 1) * tq, 0:LANES] = jnp.where(half == g, qb, jnp.zeros_like(qb))
    qrows = lhs_ref[:, 0:LANES]

    gates = jax.nn.sigmoid(g_ref[...].astype(F32))

    def gate_col(h, br):
        col = h * N_NSA_BRANCH + br
        return gates[:, col:col + 1]

    prev_start = pl.multiple_of(jnp.maximum(c - 1, 0) * tq, tq)
    diag_start = pl.multiple_of(c * tq, tq)

    ii = lax.broadcasted_iota(jnp.int32, (ncp, LANES), 0) - (c * (tq // CMP_STRIDE) - tq // CMP_STRIDE)
    ww = lax.broadcasted_iota(jnp.int32, (ncp, LANES), 1)
    place = (((ww < wcols) & (ii == ww)) | ((ww == wcols) & (ii >= wcols))).astype(BF16)
    s = _qk(jnp.concatenate([qrows, tc_ref[...]], axis=1), jnp.concatenate([kc_ref[...], place], axis=1))
    m = jnp.max(s, axis=-1, keepdims=True)
    p = jnp.exp2(s - m)
    r1 = jnp.dot(p.astype(BF16), jnp.concatenate([_with_ones(vc_ref[...]), ov_ref[...]], axis=1),
                 preferred_element_type=F32)
    has_key = m > 0.5 * NEG_INF
    inv_l = jnp.where(has_key, 1.0 / r1[:, LANES:2 * LANES], 0.0)
    o_cmp = r1[:, 0:LANES] * inv_l
    imp_h = r1[:, 2 * LANES:3 * LANES] * inv_l
    for h in range(nh):
        sl = slice(h * tq, (h + 1) * tq)
        out_ref[sl, :] = gate_col(h, 0) * o_cmp[sl]

    cur = (rowi + t0) // SEL_BLOCK
    forced = (lane == 0) | (lane == cur) | (lane == cur - 1)
    far_blocks = (c - 1) * (tq // SEL_BLOCK)
    for g in range(NSA_KV_GROUPS):
        imp = imp_h[g * NSA_REP * tq:(g * NSA_REP + 1) * tq]
        for r in range(1, NSA_REP):
            imp = imp + imp_h[(g * NSA_REP + r) * tq:(g * NSA_REP + r + 1) * tq]
        imp = jnp.where(forced, TOP_BONUS, jnp.where(lane <= cur, imp, NEG_INF))
        sel = _topk_mask(imp, SEL_TOPK)
        nm_ref[g] = jnp.where(sel, 0.0, NEG_INF).astype(BF16)
        far_mask = jnp.where(sel & (lane < far_blocks), 0.0, NEG_INF).astype(BF16)
        for r in range(NSA_REP):
            h = g * NSA_REP + r
            lhs_ref[h * tq:(h + 1) * tq, LANES:2 * LANES] = far_mask

    wpat = jnp.where(lax.broadcasted_iota(jnp.int32, (rows, LANES), 1) == 1, NEG_INF, 0.0).astype(BF16)
    back2_start = pl.multiple_of(jnp.maximum(c - 2, 0) * tq, tq)
    kwin = jnp.concatenate([kw_ref[pl.ds(back2_start, tq), :], kw_ref[pl.ds(prev_start, tq), :],
                            kw_ref[pl.ds(diag_start, tq), :]], axis=0)
    vwin = jnp.concatenate([vw_ref[pl.ds(back2_start, tq), :], vw_ref[pl.ds(prev_start, tq), :],
                            vw_ref[pl.ds(diag_start, tq), :]], axis=0)
    hot = jnp.concatenate([_lane_onehot(tq, jnp.where(c >= 2, 0, 1)), _lane_onehot(tq, jnp.where(c >= 1, 0, 1)),
                           _lane_onehot(tq, 0)], axis=0)
    s = _qk(jnp.concatenate([qrows, wpat], axis=1), jnp.concatenate([kwin, hot], axis=1))
    s = s + jnp.concatenate([jnp.tile(wt_ref[...], (nh, 1)), dp_ref[...], dd_ref[...]], axis=1)
    o_win = _softmax_values(s, _with_ones(vwin))
    for h in range(nh):
        sl = slice(h * tq, (h + 1) * tq)
        out_ref[sl, :] = out_ref[sl, :] + gate_col(h, 2) * o_win[sl]

    def write_near_mask():
        for h in range(nh):
            lhs_ref[h * tq:(h + 1) * tq, LANES:2 * LANES] = nm_ref[h // NSA_REP]

    o_sel = _pipelined_attention(c, SEL_BLOCK, 1, lhs_ref, write_near_mask, ks_ref, vs_ref, dp_ref, dd_ref,
                                 (s0_ref, s1_ref), (p0_ref, p1_ref), (a0_ref, a1_ref), m_ref, acc_ref,
                                 (t0_ref, t1_ref))

    for r in range(NSA_REP):
        h0, h1 = r, NSA_REP + r
        o0 = out_ref[h0 * tq:(h0 + 1) * tq, :] + gate_col(h0, 1) * o_sel[h0 * tq:(h0 + 1) * tq]
        o1 = out_ref[h1 * tq:(h1 + 1) * tq, :] + gate_col(h1, 1) * o_sel[h1 * tq:(h1 + 1) * tq]
        o_ref[:, r * LANES:(r + 1) * LANES] = jnp.where(half == 0, o0, o1).astype(o_ref.dtype)


def _nsa_attention(proj, kcmp, vcmp, overlap, tcmp, dprev, ddiag, wtab):
    b, s, _ = proj.shape
    tq = ATT_TILE
    ncp = kcmp.shape[1]
    rows = NSA_HEADS * tq

    def col(name, width):
        return SEC[name][0] // width

    full = lambda name: _const_spec((None, s, LANES), lambda i, c, n=name: (i, 0, col(n, LANES)))
    return pl.pallas_call(
        _nsa_kernel,
        grid=(b, s // tq),
        in_specs=[pl.BlockSpec((None, tq, 512), lambda i, c: (i, c, col("a_q", 512))),
                  full("a_ks"), full("a_vs"), full("a_kw"), full("a_vw"),
                  pl.BlockSpec((None, tq, LANES), lambda i, c: (i, c, col("a_g", LANES))),
                  _const_spec((None, ncp, LANES), lambda i, c: (i, 0, 0)),
                  _const_spec((None, ncp, LANES), lambda i, c: (i, 0, 0)),
                  _const_spec((ncp, LANES), lambda i, c: (0, 0)),
                  _const_spec((rows, LANES), lambda i, c: (0, 0)),
                  _const_spec((rows, tq), lambda i, c: (0, 0)),
                  _const_spec((rows, tq), lambda i, c: (0, 0)),
                  _const_spec((tq, tq), lambda i, c: (0, 0))],
        out_specs=pl.BlockSpec((None, tq, 512), lambda i, c: (i, c, 0)),
        out_shape=jax.ShapeDtypeStruct((b, s, 512), BF16),
        scratch_shapes=[pltpu.VMEM((rows, 2 * LANES), BF16),
                        pltpu.VMEM((NSA_KV_GROUPS, tq, LANES), BF16),
                        pltpu.VMEM((rows, LANES), F32), pltpu.VMEM((rows, 2 * LANES), F32),
                        pltpu.VMEM((rows, LANES), F32),
                        pltpu.VMEM((rows, FAR_TILE), F32), pltpu.VMEM((rows, FAR_TILE), F32),
                        pltpu.VMEM((rows, FAR_TILE), BF16), pltpu.VMEM((rows, FAR_TILE), BF16),
                        pltpu.VMEM((rows, LANES), F32), pltpu.VMEM((rows, LANES), F32),
                        pltpu.VMEM((rows, LANES), F32), pltpu.VMEM((rows, LANES), F32)],
        compiler_params=_cparams(("parallel", "arbitrary")),
        name="nsa_attention",
    )(proj, proj, proj, proj, proj, proj, kcmp, vcmp, overlap, tcmp, dprev, ddiag, wtab)


def _gla_fast(q_ref, k_ref, v_ref, r_ref, lr_ref, w2_ref, gb_ref, gn_ref, o_ref, st_ref, oi_ref, bc_ref):
    ch = GLA_CHUNK
    lb = q_ref.shape[0]
    nch = lb // ch

    x = jnp.dot(lr_ref[...], w2_ref[...], preferred_element_type=F32) + gb_ref[...]
    log_a = (jnp.minimum(x, 0.0) - jnp.log1p(jnp.exp(-jnp.abs(x)))) / GLA_GATE_TAU

    ti = lax.broadcasted_iota(jnp.int32, (lb, lb), 0)
    tj = lax.broadcasted_iota(jnp.int32, (lb, lb), 1)
    same_chunk = (ti // ch) == (tj // ch)
    causal = same_chunk & (tj <= ti)
    g_hi, g_mid, g_lo = _split3(log_a)

    def chunk_sums(w):
        return (jnp.dot(w, g_hi, preferred_element_type=F32) + jnp.dot(w, g_mid, preferred_element_type=F32)
                + jnp.dot(w, g_lo, preferred_element_type=F32))

    bcum = chunk_sums(causal.astype(BF16))
    btot = chunk_sums(same_chunk.astype(BF16))
    bc_ref[...] = bcum
    half = lax.broadcasted_iota(jnp.int32, (lb, LANES), 1) // GLA_KEY_DIM
    scale = GLA_KEY_DIM ** -0.5

    for p in range(GLA_HEADS // 2):
        cols = slice(p * LANES, (p + 1) * LANES)
        bc = bcum[:, cols]
        bt = btot[:, cols]
        kf = k_ref[:, cols].astype(F32)
        qe = q_ref[:, cols].astype(F32) * scale * jnp.exp(bc)
        kinv = (kf * jnp.exp(-bc)).astype(BF16)
        klast = (kf * jnp.exp(bt - bc)).astype(BF16)
        decay = jnp.exp(bt)
        for a in range(2):
            h = 2 * p + a
            hc = slice(h * LANES, (h + 1) * LANES)
            qa = jnp.where(half == a, qe, 0.0).astype(BF16)
            v = v_ref[:, hc]
            attn = jnp.where(causal, _qk(qa, kinv), 0.0).astype(BF16)
            o_intra = jnp.dot(attn, v, preferred_element_type=F32)
            outs = []
            st = st_ref[h]
            for cc in range(nch):
                sl = slice(cc * ch, (cc + 1) * ch)
                outs.append(_qk(qa[sl], st.astype(BF16)))
                upd = lax.dot_general(v[sl], klast[sl], (((0,), (0,)), ((), ())), preferred_element_type=F32)
                st = st * decay[cc * ch:cc * ch + 1, :] + upd
            st_ref[h] = st
            o_inter = jnp.concatenate(outs, axis=0)
            oi_ref[:, hc] = o_inter
            _gla_finish(o_inter + o_intra, hc, r_ref, gn_ref, o_ref)
    return jnp.min(btot)


def _gla_finish(o, hc, r_ref, gn_ref, o_ref):
    rg = r_ref[:, hc].astype(F32)
    o_ref[:, hc] = (_rms(o, gn_ref[...]) * (rg * jax.nn.sigmoid(rg))).astype(o_ref.dtype)


def _gla_exact_intra(q_ref, k_ref, v_ref, r_ref, gn_ref, o_ref, oi_ref, bc_ref):
    ch = GLA_CHUNK
    nch = q_ref.shape[0] // ch
    scale = GLA_KEY_DIM ** -0.5
    rowc = lax.broadcasted_iota(jnp.int32, (ch, LANES), 0)
    lanec = lax.broadcasted_iota(jnp.int32, (ch, LANES), 1)

    def chunk_step(cc, carry):
        sl = pl.ds(pl.multiple_of(cc * ch, ch), ch)
        for p in range(GLA_HEADS // 2):
            cols = slice(p * LANES, (p + 1) * LANES)
            qc = q_ref[sl, cols].astype(F32) * scale
            kc = k_ref[sl, cols].astype(F32)
            bcc = bc_ref[sl, cols]
            v0 = v_ref[sl, 2 * p * LANES:(2 * p + 1) * LANES].astype(F32)
            v1 = v_ref[sl, (2 * p + 1) * LANES:(2 * p + 2) * LANES].astype(F32)
            acc0 = jnp.zeros((ch, LANES), F32)
            acc1 = jnp.zeros((ch, LANES), F32)
            for j in range(ch):
                w = jnp.where(rowc >= j, jnp.exp(jnp.minimum(bcc - bcc[j:j + 1], 0.0)), 0.0)
                t = qc * (kc[j:j + 1] * w)
                a0 = jnp.sum(jnp.where(lanec < GLA_KEY_DIM, t, 0.0), axis=-1, keepdims=True)
                a1 = jnp.sum(jnp.where(lanec >= GLA_KEY_DIM, t, 0.0), axis=-1, keepdims=True)
                acc0 = acc0 + a0 * jnp.broadcast_to(v0[j:j + 1], (ch, LANES))
                acc1 = acc1 + a1 * jnp.broadcast_to(v1[j:j + 1], (ch, LANES))
            oi_ref[sl, 2 * p * LANES:(2 * p + 1) * LANES] += acc0
            oi_ref[sl, (2 * p + 1) * LANES:(2 * p + 2) * LANES] += acc1
        return carry

    lax.fori_loop(0, nch, chunk_step, 0)
    for h in range(GLA_HEADS):
        hc = slice(h * LANES, (h + 1) * LANES)
        _gla_finish(oi_ref[:, hc], hc, r_ref, gn_ref, o_ref)


def _gla_kernel(q_ref, k_ref, v_ref, r_ref, lr_ref, w2_ref, gb_ref, gn_ref, o_ref, st_ref, oi_ref, bc_ref):
    @pl.when(pl.program_id(0) == 0)
    def _():
        st_ref[...] = jnp.zeros(st_ref.shape, F32)

    nb = q_ref.shape[0]
    low = [_gla_fast(q_ref.at[bi], k_ref.at[bi], v_ref.at[bi], r_ref.at[bi], lr_ref.at[bi], w2_ref, gb_ref, gn_ref,
                     o_ref.at[bi], st_ref.at[bi], oi_ref.at[bi], bc_ref.at[bi]) for bi in range(nb)]
    for bi in range(nb):
        @pl.when(low[bi] < -GLA_FACTORISED_RANGE)
        def _():
            _gla_exact_intra(q_ref.at[bi], k_ref.at[bi], v_ref.at[bi], r_ref.at[bi], gn_ref, o_ref.at[bi],
                             oi_ref.at[bi], bc_ref.at[bi])


def _split3(x):
    hi = x.astype(BF16)
    r1 = x - hi.astype(F32)
    mid = r1.astype(BF16)
    lo = (r1 - mid.astype(F32)).astype(BF16)
    return hi, mid, lo


def _gla(proj, w2, gb, gn, lb):
    b, s, _ = proj.shape

    def spec(name, width):
        return pl.BlockSpec((b, lb, width), lambda c, n=name, w=width: (0, c, SEC[n][0] // w))

    return pl.pallas_call(
        _gla_kernel,
        grid=(s // lb,),
        in_specs=[spec("b_q", 256), spec("b_k", 256), spec("b_v", 512), spec("b_r", 512), spec("b_lr", LANES),
                  pl.BlockSpec((LANES, 256), lambda c: (0, 0)),
                  pl.BlockSpec((1, 256), lambda c: (0, 0)),
                  pl.BlockSpec((1, LANES), lambda c: (0, 0))],
        out_specs=pl.BlockSpec((b, lb, 512), lambda c: (0, c, 0)),
        out_shape=jax.ShapeDtypeStruct((b, s, 512), BF16),
        scratch_shapes=[pltpu.VMEM((b, GLA_HEADS, LANES, LANES), F32), pltpu.VMEM((b, lb, 512), F32),
                        pltpu.VMEM((b, lb, 256), F32)],
        compiler_params=_cparams(("arbitrary",)),
        name="gla",
    )(proj, proj, proj, proj, proj, w2, gb, gn)


def _kmean_kernel(a_ref, k_ref, o_ref):
    o_ref[...] = jnp.dot(a_ref[...], k_ref[...], preferred_element_type=F32).astype(o_ref.dtype)


def _moba_kmean(proj, avg):
    b, s, _ = proj.shape
    return pl.pallas_call(
        _kmean_kernel,
        grid=(b,),
        in_specs=[pl.BlockSpec((LANES, s), lambda i: (0, 0)),
                  pl.BlockSpec((None, s, 512), lambda i: (i, 0, SEC["c_k"][0] // 512))],
        out_specs=pl.BlockSpec((None, LANES, 512), lambda i: (i, 0, 0)),
        out_shape=jax.ShapeDtypeStruct((b, LANES, 512), BF16),
        compiler_params=_cparams(("parallel",)),
        name="moba_kmean",
    )(avg, proj)


def _moba_kernel(q_ref, k_ref, v_ref, km_ref, dp_ref, dd_ref, o_ref, lhs_ref, m_ref, acc_ref,
                 s0_ref, s1_ref, p0_ref, p1_ref, a0_ref, a1_ref):
    tq = ATT_TILE
    nh = MOBA_HEADS
    npair = nh // 2
    rows = nh * tq
    c = pl.program_id(1)

    lane = lax.broadcasted_iota(jnp.int32, (tq, LANES), 1)
    half = lane // MOBA_HEAD_DIM
    for h in range(nh):
        qb = q_ref[:, (h // 2) * LANES:(h // 2 + 1) * LANES]
        lhs_ref[h * tq:(h + 1) * tq, 0:LANES] = jnp.where(half == h % 2, qb, jnp.zeros_like(qb))

    nblk = k_ref.shape[0] // MOBA_BLOCK
    score = jnp.concatenate(
        [_qk(km_ref[0:nblk, p * LANES:(p + 1) * LANES], lhs_ref[2 * p * tq:2 * (p + 1) * tq, 0:LANES])
         for p in range(npair)], axis=1)
    blk = lax.broadcasted_iota(jnp.int32, (nblk, rows), 0)
    chosen = _topk_mask_rows(jnp.where(blk < c, score, NEG_INF), MOBA_TOPK) & (blk < c)
    chosen = jnp.concatenate([jnp.where(chosen, 1.0, 0.0), jnp.zeros((LANES - nblk, rows), F32)], axis=0)
    past = jnp.transpose(chosen) > 0.5
    lane2 = lax.broadcasted_iota(jnp.int32, (rows, LANES), 1)
    lhs_ref[:, LANES:2 * LANES] = jnp.where(past & (lane2 < c - 1), 0.0, NEG_INF).astype(BF16)

    def write_near_mask():
        lhs_ref[:, LANES:2 * LANES] = jnp.where(past | (lane2 == c), 0.0, NEG_INF).astype(BF16)

    o = _pipelined_attention(c, MOBA_BLOCK, npair, lhs_ref, write_near_mask, k_ref, v_ref, dp_ref, dd_ref,
                             (s0_ref, s1_ref), (p0_ref, p1_ref), (a0_ref, a1_ref), m_ref, acc_ref)
    for p in range(npair):
        o0 = o[2 * p * tq:(2 * p + 1) * tq]
        o1 = o[(2 * p + 1) * tq:(2 * p + 2) * tq]
        o_ref[:, p * LANES:(p + 1) * LANES] = jnp.where(half == 0, o0, o1).astype(o_ref.dtype)


def _moba_attention(proj, kmean, dprev, ddiag):
    b, s, _ = proj.shape
    tq = ATT_TILE
    rows = MOBA_HEADS * tq

    def col(name):
        return SEC[name][0] // 512

    return pl.pallas_call(
        _moba_kernel,
        grid=(b, s // tq),
        in_specs=[pl.BlockSpec((None, tq, 512), lambda i, c: (i, c, col("c_q"))),
                  _const_spec((None, s, 512), lambda i, c: (i, 0, col("c_k"))),
                  _const_spec((None, s, 512), lambda i, c: (i, 0, col("c_v"))),
                  _const_spec((None, LANES, 512), lambda i, c: (i, 0, 0)),
                  _const_spec((rows, tq), lambda i, c: (0, 0)),
                  _const_spec((rows, tq), lambda i, c: (0, 0))],
        out_specs=pl.BlockSpec((None, tq, 512), lambda i, c: (i, c, 0)),
        out_shape=jax.ShapeDtypeStruct((b, s, 512), BF16),
        scratch_shapes=[pltpu.VMEM((rows, 2 * LANES), BF16),
                        pltpu.VMEM((rows, LANES), F32), pltpu.VMEM((rows, 2 * LANES), F32),
                        pltpu.VMEM((rows, FAR_TILE), F32), pltpu.VMEM((rows, FAR_TILE), F32),
                        pltpu.VMEM((rows, FAR_TILE), BF16), pltpu.VMEM((rows, FAR_TILE), BF16),
                        pltpu.VMEM((rows, LANES), F32), pltpu.VMEM((rows, LANES), F32)],
        compiler_params=_cparams(("parallel", "arbitrary")),
        name="moba_attention",
    )(proj, proj, proj, kmean, dprev, ddiag)


def _merge_kernel(oa_ref, ob_ref, oc_ref, g0_ref, g1_ref, g2_ref, x_ref, wa_ref, wb_ref, wc_ref, wo_ref, nw_ref,
                  o_ref):
    def branch(o, w, g):
        return jax.nn.sigmoid(g[...].astype(F32)) * jnp.dot(o[...], w[...], preferred_element_type=F32)

    merged = branch(oa_ref, wa_ref, g0_ref) + branch(ob_ref, wb_ref, g1_ref) + branch(oc_ref, wc_ref, g2_ref)
    y = jnp.dot(merged.astype(BF16), wo_ref[...], preferred_element_type=F32)
    o_ref[...] = x_ref[...] + _rms(y, nw_ref[...])


def _merge(oa, ob, oc, proj, x, wa, wb, wc, wo, nw, tm):
    t, d = x.shape
    w = oa.shape[1]
    row = lambda width, j=0: pl.BlockSpec((tm, width), lambda i, j=j: (i, j))
    const = lambda shape: pl.BlockSpec(shape, lambda i: (0, 0))
    return pl.pallas_call(
        _merge_kernel,
        grid=(t // tm,),
        in_specs=[row(w), row(w), row(w), row(d, 0), row(d, 1), row(d, 2), row(d),
                  const((w, d)), const((w, d)), const((w, d)), const((d, d)), const((1, d))],
        out_specs=row(d),
        out_shape=jax.ShapeDtypeStruct((t, d), F32),
        compiler_params=_cparams(("parallel",)),
        name="merge_out",
    )(oa, ob, oc, proj, proj, proj, x, wa, wb, wc, wo, nw.reshape(1, d))


def _ffn_kernel(x_ref, npre_ref, wi_ref, wo_ref, npost_ref, o_ref):
    h = _rms(x_ref[...], npre_ref[...]).astype(BF16)
    hid = wo_ref.shape[0]
    acc = jnp.zeros(o_ref.shape, F32)
    for lo in range(0, hid, FFN_CHUNK):
        hi = min(lo + FFN_CHUNK, hid)
        gate = jnp.dot(h, wi_ref[:, lo:hi], preferred_element_type=F32)
        up = jnp.dot(h, wi_ref[:, hid + lo:hid + hi], preferred_element_type=F32)
        act = (gate * jax.nn.sigmoid(gate) * up).astype(BF16)
        acc = acc + jnp.dot(act, wo_ref[lo:hi, :], preferred_element_type=F32)
    o_ref[...] = x_ref[...] + _rms(acc, npost_ref[...])


def _ffn(x, npre, wi, wo, npost, tm):
    t, d = x.shape
    hid = wo.shape[0]
    assert hid % LANES == 0
    return pl.pallas_call(
        _ffn_kernel,
        grid=(t // tm,),
        in_specs=[pl.BlockSpec((tm, d), lambda i: (i, 0)),
                  pl.BlockSpec((1, d), lambda i: (0, 0)),
                  _const_spec((d, 2 * hid), lambda i: (0, 0)),
                  _const_spec((hid, d), lambda i: (0, 0)),
                  pl.BlockSpec((1, d), lambda i: (0, 0))],
        out_specs=pl.BlockSpec((tm, d), lambda i: (i, 0)),
        out_shape=jax.ShapeDtypeStruct((t, d), F32),
        compiler_params=_cparams(("parallel",)),
        name="ffn",
    )(x, npre.reshape(1, d), wi, wo, npost.reshape(1, d))


def _t5_bucket_table(dist):
    n = np.maximum(dist, 0)
    max_exact = NUM_BUCKETS // 2
    log_ratio = np.log(np.maximum(n, 1).astype(np.float32) / max_exact) / math.log(T5_MAX_DISTANCE / max_exact)
    large = max_exact + (log_ratio * (NUM_BUCKETS - max_exact)).astype(np.int32)
    return np.where(n < max_exact, n, np.minimum(large, NUM_BUCKETS - 1)).astype(np.int32)


def _bias_lookup(rel, dist):
    onehot = (jnp.asarray(_t5_bucket_table(dist))[..., None] == jnp.arange(NUM_BUCKETS)).astype(F32)
    return jnp.dot(onehot, (rel - rel[NUM_BUCKETS - 1]) * LOG2E, precision=lax.Precision.HIGHEST)


def _near_bias(rel, t):
    qi = np.arange(t)[:, None]
    ki = np.arange(t)[None, :]
    prev = _bias_lookup(rel, qi + t - ki)
    diag = jnp.where((qi >= ki)[..., None], _bias_lookup(rel, qi - ki), NEG_INF)
    flat = lambda b: b.transpose(2, 0, 1).reshape(-1, t)
    return flat(prev), flat(diag)


def _cmp_bias(rel, t):
    w = t // CMP_STRIDE
    qi = np.arange(t)[:, None]
    j = np.arange(LANES)[None, :]
    d = qi - CMP_STRIDE * (j - w) - (CMP_LEN - 1)
    b = jnp.where(((j < 2 * w) & (d >= 0))[..., None], _bias_lookup(rel, d),
                  jnp.asarray(np.where(j <= 2 * w, NEG_INF, 0.0) * np.ones_like(d), F32)[..., None])
    return b.transpose(2, 0, 1).reshape(-1, LANES).astype(BF16)


def _overlap(s, nc):
    n_cmp = (s - CMP_LEN) // CMP_STRIDE + 1
    cmp_end = np.arange(nc) * CMP_STRIDE + CMP_LEN - 1
    cmp_start = cmp_end - (CMP_LEN - 1)
    sb_start = np.arange(LANES) * SEL_BLOCK
    ov = (cmp_start[:, None] < sb_start[None, :] + SEL_BLOCK) & (cmp_end[:, None] >= sb_start[None, :])
    ov = ov & (np.arange(nc)[:, None] < n_cmp) & (np.arange(LANES)[None, :] < s // SEL_BLOCK)
    return jnp.asarray(ov, BF16)


def _pair_diag(w):
    z = jnp.zeros_like(w)
    return jnp.concatenate([jnp.concatenate([w, z], axis=-1), jnp.concatenate([z, w], axis=-1)], axis=-2)


def kernel(x, rel_bias, norm_mix_pre, norm_mix_post, norm_ffn_pre, norm_ffn_post, w_in, nsa_pe_k, nsa_pe_v, nsa_cmp_k_w1, nsa_cmp_k_w2, nsa_cmp_v_w1, nsa_cmp_v_w2, gla_gate_w2, gla_gate_b, gla_norm, w_branch_a, w_branch_b, w_branch_c, w_out, w_ffn_in, w_ffn_out):
    b, s, d = x.shape
    depth = w_in.shape[0]
    t = b * s
    dk = NSA_HEAD_DIM
    tq = ATT_TILE
    assert d == D_MODEL and w_in.shape[2] == D_IN
    assert WINDOW == 2 * tq and MOBA_BLOCK == tq and s % tq == 0
    assert SEL_TOPK <= s // SEL_BLOCK <= LANES and s // MOBA_BLOCK <= LANES
    nc = s // CMP_STRIDE
    assert nc % LANES == 0

    w_in_p = _layout_w_in(w_in)
    rel_a = rel_bias[:, :NSA_HEADS]
    rel_c = rel_bias[:, NSA_HEADS:]
    dprev_a, ddiag_a = _near_bias(rel_a, tq)
    dprev_c, ddiag_c = _near_bias(rel_c, tq)
    tcmp = _cmp_bias(rel_a, tq)
    overlap = _overlap(s, nc)
    wtab = jnp.asarray(np.where(np.arange(tq)[None, :] > np.arange(tq)[:, None], 0.0, NEG_INF), F32)
    avg = jnp.asarray((np.arange(LANES)[:, None] == np.arange(s)[None, :] // MOBA_BLOCK) / MOBA_BLOCK, BF16)

    def cmp_w1(w1):
        w = _pair_diag(w1.reshape(depth, 2, CMP_STRIDE, dk, dk))
        return w.reshape(depth, 2, CMP_STRIDE * 2 * dk, 2 * dk)

    def cmp_pe(pe):
        p2 = jnp.concatenate([pe, pe], axis=-1).reshape(depth, 2, 1, CMP_STRIDE * 2 * dk)
        return jnp.broadcast_to(p2, (depth, 2, 8, CMP_STRIDE * 2 * dk))

    cw1 = jnp.stack([cmp_w1(nsa_cmp_k_w1), cmp_w1(nsa_cmp_v_w1)], axis=1).astype(BF16)
    cw2 = jnp.stack([_pair_diag(nsa_cmp_k_w2), _pair_diag(nsa_cmp_v_w2)], axis=1).astype(BF16)
    cpe = jnp.stack([cmp_pe(nsa_pe_k), cmp_pe(nsa_pe_v)], axis=1).astype(BF16)

    gw2 = jnp.concatenate([gla_gate_w2, jnp.zeros((depth, LANES - GLA_GATE_RANK, gla_gate_w2.shape[2]), F32)],
                          axis=1).astype(BF16)
    wa = (w_branch_a.reshape(depth, NSA_KV_GROUPS, NSA_REP, dk, d).transpose(0, 2, 1, 3, 4)
          .reshape(depth, NSA_HEADS * dk, d).astype(BF16))
    wb = w_branch_b.astype(BF16)
    wc = w_branch_c.astype(BF16)
    wo = w_out.astype(BF16)
    wi = w_ffn_in.astype(BF16)
    wf = w_ffn_out.astype(BF16)

    xf = x.reshape(t, d)
    for layer in range(depth):
        proj = _norm_matmul(xf, norm_mix_pre[layer], w_in_p[layer], min(512, t)).reshape(b, s, NP_COLS)
        kc0, vc0 = SEC["a_kc"][0], SEC["a_vc"][0]
        xkv = jnp.stack([proj[:, :, kc0:kc0 + LANES], proj[:, :, vc0:vc0 + LANES]])
        xkv = xkv.reshape(2, b, nc, CMP_STRIDE * LANES)
        cmp = _compress(xkv, cpe[layer], cw1[layer], cw2[layer])
        o_a = _nsa_attention(proj, cmp[0], cmp[1], overlap, tcmp, dprev_a, ddiag_a, wtab)
        o_b = _gla(proj, gw2[layer], gla_gate_b[layer].reshape(1, -1), gla_norm[layer].reshape(1, -1),
                   min(256, s))
        kmean = _moba_kmean(proj, avg)
        o_c = _moba_attention(proj, kmean, dprev_c, ddiag_c)
        xf = _merge(o_a.reshape(t, -1), o_b.reshape(t, -1), o_c.reshape(t, -1), proj.reshape(t, NP_COLS), xf,
                    wa[layer], wb[layer], wc[layer], wo[layer], norm_mix_post[layer], min(512, t))
        xf = _ffn(xf, norm_ffn_pre[layer], wi[layer], wf[layer], norm_ffn_post[layer], min(512, t))
    return xf.reshape(b, s, d)
```
